```python
import math
import jax
import jax.numpy as jnp
from jax import lax
import numpy as np

D_MODEL = 1024
BATCH = 8
SEQ = 8192
DEPTH = 2

GRID_W = 64
CTX_LEN = 256

RWKV_HEAD = 64
RWKV_WIDTH = D_MODEL
RWKV_HEADS = RWKV_WIDTH // RWKV_HEAD
LORA_W = 64
LORA_A = 64
LORA_G = 160
GN_EPS = RWKV_HEAD * 1e-5

CONV_CH = D_MODEL
CONV_K = 31

PEER_HEADS = 8
PEER_NKEYS = 128
PEER_EXPERTS = PEER_NKEYS * PEER_NKEYS
PEER_QDIM = 256
PEER_HALF = PEER_QDIM // 2
PEER_TOPK = 16
PEER_CHUNK = 256

N_BRANCH = 2
NORM_EPS = 1e-6
LN_EPS = 1e-5

P_RWKV = 3 * RWKV_WIDTH + 2 * LORA_W + 2 * LORA_A + LORA_G
P_CONV = 2 * CONV_CH
P_GATE = N_BRANCH * D_MODEL
P_IN = P_RWKV + P_CONV + P_GATE
RWKV_SPLITS = (RWKV_WIDTH, 2 * RWKV_WIDTH, 3 * RWKV_WIDTH,
               3 * RWKV_WIDTH + LORA_W, 3 * RWKV_WIDTH + 2 * LORA_W,
               3 * RWKV_WIDTH + 2 * LORA_W + LORA_A, 3 * RWKV_WIDTH + 2 * LORA_W + 2 * LORA_A)

kernel_name = 'hybrid_rwkv7_conformer_peer_flow_block'


def rmsnorm(x, g):
    xf = x.astype(jnp.float32)
    y = xf * lax.rsqrt(jnp.mean(xf * xf, axis=-1, keepdims=True) + NORM_EPS)
    return (y * g.astype(jnp.float32)).astype(x.dtype)


def layernorm(x, g, b):
    xf = x.astype(jnp.float32)
    mu = jnp.mean(xf, axis=-1, keepdims=True)
    var = jnp.mean(jnp.square(xf - mu), axis=-1, keepdims=True)
    return ((xf - mu) * lax.rsqrt(var + LN_EPS) * g + b).astype(x.dtype)


def modulate(x, g, shift, scale):
    return rmsnorm(x, g) * (1 + scale) + shift


def token_shift(z, mu):
    zp = jnp.pad(z[:, :-1], ((0, 0), (1, 0), (0, 0)))
    zn = jnp.pad(z[:, 1:], ((0, 0), (0, 1), (0, 0)))
    return z + mu[0] * (zp - z) + mu[1] * (zn - z)


def rwkv_scan(decay, k, v, kk, a, r, s0, reverse):
    emit = r is not None
    seq = (decay, k, v, kk, a) + ((r,) if emit else ())
    xs = tuple(jnp.moveaxis(t.astype(jnp.float32), 1, 0) for t in seq)

    def step(S, inp):
        w_t, k_t, v_t, kk_t, a_t = inp[:5]
        sa = jnp.einsum('bhvk,bhk->bhv', S, kk_t)
        S = (S * w_t[:, :, None, :]
             - sa[..., None] * (kk_t * a_t)[:, :, None, :]
             + v_t[..., None] * k_t[:, :, None, :])
        y = jnp.einsum('bhvk,bhk->bhv', S, inp[5]) if emit else None
        return S, y

    S, ys = lax.scan(step, s0, xs, reverse=reverse)
    return S, (jnp.moveaxis(ys, 0, 1) if emit else None)


def rwkv_branch(zr, p, s0, emit):
    B, L, _ = zr.shape
    H, N = RWKV_HEADS, RWKV_HEAD
    r, k, v, w1f, w1b, a1f, a1b, g1 = jnp.split(zr, RWKV_SPLITS, axis=-1)
    heads = lambda t: t.reshape(B, L, H, N)
    kkr = heads(k * p['k_k']).astype(jnp.float32)
    kk = kkr / jnp.maximum(jnp.sqrt(jnp.sum(kkr * kkr, axis=-1, keepdims=True)), 1e-12)
    states, outs, bonus = [], [], []
    for d, (w1d, a1d) in enumerate(((w1f, a1f), (w1b, a1b))):
        wlog = -jax.nn.softplus(-(p['w0'][d] + jnp.tanh(w1d) @ p['w2'][d])) - 0.5
        decay = jnp.exp(-jnp.exp(wlog.astype(jnp.float32)))
        a = jax.nn.sigmoid(a1d @ p['a2'][d] + p['a0'][d])
        kd = k * (1 + (a - 1) * p['k_a'])
        S, y = rwkv_scan(heads(decay), heads(kd), heads(v), kk, heads(a),
                         heads(r) if emit else None, s0[d], reverse=(d == 1))
        states.append(S)
        if emit:
            outs.append(y)
            bonus.append(jnp.sum(heads(r * kd) * p['r_k'], axis=-1, keepdims=True))
    if not emit:
        return None, (states[0], states[1])
    o = outs[0] + outs[1]
    mu = jnp.mean(o, axis=-1, keepdims=True)
    var = jnp.mean(jnp.square(o - mu), axis=-1, keepdims=True)
    o = ((o - mu) * lax.rsqrt(var + GN_EPS)).reshape(B, L, RWKV_WIDTH) * p['lnx_g'] + p['lnx_b']
    o = o + ((bonus[0] + bonus[1]).astype(jnp.float32) * heads(v).astype(jnp.float32)).reshape(B, L, RWKV_WIDTH)
    g = jax.nn.sigmoid(g1) @ p['g2']
    y = (o.astype(zr.dtype) * g) @ p['w_oA']
    return y, (states[0], states[1])


def conv_branch(zc, p, rows):
    B, L, _ = zc.shape
    u = zc[..., :CONV_CH] * jax.nn.sigmoid(zc[..., CONV_CH:])
    img = u.reshape(B, rows, GRID_W, CONV_CH) if rows is not None else u[:, :, None, :]
    rhs = p['conv_w'][:, None, None, :].astype(img.dtype)
    out = lax.conv_general_dilated(img, rhs, window_strides=(1, 1),
                                   padding=((CONV_K // 2, CONV_K // 2), (0, 0)),
                                   dimension_numbers=('NHWC', 'HWIO', 'NHWC'),
                                   feature_group_count=CONV_CH)
    out = layernorm(out.reshape(B, L, CONV_CH), p['cnorm_g'], p['cnorm_b'])
    return jax.nn.silu(out) @ p['w_oB']


def token_mixer(h, p, rows, s0, emit):
    w_in = p['w_in'] if emit else p['w_in'][:, :P_RWKV]
    z = h @ w_in
    zr = token_shift(z[..., :P_RWKV], p['shift_mu'])
    y_a, states = rwkv_branch(zr, p, s0, emit)
    if not emit:
        return None, states
    y_b = conv_branch(z[..., P_RWKV:P_RWKV + P_CONV], p, rows)
    gates = jax.nn.sigmoid(z[..., P_RWKV + P_CONV:] + p['gate_b'])
    m = gates[..., :D_MODEL] * y_a + gates[..., D_MODEL:] * y_b
    return m @ p['w_out'], states


def peer(h, p):
    B, L, D = h.shape
    T = B * L
    chunk = math.gcd(T, PEER_CHUNK)
    hb = h.reshape(T // chunk, chunk, D)

    def block(hc):
        q = (hc @ p['w_q']).reshape(chunk, PEER_HEADS, 2, PEER_HALF)
        s = jnp.einsum('thcd,hcnd->thcn', q, p['sub_keys']).astype(jnp.float32)
        s1, i1 = lax.top_k(s[:, :, 0], PEER_TOPK)
        s2, i2 = lax.top_k(s[:, :, 1], PEER_TOPK)
        cand = (s1[..., :, None] + s2[..., None, :]).reshape(chunk, PEER_HEADS, PEER_TOPK * PEER_TOPK)
        cidx = (i1[..., :, None] * PEER_NKEYS + i2[..., None, :]).reshape(chunk, PEER_HEADS, PEER_TOPK * PEER_TOPK)
        best, pos = lax.top_k(cand, PEER_TOPK)
        eidx = jnp.take_along_axis(cidx, pos, axis=-1)
        gw = jax.nn.softmax(best, axis=-1)
        u = jnp.take(p['peer_u'], eidx, axis=0)
        act = jax.nn.gelu(jnp.einsum('td,thkd->thk', hc, u).astype(jnp.float32), approximate=False) * gw
        v = jnp.take(p['peer_v'], eidx, axis=0)
        return jnp.einsum('thk,thkd->td', act.astype(hc.dtype), v)

    return lax.map(block, hb).reshape(B, L, D)


def setup_inputs(seed: int = 0) -> dict:
    key = jax.random.key(seed)
    ks = jax.random.split(key, 40)
    f32 = jnp.float32
    nrm = lambda k, shape, s: jax.random.normal(k, shape, f32) * s
    Ly, D, W, H = DEPTH, D_MODEL, RWKV_WIDTH, RWKV_HEADS
    return {
        'x': nrm(ks[0], (BATCH, SEQ, D), 1.0),
        'c': nrm(ks[1], (BATCH, D), 1.0),
        'ctx': nrm(ks[2], (BATCH, CTX_LEN, D), 1.0),
        'c_ctx': nrm(ks[3], (D,), 1.0),
        'ada_w': nrm(ks[4], (Ly, D, 6 * D), 0.5 * D ** -0.5),
        'ada_b': nrm(ks[5], (Ly, 6 * D), 0.02),
        'norm1_g': 1.0 + nrm(ks[6], (Ly, D), 0.02),
        'norm2_g': 1.0 + nrm(ks[7], (Ly, D), 0.02),
        'w_in': nrm(ks[8], (Ly, D, P_IN), D ** -0.5),
        'shift_mu': jax.random.uniform(ks[9], (Ly, 2, P_RWKV), f32, 0.0, 0.5),
        'w0': jax.random.uniform(ks[10], (Ly, 2, W), f32, -6.0, 1.0),
        'w2': nrm(ks[11], (Ly, 2, LORA_W, W), 0.5 * LORA_W ** -0.5),
        'a0': nrm(ks[12], (Ly, 2, W), 0.5),
        'a2': nrm(ks[13], (Ly, 2, LORA_A, W), 0.5 * LORA_A ** -0.5),
        'g2': nrm(ks[14], (Ly, LORA_G, W), LORA_G ** -0.5),
        'k_k': 0.85 + nrm(ks[15], (Ly, W), 0.05),
        'k_a': 1.0 + nrm(ks[16], (Ly, W), 0.05),
        'r_k': nrm(ks[17], (Ly, H, RWKV_HEAD), 0.1),
        'lnx_g': 1.0 + nrm(ks[18], (Ly, W), 0.02),
        'lnx_b': nrm(ks[19], (Ly, W), 0.02),
        'w_oA': nrm(ks[20], (Ly, W, D), W ** -0.5),
        'conv_w': nrm(ks[21], (Ly, CONV_K, CONV_CH), CONV_K ** -0.5),
        'cnorm_g': 1.0 + nrm(ks[22], (Ly, CONV_CH), 0.02),
        'cnorm_b': nrm(ks[23], (Ly, CONV_CH), 0.02),
        'w_oB': nrm(ks[24], (Ly, CONV_CH, D), CONV_CH ** -0.5),
        'gate_b': nrm(ks[25], (Ly, P_GATE), 0.02),
        'w_out': nrm(ks[26], (Ly, D, D), D ** -0.5),
        'w_q': nrm(ks[27], (Ly, D, PEER_HEADS * PEER_QDIM), D ** -0.5),
        'sub_keys': nrm(ks[28], (Ly, PEER_HEADS, 2, PEER_NKEYS, PEER_HALF), PEER_HALF ** -0.5),
        'peer_u': nrm(ks[29], (Ly, PEER_EXPERTS, D), D ** -0.5),
        'peer_v': nrm(ks[30], (Ly, PEER_EXPERTS, D), 0.5),
        'final_g': 1.0 + nrm(ks[31], (D,), 0.02),
    }


def reference(x, c, ctx, c_ctx, ada_w, ada_b, norm1_g, norm2_g, w_in, shift_mu, w0, w2, a0, a2, g2,
              k_k, k_a, r_k, lnx_g, lnx_b, w_oA, conv_w, cnorm_g, cnorm_b, w_oB, gate_b, w_out,
              w_q, sub_keys, peer_u, peer_v, final_g):
    B = x.shape[0]
    rows = x.shape[1] // GRID_W
    xc = ctx
    zero_state = jnp.zeros((B, RWKV_HEADS, RWKV_HEAD, RWKV_HEAD), jnp.float32)
    for l in range(DEPTH):
        last = l == DEPTH - 1
        p = dict(w_in=w_in[l], shift_mu=shift_mu[l], w0=w0[l], w2=w2[l], a0=a0[l], a2=a2[l],
                 g2=g2[l], k_k=k_k[l], k_a=k_a[l], r_k=r_k[l], lnx_g=lnx_g[l], lnx_b=lnx_b[l],
                 w_oA=w_oA[l], conv_w=conv_w[l], cnorm_g=cnorm_g[l], cnorm_b=cnorm_b[l],
                 w_oB=w_oB[l], gate_b=gate_b[l], w_out=w_out[l], w_q=w_q[l],
                 sub_keys=sub_keys[l], peer_u=peer_u[l], peer_v=peer_v[l])
        mod = jax.nn.silu(c) @ ada_w[l] + ada_b[l]
        sh1, sc1, gt1, sh2, sc2, gt2 = jnp.split(mod[:, None, :], 6, axis=-1)
        modc = jax.nn.silu(c_ctx) @ ada_w[l] + ada_b[l]
        csh1, csc1, cgt1, csh2, csc2, cgt2 = jnp.split(modc, 6, axis=-1)

        hc = modulate(xc, norm1_g[l], csh1, csc1)
        yc, ctx_states = token_mixer(hc, p, None, (zero_state, zero_state), emit=not last)

        h = modulate(x, norm1_g[l], sh1, sc1)
        y, _ = token_mixer(h, p, rows, ctx_states, emit=True)
        x = x + gt1 * y
        x = x + gt2 * peer(modulate(x, norm2_g[l], sh2, sc2), p)

        if not last:
            xc = xc + cgt1 * yc
            xc = xc + cgt2 * peer(modulate(xc, norm2_g[l], csh2, csc2), p)
    return rmsnorm(x, final_g)
```

```python
import functools

import jax
import jax.numpy as jnp
from jax import lax
from jax.experimental import pallas as pl
from jax.experimental.pallas import tpu as pltpu

F32 = jnp.float32
BF16 = jnp.bfloat16
HI = lax.Precision.HIGHEST

D_MODEL = 1024
HEAD = 64
HEADS = D_MODEL // HEAD
GROUP_HEADS = 4
GROUP_W = GROUP_HEADS * HEAD
N_GROUPS = HEADS // GROUP_HEADS
CHUNK = 64
LORA_W = 64
LORA_A = 64
LORA_G = 160
LORA_G_PAD = 256
P_RWKV = 3 * D_MODEL + 2 * LORA_W + 2 * LORA_A + LORA_G
P_RWKV_PAD = 3 * D_MODEL + 2 * LORA_W + 2 * LORA_A + LORA_G_PAD
COL_W1 = 3 * D_MODEL
COL_A1 = COL_W1 + 2 * LORA_W
COL_G1 = COL_A1 + 2 * LORA_A
CONV_K = 31
CONV_HALF = CONV_K // 2
GRID_W = 64
PEER_HEADS = 8
PEER_NKEYS = 128
PEER_HALF = 128
PEER_TOPK = 16
NORM_EPS = 1e-6
LN_EPS = 1e-5
GN_EPS = HEAD * 1e-5
VMEM_LIMIT = 56 * 1024 * 1024
NOT_SELECTED = 99.0
NEG_INF = float("-inf")
SQRT_HALF = 0.7071067811865476


def _cp(*sem):
    return pltpu.CompilerParams(dimension_semantics=sem, vmem_limit_bytes=VMEM_LIMIT)


def _dot(a, b):
    return jnp.dot(a, b, preferred_element_type=F32)


def _dot_hi(a, b):
    return jnp.dot(a, b, precision=HI, preferred_element_type=F32)


def _dot_nt(a, b):
    return lax.dot_general(a, b, (((1,), (1,)), ((), ())), preferred_element_type=F32)


def _dot_tn(a, b):
    return lax.dot_general(a, b, (((0,), (0,)), ((), ())), preferred_element_type=F32)


def _bf(a):
    return a.astype(BF16)


def _split(a):
    hi = a.astype(BF16)
    return hi, (a - hi.astype(F32)).astype(BF16)


def _sigmoid(x):
    return 1.0 / (1.0 + jnp.exp(-x))


def _softplus(x):
    return jnp.maximum(x, 0.0) + jnp.log(1.0 + jnp.exp(-jnp.abs(x)))


def _rms(x, g):
    return x * lax.rsqrt(jnp.mean(x * x, axis=-1, keepdims=True) + NORM_EPS) * g


def _mod_kernel(c_ref, w_ref, b_ref, o_ref):
    c = c_ref[...]
    o_ref[...] = _dot_hi(c * _sigmoid(c), w_ref[...]) + b_ref[...]


def _modulation(c_rows, ada_w, ada_b):
    R, D = c_rows.shape
    N = ada_w.shape[1]
    TN = 512
    return pl.pallas_call(
        _mod_kernel,
        grid=(N // TN,),
        in_specs=[pl.BlockSpec((R, D), lambda j: (0, 0)),
                  pl.BlockSpec((D, TN), lambda j: (0, j)),
                  pl.BlockSpec((1, TN), lambda j: (0, j))],
        out_specs=pl.BlockSpec((R, TN), lambda j: (0, j)),
        out_shape=jax.ShapeDtypeStruct((R, N), F32),
        compiler_params=_cp("arbitrary"),
        name="modulation",
    )(c_rows, ada_w, ada_b.reshape(1, N))


def _proj_kernel(x_ref, sh_ref, sc_ref, g_ref, w_ref, o_ref):
    h = _rms(x_ref[0], g_ref[...]) * (1.0 + sc_ref[0]) + sh_ref[0]
    o_ref[0] = _dot(_bf(h), w_ref[...])


def _proj(x, sh, sc, g, w):
    B, L, D = x.shape
    N = w.shape[1]
    TM = min(L, 256)
    return pl.pallas_call(
        _proj_kernel,
        grid=(B, L // TM),
        in_specs=[pl.BlockSpec((1, TM, D), lambda b, i: (b, i, 0)),
                  pl.BlockSpec((1, 1, D), lambda b, i: (b, 0, 0)),
                  pl.BlockSpec((1, 1, D), lambda b, i: (b, 0, 0)),
                  pl.BlockSpec((1, D), lambda b, i: (0, 0)),
                  pl.BlockSpec((D, N), lambda b, i: (0, 0))],
        out_specs=pl.BlockSpec((1, TM, N), lambda b, i: (b, i, 0)),
        out_shape=jax.ShapeDtypeStruct((B, L, N), F32),
        compiler_params=_cp("parallel", "parallel"),
        name="proj",
    )(x, sh, sc, g.reshape(1, D), w)


_KAP, _RT, _KT, _BT, _KH, _BH, _V = range(7)


def _rwkv_kernel(z_ref, zp_ref, zn_ref, mu_ref, w0_ref, w2_ref, a0_ref, a2_ref, kk_ref, ka_ref,
                 rk_ref, hs_ref, hb_ref, tri_ref, msl_ref, minc_ref, h0_ref,
                 y_ref, bv_ref, gs_ref, hT_ref, H_scr, nat_scr, pc_scr, *, n_chunks):
    d = pl.program_id(0)
    c = pl.program_id(2)
    cc = jnp.where(d == 0, c, n_chunks - 1 - c)
    C = CHUNK

    @pl.when(c == 0)
    def _():
        H_scr[...] = h0_ref[0, 0]

    first = cc == 0
    last = cc == n_chunks - 1
    row = lax.broadcasted_iota(jnp.int32, (C, 1), 0)

    def shifted(lo, hi):
        z = z_ref[0, :, lo:hi]
        prev_row = jnp.where(first, 0.0, zp_ref[0, 7:8, lo:hi])
        next_row = jnp.where(last, 0.0, zn_ref[0, 0:1, lo:hi])
        zp = jnp.where(row == 0, prev_row, pltpu.roll(z, 1, 0))
        zn = jnp.where(row == C - 1, next_row, pltpu.roll(z, C - 1, 0))
        return z + mu_ref[0:1, lo:hi] * (zp - z) + mu_ref[1:2, lo:hi] * (zn - z)

    hs = hs_ref[...]
    hb = hb_ref[...]

    def head_sum(t):
        return _dot_hi(_dot_hi(t, hs), hb)

    r = shifted(0, D_MODEL)
    k = shifted(D_MODEL, 2 * D_MODEL)
    v = shifted(2 * D_MODEL, 3 * D_MODEL)
    w1 = shifted(COL_W1, COL_A1)
    a1 = shifted(COL_A1, COL_G1)
    g1 = shifted(COL_G1, P_RWKV_PAD)

    wl = w0_ref[0] + _dot_hi(jnp.tanh(w1), w2_ref[0])
    logw = -jnp.exp(-_softplus(-wl) - 0.5)
    a = _sigmoid(_dot_hi(a1, a2_ref[0]) + a0_ref[0])
    kkr = k * kk_ref[...]
    kk = kkr / jnp.maximum(jnp.sqrt(head_sum(kkr * kkr)), 1e-12)
    kd = k * (1.0 + (a - 1.0) * ka_ref[...])
    bb = kk * a
    bv_ref[0, 0] = head_sum(r * kd * rk_ref[...]) * v
    gs_ref[0, 0] = _sigmoid(g1)

    g_in = _dot_hi(tri_ref[0], logw)
    g_ex = g_in - logw
    g_c = jnp.sum(logw, axis=0, keepdims=True)
    e_inv = jnp.exp(-g_in)
    e_hat = jnp.exp(g_c - g_in)
    nat_scr[_KAP] = kk * jnp.exp(g_ex)
    nat_scr[_RT] = r * jnp.exp(g_in)
    nat_scr[_KT] = kd * e_inv
    nat_scr[_BT] = bb * e_inv
    nat_scr[_KH] = kd * e_hat
    nat_scr[_BH] = bb * e_hat
    nat_scr[_V] = v
    pc_scr[...] = jnp.exp(g_c)

    lane_head = lax.broadcasted_iota(jnp.int32, (C, GROUP_W), 1) // HEAD
    ii = lax.broadcasted_iota(jnp.int32, (GROUP_W, GROUP_W), 0)
    jj = lax.broadcasted_iota(jnp.int32, (GROUP_W, GROUP_W), 1)
    eye = ii == jj
    msl = msl_ref[0] > 0.5
    minc = minc_ref[0] > 0.5

    def stacked(t):
        return jnp.concatenate([jnp.where(lane_head == j, t, 0.0) for j in range(GROUP_HEADS)], axis=0)

    def collapse(t):
        return t[0:C] + t[C:2 * C] + t[2 * C:3 * C] + t[3 * C:4 * C]

    def group(g, carry):
        off = pl.multiple_of(g * GROUP_W, GROUP_W)

        def nat(i):
            return nat_scr[i, :, pl.ds(off, GROUP_W)]

        x_kap = stacked(nat(_KAP))
        x_r = stacked(nat(_RT))
        x_v = _bf(stacked(nat(_V)))
        x_bk = jnp.concatenate([_bf(stacked(nat(_BT))), _bf(stacked(nat(_KT)))], axis=0)
        a1m = _dot_nt(_bf(x_kap), x_bk)
        a2m = _dot_nt(_bf(x_r), x_bk)
        n_mat = jnp.where(msl, a1m[:, :GROUP_W], 0.0)
        a_kk = jnp.where(msl, a1m[:, GROUP_W:], 0.0)
        a_rb = jnp.where(minc, a2m[:, :GROUP_W], 0.0)
        a_rk = jnp.where(minc, a2m[:, GROUP_W:], 0.0)
        p_inv = jnp.where(eye, 1.0, 0.0) - n_mat
        n_pow = n_mat
        for _ in range(5):
            nb = _bf(n_pow)
            n_pow = _dot(nb, nb)
            p_inv = p_inv + _dot(_bf(p_inv), _bf(n_pow))
        g0 = _dot(_bf(a_kk), x_v)
        w12 = _dot(_bf(p_inv), jnp.concatenate([_bf(x_kap), _bf(g0)], axis=1))
        w12b = _bf(w12)
        aw = _dot(_bf(a_rb), w12b)
        q_hat = x_r - aw[:, :GROUP_W]
        y0 = _dot(_bf(a_rk), x_v) - aw[:, GROUP_W:]
        bw = _dot_tn(_bf(stacked(nat(_BH))), w12b)
        m_mat = jnp.where(eye, pc_scr[:, pl.ds(off, GROUP_W)], 0.0) - bw[:, :GROUP_W]
        j_mat = _dot_tn(_bf(stacked(nat(_KH))), x_v) - bw[:, GROUP_W:]
        h_hi, h_lo = _split(H_scr[g])
        qb = _bf(collapse(q_hat))
        y_ref[0, 0, :, pl.ds(off, GROUP_W)] = _dot(qb, h_hi) + _dot(qb, h_lo) + collapse(y0)
        m_hi, m_lo = _split(m_mat)
        H_scr[g] = _dot(m_hi, h_hi) + _dot(m_hi, h_lo) + _dot(m_lo, h_hi) + j_mat
        return carry

    lax.fori_loop(0, N_GROUPS, group, 0)

    @pl.when(c == n_chunks - 1)
    def _():
        hT_ref[0, 0] = H_scr[...]


def _rwkv(z, pr, h0):
    B, L, N = z.shape
    C = CHUNK
    nC = L // C
    D = D_MODEL
    nb8 = L // 8

    def cidx(d, c):
        return c + d * (nC - 1 - 2 * c)

    zmap = lambda d, b, c: (b, cidx(d, c), 0)
    pmap = lambda d, b, c: (b, jnp.maximum(cidx(d, c) * (C // 8) - 1, 0), 0)
    nmap = lambda d, b, c: (b, jnp.minimum((cidx(d, c) + 1) * (C // 8), nb8 - 1), 0)
    const2 = lambda d, b, c: (0, 0)
    dir3 = lambda d, b, c: (d, 0, 0)
    omap = lambda d, b, c: (d, b, cidx(d, c), 0)
    smap = lambda d, b, c: (d, b, 0, 0, 0)
    kern = functools.partial(_rwkv_kernel, n_chunks=nC)
    return pl.pallas_call(
        kern,
        grid=(2, B, nC),
        in_specs=[pl.BlockSpec((1, C, N), zmap),
                  pl.BlockSpec((1, 8, N), pmap),
                  pl.BlockSpec((1, 8, N), nmap),
                  pl.BlockSpec((2, N), const2),
                  pl.BlockSpec((1, 1, D), dir3),
                  pl.BlockSpec((1, 2 * LORA_W, D), dir3),
                  pl.BlockSpec((1, 1, D), dir3),
                  pl.BlockSpec((1, 2 * LORA_A, D), dir3),
                  pl.BlockSpec((1, D), const2),
                  pl.BlockSpec((1, D), const2),
                  pl.BlockSpec((1, D), const2),
                  pl.BlockSpec((D, 128), const2),
                  pl.BlockSpec((128, D), const2),
                  pl.BlockSpec((1, C, C), dir3),
                  pl.BlockSpec((1, GROUP_W, GROUP_W), dir3),
                  pl.BlockSpec((1, GROUP_W, GROUP_W), dir3),
                  pl.BlockSpec((1, 1, N_GROUPS, GROUP_W, GROUP_W), smap)],
        out_specs=[pl.BlockSpec((1, 1, C, D), omap),
                   pl.BlockSpec((1, 1, C, D), omap),
                   pl.BlockSpec((1, 1, C, LORA_G_PAD), omap),
                   pl.BlockSpec((1, 1, N_GROUPS, GROUP_W, GROUP_W), smap)],
        out_shape=[jax.ShapeDtypeStruct((2, B, L, D), F32),
                   jax.ShapeDtypeStruct((2, B, L, D), F32),
                   jax.ShapeDtypeStruct((2, B, L, LORA_G_PAD), F32),
                   jax.ShapeDtypeStruct((2, B, N_GROUPS, GROUP_W, GROUP_W), F32)],
        scratch_shapes=[pltpu.VMEM((N_GROUPS, GROUP_W, GROUP_W), F32),
                        pltpu.VMEM((7, C, D), F32),
                        pltpu.VMEM((1, D), F32)],
        compiler_params=_cp("arbitrary", "arbitrary", "arbitrary"),
        name="rwkv",
    )(z, z, z, pr["mu"], pr["w0"], pr["w2p"], pr["a0"], pr["a2p"], pr["k_k"], pr["k_a"], pr["r_k"],
      pr["hs"], pr["hb"], pr["tri"], pr["msl"], pr["minc"], h0)


def _conv_kernel(za_ref, zb_ref, w_ref, o_ref, upad, *, L, stride, rows_per_step):
    pad = CONV_HALF * stride
    TC = za_ref.shape[-1]
    upad[0:pad, :] = jnp.zeros((pad, TC), F32)
    upad[pad + L:pad + L + pad, :] = jnp.zeros((pad, TC), F32)
    upad[pad:pad + L, :] = za_ref[0] * _sigmoid(zb_ref[0])
    RB = rows_per_step

    def block(r0):
        acc = jnp.zeros((RB, TC), F32)
        for j in range(CONV_K):
            acc = acc + w_ref[j:j + 1, :] * upad[pl.ds(r0 + j * stride, RB), :]
        o_ref[0, pl.ds(r0, RB), :] = acc

    if stride % 8 == 0:
        def body(i, carry):
            block(pl.multiple_of(i * RB, RB))
            return carry
        lax.fori_loop(0, L // RB, body, 0)
    else:
        for i in range(L // RB):
            block(i * RB)


def _conv(zcg, conv_w, stride):
    B, L, _ = zcg.shape
    D = D_MODEL
    TC = 128
    nct = D // TC
    wpad = jnp.pad(conv_w, ((0, 32 - CONV_K), (0, 0)))
    RB = min(L, 128)
    kern = functools.partial(_conv_kernel, L=L, stride=stride, rows_per_step=RB)
    return pl.pallas_call(
        kern,
        grid=(B, nct),
        in_specs=[pl.BlockSpec((1, L, TC), lambda b, j: (b, 0, j)),
                  pl.BlockSpec((1, L, TC), lambda b, j: (b, 0, j + nct)),
                  pl.BlockSpec((32, TC), lambda b, j: (0, j))],
        out_specs=pl.BlockSpec((1, L, TC), lambda b, j: (b, 0, j)),
        out_shape=jax.ShapeDtypeStruct((B, L, D), F32),
        scratch_shapes=[pltpu.VMEM((L + 2 * CONV_HALF * stride, TC), F32)],
        compiler_params=_cp("parallel", "parallel"),
        name="conv",
    )(zcg, zcg, wpad)


def _post_kernel(x_ref, yf_ref, yb_ref, bf_ref, bb_ref, gs_ref, cv_ref, zg_ref,
                 lng_ref, lnb_ref, g2_ref, woa_ref, cng_ref, cnb_ref, wob_ref, gb_ref, wout_ref,
                 gt_ref, n2g_ref, sh2_ref, sc2_ref, hs_ref, hb_ref, xo_ref, hn_ref):
    D = D_MODEL
    hs = hs_ref[...]
    hb = hb_ref[...]

    def head_mean(t):
        return _dot_hi(_dot_hi(t, hs), hb) * (1.0 / HEAD)

    o = yf_ref[0, 0] + yb_ref[0, 0]
    oc = o - head_mean(o)
    on = oc * lax.rsqrt(head_mean(oc * oc) + GN_EPS) * lng_ref[...] + lnb_ref[...]
    on = on + bf_ref[0, 0] + bb_ref[0, 0]
    gate = _dot(_bf(gs_ref[0, 0]), g2_ref[...])
    y_a = _dot(_bf(on * gate), woa_ref[...])

    cv = cv_ref[0]
    cm = jnp.mean(cv, axis=-1, keepdims=True)
    cc = cv - cm
    cn = cc * lax.rsqrt(jnp.mean(cc * cc, axis=-1, keepdims=True) + LN_EPS) * cng_ref[...] + cnb_ref[...]
    y_b = _dot(_bf(cn * _sigmoid(cn)), wob_ref[...])

    gates = _sigmoid(zg_ref[0] + gb_ref[...])
    m = gates[:, :D] * y_a + gates[:, D:] * y_b
    xn = x_ref[0] + gt_ref[0] * _dot(_bf(m), wout_ref[...])
    xo_ref[0] = xn
    hn_ref[0] = _bf(_rms(xn, n2g_ref[...]) * (1.0 + sc2_ref[0]) + sh2_ref[0])


def _post(x, y, bv, gs, cv, zcg, pr, gt1, sh2, sc2):
    B, L, D = x.shape
    TM = min(L, 256)
    tok = lambda b, i: (b, i, 0)
    fwd = lambda b, i: (0, b, i, 0)
    bwd = lambda b, i: (1, b, i, 0)
    cst = lambda b, i: (0, 0)
    per_b = lambda b, i: (b, 0, 0)
    row = pl.BlockSpec((1, D), cst)
    mat = pl.BlockSpec((D, D), cst)
    return pl.pallas_call(
        _post_kernel,
        grid=(B, L // TM),
        in_specs=[pl.BlockSpec((1, TM, D), tok),
                  pl.BlockSpec((1, 1, TM, D), fwd), pl.BlockSpec((1, 1, TM, D), bwd),
                  pl.BlockSpec((1, 1, TM, D), fwd), pl.BlockSpec((1, 1, TM, D), bwd),
                  pl.BlockSpec((1, 1, TM, LORA_G_PAD), fwd),
                  pl.BlockSpec((1, TM, D), tok),
                  pl.BlockSpec((1, TM, 2 * D), lambda b, i: (b, i, 1)),
                  row, row, pl.BlockSpec((LORA_G_PAD, D), cst), mat,
                  row, row, mat, pl.BlockSpec((1, 2 * D), cst), mat,
                  pl.BlockSpec((1, 1, D), per_b), row,
                  pl.BlockSpec((1, 1, D), per_b), pl.BlockSpec((1, 1, D), per_b),
                  pl.BlockSpec((D, 128), cst), pl.BlockSpec((128, D), cst)],
        out_specs=[pl.BlockSpec((1, TM, D), tok), pl.BlockSpec((1, TM, D), tok)],
        out_shape=[jax.ShapeDtypeStruct((B, L, D), F32), jax.ShapeDtypeStruct((B, L, D), BF16)],
        compiler_params=_cp("parallel", "parallel"),
        name="post",
    )(x, y, y, bv, bv, gs, cv, zcg,
      pr["lnx_g"], pr["lnx_b"], pr["g2p"], pr["w_oA"], pr["cnorm_g"], pr["cnorm_b"], pr["w_oB"],
      pr["gate_b"], pr["w_out"], gt1, pr["norm2_g"], sh2, sc2, pr["hs"], pr["hb"])


def _top16(s, rowid):
    rank = jnp.full(s.shape, NOT_SELECTED, F32)
    cur = s
    vals = []
    for r in range(PEER_TOPK):
        m = jnp.max(cur, axis=0, keepdims=True)
        idx = jnp.min(jnp.where(cur == m, rowid, 1e9), axis=0, keepdims=True)
        sel = rowid == idx
        rank = jnp.where(sel, float(r), rank)
        cur = jnp.where(sel, NEG_INF, cur)
        vals.append(m)
    return rank, vals


def _peer_prep_kernel(h_ref, wq_ref, keys_ref, r2_ref, na_ref, e1_ref, e2_ref, q_scr):
    TM = h_ref.shape[0]
    K = PEER_TOPK
    q_scr[...] = _bf(_dot(h_ref[...], wq_ref[...]))
    rowid = lax.broadcasted_iota(jnp.int32, (PEER_NKEYS, TM), 0).astype(F32)
    kaid = lax.broadcasted_iota(jnp.int32, (K, TM), 0).astype(F32)

    def head(h, carry):
        off = pl.multiple_of(h * 2 * PEER_HALF, 2 * PEER_HALF)
        s1 = _dot_nt(keys_ref[h, 0], q_scr[:, pl.ds(off, PEER_HALF)])
        s2 = _dot_nt(keys_ref[h, 1], q_scr[:, pl.ds(off + PEER_HALF, PEER_HALF)])
        rank1, vals1 = _top16(s1, rowid)
        rank2, vals2 = _top16(s2, rowid)
        v1 = jnp.concatenate(vals1, axis=0)
        taken = jnp.zeros((K, TM), F32)
        front = v1 + vals2[0]
        for _ in range(K):
            m = jnp.max(front, axis=0, keepdims=True)
            idx = jnp.min(jnp.where(front == m, kaid, 1e9), axis=0, keepdims=True)
            sel = kaid == idx
            taken = taken + jnp.where(sel, 1.0, 0.0)
            nxt = jnp.full((K, TM), NEG_INF, F32)
            for kb in range(1, K):
                nxt = jnp.where(taken == float(kb), vals2[kb], nxt)
            front = jnp.where(sel, v1 + nxt, front)
        e1k = jnp.exp(v1 - vals1[0])
        pref = jnp.zeros((1, TM), F32)
        zrow = jnp.zeros((K, TM), F32)
        for kb in range(K):
            pref = pref + jnp.exp(vals2[kb] - vals2[0])
            zrow = jnp.where(taken == float(kb + 1), pref, zrow)
        z = jnp.sum(e1k * zrow, axis=0, keepdims=True)
        na = jnp.zeros((PEER_NKEYS, TM), F32)
        for ka in range(K):
            na = jnp.where(rank1 == float(ka), taken[ka:ka + 1], na)
        r2_ref[h] = rank2
        na_ref[h] = na
        e1_ref[h] = jnp.where(rank1 < float(K), jnp.exp(s1 - vals1[0]) / z, 0.0)
        e2_ref[h] = jnp.where(rank2 < float(K), jnp.exp(s2 - vals2[0]), 0.0)
        return carry

    lax.fori_loop(0, PEER_HEADS, head, 0)


def _peer_prep(hn, wq, keys, TM):
    T, D = hn.shape
    Q = wq.shape[1]
    shp = jax.ShapeDtypeStruct((PEER_HEADS, PEER_NKEYS, T), F32)
    ospec = pl.BlockSpec((PEER_HEADS, PEER_NKEYS, TM), lambda i: (0, 0, i))
    return pl.pallas_call(
        _peer_prep_kernel,
        grid=(T // TM,),
        in_specs=[pl.BlockSpec((TM, D), lambda i: (i, 0)),
                  pl.BlockSpec((D, Q), lambda i: (0, 0)),
                  pl.BlockSpec((PEER_HEADS, 2, PEER_NKEYS, PEER_HALF), lambda i: (0, 0, 0, 0))],
        out_specs=[ospec, ospec, ospec, ospec],
        out_shape=[shp, shp, shp, shp],
        scratch_shapes=[pltpu.VMEM((TM, Q), BF16)],
        compiler_params=_cp("parallel"),
        name="peer_prep",
    )(hn, wq, keys)


def _peer_dense_kernel(h_ref, r2_ref, na_ref, e1_ref, e2_ref, u_ref, v_ref, x_ref, gt_ref, fg_ref,
                       o_ref, acc, *, experts_per_step, final_norm):
    e = pl.program_id(1)
    TM = h_ref.shape[0]
    SLAB = PEER_NKEYS

    @pl.when(e == 0)
    def _():
        acc[...] = jnp.zeros_like(acc)

    hn = h_ref[...]
    for p in range(experts_per_step // (2 * SLAB)):
        lo = p * 2 * SLAB
        s = _dot_nt(u_ref[lo:lo + 2 * SLAB, :], hn)
        act = 0.5 * s * (1.0 + lax.erf(s * SQRT_HALF))
        weights = []
        for half in range(2):
            a = e * (experts_per_step // SLAB) + p * 2 + half
            w = jnp.zeros((SLAB, TM), F32)
            for h in range(PEER_HEADS):
                chosen = r2_ref[h] < na_ref[h, pl.ds(a, 1), :]
                w = w + jnp.where(chosen, e2_ref[h], 0.0) * e1_ref[h, pl.ds(a, 1), :]
            weights.append(w)
        aw = _bf(act * jnp.concatenate(weights, axis=0))
        acc[...] += _dot_tn(aw, v_ref[lo:lo + 2 * SLAB, :])

    @pl.when(e == pl.num_programs(1) - 1)
    def _():
        xn = x_ref[...] + gt_ref[0] * acc[...]
        o_ref[...] = _rms(xn, fg_ref[...]) if final_norm else xn


def _peer_dense(x, hn, sel, u, v, gt2, TM, tiles_per_batch, final_g):
    T, D = x.shape
    E = u.shape[0]
    EB = 1024
    final_norm = final_g is not None
    fg = (final_g if final_norm else jnp.ones((D,), F32)).reshape(1, D)
    sspec = pl.BlockSpec((PEER_HEADS, PEER_NKEYS, TM), lambda i, e: (0, 0, i))
    kern = functools.partial(_peer_dense_kernel, experts_per_step=EB, final_norm=final_norm)
    return pl.pallas_call(
        kern,
        grid=(T // TM, E // EB),
        in_specs=[pl.BlockSpec((TM, D), lambda i, e: (i, 0)),
                  sspec, sspec, sspec, sspec,
                  pl.BlockSpec((EB, D), lambda i, e: (e, 0)),
                  pl.BlockSpec((EB, D), lambda i, e: (e, 0)),
                  pl.BlockSpec((TM, D), lambda i, e: (i, 0)),
                  pl.BlockSpec((1, 1, D), lambda i, e: (i // tiles_per_batch, 0, 0)),
                  pl.BlockSpec((1, D), lambda i, e: (0, 0))],
        out_specs=pl.BlockSpec((TM, D), lambda i, e: (i, 0)),
        out_shape=jax.ShapeDtypeStruct((T, D), F32),
        scratch_shapes=[pltpu.VMEM((TM, D), F32)],
        compiler_params=_cp("parallel", "arbitrary"),
        name="peer_dense",
    )(hn, *sel, u, v, x, gt2, fg)


def _direction_masks():
    t = jnp.arange(CHUNK)
    le = (t[None, :] <= t[:, None]).astype(F32)
    tri = jnp.stack([le, le.T])
    i = jnp.arange(GROUP_W)
    same = (i[:, None] // CHUNK) == (i[None, :] // CHUNK)
    ti, tj = i[:, None] % CHUNK, i[None, :] % CHUNK
    msl = jnp.stack([same & (tj < ti), same & (tj > ti)]).astype(F32)
    minc = jnp.stack([same & (tj <= ti), same & (tj >= ti)]).astype(F32)
    return tri, msl, minc


def _layer_params(l, w_in, shift_mu, w0, w2, a0, a2, g2, k_k, k_a, r_k, lnx_g, lnx_b, w_oA, conv_w,
                  cnorm_g, cnorm_b, w_oB, gate_b, w_out, norm2_g):
    D = D_MODEL
    row = lambda t: t.reshape(1, -1)
    zeros = jnp.zeros((LORA_W, D), F32)
    tri, msl, minc = _direction_masks()
    hs = (jnp.arange(D)[:, None] // HEAD == jnp.arange(128)[None, :]).astype(F32)
    pad_cols = P_RWKV_PAD - P_RWKV
    return dict(
        w_rk=_bf(jnp.pad(w_in[l][:, :P_RWKV], ((0, 0), (0, pad_cols)))),
        w_cg=_bf(w_in[l][:, P_RWKV:]),
        mu=jnp.pad(shift_mu[l], ((0, 0), (0, pad_cols))),
        w0=w0[l].reshape(2, 1, D), a0=a0[l].reshape(2, 1, D),
        w2p=jnp.stack([jnp.concatenate([w2[l, 0], zeros]), jnp.concatenate([zeros, w2[l, 1]])]),
        a2p=jnp.stack([jnp.concatenate([a2[l, 0], zeros]), jnp.concatenate([zeros, a2[l, 1]])]),
        g2p=_bf(jnp.pad(g2[l], ((0, LORA_G_PAD - LORA_G), (0, 0)))),
        k_k=row(k_k[l]), k_a=row(k_a[l]), r_k=row(r_k[l]),
        lnx_g=row(lnx_g[l]), lnx_b=row(lnx_b[l]), w_oA=_bf(w_oA[l]),
        conv_w=conv_w[l], cnorm_g=row(cnorm_g[l]), cnorm_b=row(cnorm_b[l]), w_oB=_bf(w_oB[l]),
        gate_b=row(gate_b[l]), w_out=_bf(w_out[l]), norm2_g=row(norm2_g[l]),
        hs=hs, hb=hs.T, tri=tri, msl=msl, minc=minc)


def _mixer(x, mod, pr, norm1_g, stride, h0, emit):
    B, L, D = x.shape
    sh1, sc1, gt1, sh2, sc2 = (mod[:, i:i + 1, :] for i in range(5))
    z_rk = _proj(x, sh1, sc1, norm1_g, pr["w_rk"])
    y, bv, gs, h_t = _rwkv(z_rk, pr, h0)
    if not emit:
        return None, None, h_t
    z_cg = _proj(x, sh1, sc1, norm1_g, pr["w_cg"])
    cv = _conv(z_cg, pr["conv_w"], stride)
    xn, hn = _post(x, y, bv, gs, cv, z_cg, pr, gt1, sh2, sc2)
    return xn, hn, h_t


def _peer(x, hn, wq, keys, u, v, gt2, final_g=None):
    B, L, D = x.shape
    T = B * L
    TM = min(L, 256)
    sel = _peer_prep(hn.reshape(T, D), wq, keys, TM)
    out = _peer_dense(x.reshape(T, D), hn.reshape(T, D), sel, u, v, gt2, TM, L // TM, final_g)
    return out.reshape(B, L, D)


def kernel(x, c, ctx, c_ctx, ada_w, ada_b, norm1_g, norm2_g, w_in, shift_mu, w0, w2, a0, a2, g2, k_k, k_a, r_k, lnx_g, lnx_b, w_oA, conv_w, cnorm_g, cnorm_b, w_oB, gate_b, w_out, w_q, sub_keys, peer_u, peer_v, final_g):
    B, L, D = x.shape
    depth = ada_w.shape[0]
    xc = ctx
    n_rows = -(-(B + 1) // 8) * 8
    c_rows = jnp.pad(jnp.concatenate([c, c_ctx[None, :]], axis=0), ((0, n_rows - B - 1), (0, 0)))
    zero_state = jnp.zeros((2, B, N_GROUPS, GROUP_W, GROUP_W), F32)
    for l in range(depth):
        last = l == depth - 1
        pr = _layer_params(l, w_in, shift_mu, w0, w2, a0, a2, g2, k_k, k_a, r_k, lnx_g, lnx_b, w_oA,
                           conv_w, cnorm_g, cnorm_b, w_oB, gate_b, w_out, norm2_g)
        mod_all = _modulation(c_rows, ada_w[l], ada_b[l])
        mod = mod_all[:B].reshape(B, 6, D)
        modc = jnp.broadcast_to(mod_all[B].reshape(1, 6, D), (B, 6, D))
        wq = _bf(w_q[l])
        keys = _bf(sub_keys[l])
        u = _bf(peer_u[l])
        v = _bf(peer_v[l])

        xc_new, hnc, ctx_states = _mixer(xc, modc, pr, norm1_g[l], 1, zero_state, emit=not last)
        xn, hn, _ = _mixer(x, mod, pr, norm1_g[l], GRID_W, ctx_states, emit=True)
        x = _peer(xn, hn, wq, keys, u, v, mod[:, 5:6, :], final_g if last else None)
        if not last:
            xc = _peer(xc_new, hnc, wq, keys, u, v, modc[:, 5:6, :])
    return x
```

```python
import functools

import jax
import jax.numpy as jnp
from jax import lax
from jax.experimental import pallas as pl
from jax.experimental.pallas import tpu as pltpu

F32 = jnp.float32
BF16 = jnp.bfloat16
HI = lax.Precision.HIGHEST

D_MODEL = 1024
HEAD = 64
HEADS = D_MODEL // HEAD
GROUP_HEADS = 4
GROUP_W = GROUP_HEADS * HEAD
N_GROUPS = HEADS // GROUP_HEADS
CHUNK = 64
LORA_W = 64
LORA_A = 64
LORA_G = 160
LORA_G_PAD = 256
P_RWKV = 3 * D_MODEL + 2 * LORA_W + 2 * LORA_A + LORA_G
P_RWKV_PAD = 3 * D_MODEL + 2 * LORA_W + 2 * LORA_A + LORA_G_PAD
COL_W1 = 3 * D_MODEL
COL_A1 = COL_W1 + 2 * LORA_W
COL_G1 = COL_A1 + 2 * LORA_A
CONV_K = 31
CONV_HALF = CONV_K // 2
GRID_W = 64
PEER_HEADS = 8
PEER_NKEYS = 128
PEER_HALF = 128
PEER_TOPK = 16
NORM_EPS = 1e-6
LN_EPS = 1e-5
GN_EPS = HEAD * 1e-5
VMEM_LIMIT = 56 * 1024 * 1024
NOT_SELECTED = 99.0
NEG_INF = float("-inf")
SQRT_HALF = 0.7071067811865476


def _cp(*sem):
    return pltpu.CompilerParams(dimension_semantics=sem, vmem_limit_bytes=VMEM_LIMIT)


def _dot(a, b):
    return jnp.dot(a, b, preferred_element_type=F32)


def _dot_hi(a, b):
    return jnp.dot(a, b, precision=HI, preferred_element_type=F32)


def _dot_nt(a, b):
    return lax.dot_general(a, b, (((1,), (1,)), ((), ())), preferred_element_type=F32)


def _dot_tn(a, b):
    return lax.dot_general(a, b, (((0,), (0,)), ((), ())), preferred_element_type=F32)


def _bf(a):
    return a.astype(BF16)


def _split(a):
    hi = a.astype(BF16)
    return hi, (a - hi.astype(F32)).astype(BF16)


def _dot_split_lhs(a, b_bf):
    hi, lo = _split(a)
    return _dot(hi, b_bf) + _dot(lo, b_bf)


def _dot_split(a, b_hi, b_lo):
    hi, lo = _split(a)
    return _dot(hi, b_hi) + _dot(hi, b_lo) + _dot(lo, b_hi)


def _head_sum(t, hs_bf, hb_bf):
    return _dot_split_lhs(_dot_split_lhs(t, hs_bf), hb_bf)


def _sigmoid(x):
    return 1.0 / (1.0 + jnp.exp(-x))


def _softplus(x):
    return jnp.maximum(x, 0.0) + jnp.log(1.0 + jnp.exp(-jnp.abs(x)))


def _rms(x, g):
    return x * lax.rsqrt(jnp.mean(x * x, axis=-1, keepdims=True) + NORM_EPS) * g


def _mod_kernel(c_ref, w_ref, b_ref, o_ref):
    c = c_ref[...]
    o_ref[...] = _dot_hi(c * _sigmoid(c), w_ref[...]) + b_ref[...]


def _modulation(c_rows, ada_w, ada_b):
    R, D = c_rows.shape
    N = ada_w.shape[1]
    TN = 512
    return pl.pallas_call(
        _mod_kernel,
        grid=(N // TN,),
        in_specs=[pl.BlockSpec((R, D), lambda j: (0, 0)),
                  pl.BlockSpec((D, TN), lambda j: (0, j)),
                  pl.BlockSpec((1, TN), lambda j: (0, j))],
        out_specs=pl.BlockSpec((R, TN), lambda j: (0, j)),
        out_shape=jax.ShapeDtypeStruct((R, N), F32),
        compiler_params=_cp("arbitrary"),
        name="modulation",
    )(c_rows, ada_w, ada_b.reshape(1, N))


def _proj_kernel(x_ref, sh_ref, sc_ref, g_ref, w_ref, o_ref):
    h = _rms(x_ref[0], g_ref[...]) * (1.0 + sc_ref[0]) + sh_ref[0]
    o_ref[0] = _dot(_bf(h), w_ref[...])


def _proj(x, sh, sc, g, w):
    B, L, D = x.shape
    N = w.shape[1]
    TM = min(L, 256)
    return pl.pallas_call(
        _proj_kernel,
        grid=(B, L // TM),
        in_specs=[pl.BlockSpec((1, TM, D), lambda b, i: (b, i, 0)),
                  pl.BlockSpec((1, 1, D), lambda b, i: (b, 0, 0)),
                  pl.BlockSpec((1, 1, D), lambda b, i: (b, 0, 0)),
                  pl.BlockSpec((1, D), lambda b, i: (0, 0)),
                  pl.BlockSpec((D, N), lambda b, i: (0, 0))],
        out_specs=pl.BlockSpec((1, TM, N), lambda b, i: (b, i, 0)),
        out_shape=jax.ShapeDtypeStruct((B, L, N), F32),
        compiler_params=_cp("parallel", "parallel"),
        name="proj",
    )(x, sh, sc, g.reshape(1, D), w)


_KAP, _RT, _KT, _BT, _KH, _BH, _V = range(7)


def _rwkv_kernel(z_ref, zp_ref, zn_ref, mu_ref, w0_ref, w2h_ref, w2l_ref, a0_ref, a2h_ref, a2l_ref,
                 kk_ref, ka_ref, rk_ref, hs_ref, hb_ref, tri_ref, msl_ref, minc_ref, h0_ref,
                 y_ref, bv_ref, gs_ref, hT_ref, H_scr, nat_scr, pc_scr, *, n_chunks):
    d = pl.program_id(0)
    c = pl.program_id(2)
    cc = jnp.where(d == 0, c, n_chunks - 1 - c)
    C = CHUNK

    @pl.when(c == 0)
    def _():
        H_scr[...] = h0_ref[0, 0]

    first = cc == 0
    last = cc == n_chunks - 1
    row = lax.broadcasted_iota(jnp.int32, (C, 1), 0)

    def shifted(lo, hi):
        z = z_ref[0, :, lo:hi]
        prev_row = jnp.where(first, 0.0, zp_ref[0, 7:8, lo:hi])
        next_row = jnp.where(last, 0.0, zn_ref[0, 0:1, lo:hi])
        zp = jnp.where(row == 0, prev_row, pltpu.roll(z, 1, 0))
        zn = jnp.where(row == C - 1, next_row, pltpu.roll(z, C - 1, 0))
        return z + mu_ref[0:1, lo:hi] * (zp - z) + mu_ref[1:2, lo:hi] * (zn - z)

    hs = hs_ref[...]
    hb = hb_ref[...]

    def head_sum(t):
        return _head_sum(t, hs, hb)

    r = shifted(0, D_MODEL)
    k = shifted(D_MODEL, 2 * D_MODEL)
    v = shifted(2 * D_MODEL, 3 * D_MODEL)
    w1 = shifted(COL_W1, COL_A1)
    a1 = shifted(COL_A1, COL_G1)
    g1 = shifted(COL_G1, P_RWKV_PAD)

    wl = w0_ref[0] + _dot_split(jnp.tanh(w1), w2h_ref[0], w2l_ref[0])
    logw = -jnp.exp(-_softplus(-wl) - 0.5)
    a = _sigmoid(_dot_split(a1, a2h_ref[0], a2l_ref[0]) + a0_ref[0])
    kkr = k * kk_ref[...]
    kk = kkr / jnp.maximum(jnp.sqrt(head_sum(kkr * kkr)), 1e-12)
    kd = k * (1.0 + (a - 1.0) * ka_ref[...])
    bb = kk * a
    bv_ref[0, 0] = head_sum(r * kd * rk_ref[...]) * v
    gs_ref[0, 0] = _sigmoid(g1)

    lw_hi, lw_lo = _split(logw)
    g_in = _dot(tri_ref[0], lw_hi) + _dot(tri_ref[0], lw_lo)
    g_ex = g_in - logw
    g_c = jnp.sum(logw, axis=0, keepdims=True)
    e_inv = jnp.exp(-g_in)
    e_hat = jnp.exp(g_c - g_in)
    nat_scr[_KAP] = kk * jnp.exp(g_ex)
    nat_scr[_RT] = r * jnp.exp(g_in)
    nat_scr[_KT] = kd * e_inv
    nat_scr[_BT] = bb * e_inv
    nat_scr[_KH] = kd * e_hat
    nat_scr[_BH] = bb * e_hat
    nat_scr[_V] = v
    pc_scr[...] = jnp.exp(g_c)

    lane_head = lax.broadcasted_iota(jnp.int32, (C, GROUP_W), 1) // HEAD
    ii = lax.broadcasted_iota(jnp.int32, (GROUP_W, GROUP_W), 0)
    jj = lax.broadcasted_iota(jnp.int32, (GROUP_W, GROUP_W), 1)
    eye = ii == jj
    msl = msl_ref[0] > 0.5
    minc = minc_ref[0] > 0.5

    def stacked(t):
        return jnp.concatenate([jnp.where(lane_head == j, t, 0.0) for j in range(GROUP_HEADS)], axis=0)

    def collapse(t):
        return t[0:C] + t[C:2 * C] + t[2 * C:3 * C] + t[3 * C:4 * C]

    G = range(N_GROUPS)

    def nat(i, g):
        return nat_scr[i, :, g * GROUP_W:(g + 1) * GROUP_W]

    x_kap = [stacked(nat(_KAP, g)) for g in G]
    x_r = [stacked(nat(_RT, g)) for g in G]
    x_v = [_bf(stacked(nat(_V, g))) for g in G]
    x_bk = [jnp.concatenate([_bf(stacked(nat(_BT, g))), _bf(stacked(nat(_KT, g)))], axis=0) for g in G]
    a1m = [_dot_nt(_bf(x_kap[g]), x_bk[g]) for g in G]
    a2m = [_dot_nt(_bf(x_r[g]), x_bk[g]) for g in G]
    n_pow = [jnp.where(msl, a1m[g][:, :GROUP_W], 0.0) for g in G]
    a_kk = [_bf(jnp.where(msl, a1m[g][:, GROUP_W:], 0.0)) for g in G]
    a_rb = [_bf(jnp.where(minc, a2m[g][:, :GROUP_W], 0.0)) for g in G]
    a_rk = [_bf(jnp.where(minc, a2m[g][:, GROUP_W:], 0.0)) for g in G]
    g0 = [_dot(a_kk[g], x_v[g]) for g in G]
    y0 = [_dot(a_rk[g], x_v[g]) for g in G]
    eye_f = jnp.where(eye, 1.0, 0.0)
    p_inv = [eye_f - n_pow[g] for g in G]
    for _ in range(5):
        nb = [_bf(n_pow[g]) for g in G]
        n_pow = [_dot(nb[g], nb[g]) for g in G]
        p_inv = [p_inv[g] + _dot(_bf(p_inv[g]), _bf(n_pow[g])) for g in G]
    w12 = [_bf(_dot(_bf(p_inv[g]), jnp.concatenate([_bf(x_kap[g]), _bf(g0[g])], axis=1))) for g in G]
    aw = [_dot(a_rb[g], w12[g]) for g in G]
    bw = [_dot_tn(_bf(stacked(nat(_BH, g))), w12[g]) for g in G]
    kv = [_dot_tn(_bf(stacked(nat(_KH, g))), x_v[g]) for g in G]
    h_old = [_split(H_scr[g]) for g in G]
    ys = []
    h_new = []
    for g in G:
        h_hi, h_lo = h_old[g]
        qb = _bf(collapse(x_r[g] - aw[g][:, :GROUP_W]))
        ys.append(_dot(qb, h_hi) + _dot(qb, h_lo) + collapse(y0[g] - aw[g][:, GROUP_W:]))
        m_mat = jnp.where(eye, pc_scr[:, g * GROUP_W:(g + 1) * GROUP_W], 0.0) - bw[g][:, :GROUP_W]
        m_hi, m_lo = _split(m_mat)
        h_new.append(_dot(m_hi, h_hi) + _dot(m_hi, h_lo) + _dot(m_lo, h_hi) + kv[g] - bw[g][:, GROUP_W:])
    y_ref[0, 0] = jnp.concatenate(ys, axis=1)
    for g in G:
        H_scr[g] = h_new[g]

    @pl.when(c == n_chunks - 1)
    def _():
        hT_ref[0, 0] = H_scr[...]


def _rwkv(z, pr, h0):
    B, L, N = z.shape
    C = CHUNK
    nC = L // C
    D = D_MODEL
    nb8 = L // 8

    def cidx(d, c):
        return c + d * (nC - 1 - 2 * c)

    zmap = lambda d, b, c: (b, cidx(d, c), 0)
    pmap = lambda d, b, c: (b, jnp.maximum(cidx(d, c) * (C // 8) - 1, 0), 0)
    nmap = lambda d, b, c: (b, jnp.minimum((cidx(d, c) + 1) * (C // 8), nb8 - 1), 0)
    const2 = lambda d, b, c: (0, 0)
    dir3 = lambda d, b, c: (d, 0, 0)
    omap = lambda d, b, c: (d, b, cidx(d, c), 0)
    smap = lambda d, b, c: (d, b, 0, 0, 0)
    kern = functools.partial(_rwkv_kernel, n_chunks=nC)
    return pl.pallas_call(
        kern,
        grid=(2, B, nC),
        in_specs=[pl.BlockSpec((1, C, N), zmap),
                  pl.BlockSpec((1, 8, N), pmap),
                  pl.BlockSpec((1, 8, N), nmap),
                  pl.BlockSpec((2, N), const2),
                  pl.BlockSpec((1, 1, D), dir3),
                  pl.BlockSpec((1, 2 * LORA_W, D), dir3),
                  pl.BlockSpec((1, 2 * LORA_W, D), dir3),
                  pl.BlockSpec((1, 1, D), dir3),
                  pl.BlockSpec((1, 2 * LORA_A, D), dir3),
                  pl.BlockSpec((1, 2 * LORA_A, D), dir3),
                  pl.BlockSpec((1, D), const2),
                  pl.BlockSpec((1, D), const2),
                  pl.BlockSpec((1, D), const2),
                  pl.BlockSpec((D, 128), const2),
                  pl.BlockSpec((128, D), const2),
                  pl.BlockSpec((1, C, C), dir3),
                  pl.BlockSpec((1, GROUP_W, GROUP_W), dir3),
                  pl.BlockSpec((1, GROUP_W, GROUP_W), dir3),
                  pl.BlockSpec((1, 1, N_GROUPS, GROUP_W, GROUP_W), smap)],
        out_specs=[pl.BlockSpec((1, 1, C, D), omap),
                   pl.BlockSpec((1, 1, C, D), omap),
                   pl.BlockSpec((1, 1, C, LORA_G_PAD), omap),
                   pl.BlockSpec((1, 1, N_GROUPS, GROUP_W, GROUP_W), smap)],
        out_shape=[jax.ShapeDtypeStruct((2, B, L, D), F32),
                   jax.ShapeDtypeStruct((2, B, L, D), F32),
                   jax.ShapeDtypeStruct((2, B, L, LORA_G_PAD), F32),
                   jax.ShapeDtypeStruct((2, B, N_GROUPS, GROUP_W, GROUP_W), F32)],
        scratch_shapes=[pltpu.VMEM((N_GROUPS, GROUP_W, GROUP_W), F32),
                        pltpu.VMEM((7, C, D), F32),
                        pltpu.VMEM((1, D), F32)],
        compiler_params=_cp("arbitrary", "arbitrary", "arbitrary"),
        name="rwkv",
    )(z, z, z, pr["mu"], pr["w0"], *pr["w2p"], pr["a0"], *pr["a2p"], pr["k_k"], pr["k_a"], pr["r_k"],
      pr["hs"], pr["hb"], pr["tri"], pr["msl"], pr["minc"], h0)


def _conv_kernel(za_ref, zb_ref, w_ref, o_ref, upad, *, L, stride, rows_per_step):
    pad = CONV_HALF * stride
    TC = za_ref.shape[-1]
    upad[0:pad, :] = jnp.zeros((pad, TC), F32)
    upad[pad + L:pad + L + pad, :] = jnp.zeros((pad, TC), F32)
    upad[pad:pad + L, :] = za_ref[0] * _sigmoid(zb_ref[0])
    RB = rows_per_step

    def block(r0):
        acc = jnp.zeros((RB, TC), F32)
        for j in range(CONV_K):
            acc = acc + w_ref[j:j + 1, :] * upad[pl.ds(r0 + j * stride, RB), :]
        o_ref[0, pl.ds(r0, RB), :] = acc

    if stride % 8 == 0:
        def body(i, carry):
            block(pl.multiple_of(i * RB, RB))
            return carry
        lax.fori_loop(0, L // RB, body, 0)
    else:
        for i in range(L // RB):
            block(i * RB)


def _conv(zcg, conv_w, stride):
    B, L, _ = zcg.shape
    D = D_MODEL
    TC = 128
    nct = D // TC
    wpad = jnp.pad(conv_w, ((0, 32 - CONV_K), (0, 0)))
    RB = min(L, 128)
    kern = functools.partial(_conv_kernel, L=L, stride=stride, rows_per_step=RB)
    return pl.pallas_call(
        kern,
        grid=(B, nct),
        in_specs=[pl.BlockSpec((1, L, TC), lambda b, j: (b, 0, j)),
                  pl.BlockSpec((1, L, TC), lambda b, j: (b, 0, j + nct)),
                  pl.BlockSpec((32, TC), lambda b, j: (0, j))],
        out_specs=pl.BlockSpec((1, L, TC), lambda b, j: (b, 0, j)),
        out_shape=jax.ShapeDtypeStruct((B, L, D), F32),
        scratch_shapes=[pltpu.VMEM((L + 2 * CONV_HALF * stride, TC), F32)],
        compiler_params=_cp("parallel", "parallel"),
        name="conv",
    )(zcg, zcg, wpad)


def _post_kernel(x_ref, yf_ref, yb_ref, bf_ref, bb_ref, gs_ref, cv_ref, zg_ref,
                 lng_ref, lnb_ref, g2_ref, woa_ref, cng_ref, cnb_ref, wob_ref, gb_ref, wout_ref,
                 gt_ref, n2g_ref, sh2_ref, sc2_ref, hs_ref, hb_ref, xo_ref, hn_ref):
    D = D_MODEL
    hs = hs_ref[...]
    hb = hb_ref[...]

    def head_mean(t):
        return _head_sum(t, hs, hb) * (1.0 / HEAD)

    o = yf_ref[0, 0] + yb_ref[0, 0]
    oc = o - head_mean(o)
    on = oc * lax.rsqrt(head_mean(oc * oc) + GN_EPS) * lng_ref[...] + lnb_ref[...]
    on = on + bf_ref[0, 0] + bb_ref[0, 0]
    gate = _dot(_bf(gs_ref[0, 0]), g2_ref[...])
    y_a = _dot(_bf(on * gate), woa_ref[...])

    cv = cv_ref[0]
    cm = jnp.mean(cv, axis=-1, keepdims=True)
    cc = cv - cm
    cn = cc * lax.rsqrt(jnp.mean(cc * cc, axis=-1, keepdims=True) + LN_EPS) * cng_ref[...] + cnb_ref[...]
    y_b = _dot(_bf(cn * _sigmoid(cn)), wob_ref[...])

    gates = _sigmoid(zg_ref[0] + gb_ref[...])
    m = gates[:, :D] * y_a + gates[:, D:] * y_b
    xn = x_ref[0] + gt_ref[0] * _dot(_bf(m), wout_ref[...])
    xo_ref[0] = xn
    hn_ref[0] = _bf(_rms(xn, n2g_ref[...]) * (1.0 + sc2_ref[0]) + sh2_ref[0])


def _post(x, y, bv, gs, cv, zcg, pr, gt1, sh2, sc2):
    B, L, D = x.shape
    TM = min(L, 256)
    tok = lambda b, i: (b, i, 0)
    fwd = lambda b, i: (0, b, i, 0)
    bwd = lambda b, i: (1, b, i, 0)
    cst = lambda b, i: (0, 0)
    per_b = lambda b, i: (b, 0, 0)
    row = pl.BlockSpec((1, D), cst)
    mat = pl.BlockSpec((D, D), cst)
    return pl.pallas_call(
        _post_kernel,
        grid=(B, L // TM),
        in_specs=[pl.BlockSpec((1, TM, D), tok),
                  pl.BlockSpec((1, 1, TM, D), fwd), pl.BlockSpec((1, 1, TM, D), bwd),
                  pl.BlockSpec((1, 1, TM, D), fwd), pl.BlockSpec((1, 1, TM, D), bwd),
                  pl.BlockSpec((1, 1, TM, LORA_G_PAD), fwd),
                  pl.BlockSpec((1, TM, D), tok),
                  pl.BlockSpec((1, TM, 2 * D), lambda b, i: (b, i, 1)),
                  row, row, pl.BlockSpec((LORA_G_PAD, D), cst), mat,
                  row, row, mat, pl.BlockSpec((1, 2 * D), cst), mat,
                  pl.BlockSpec((1, 1, D), per_b), row,
                  pl.BlockSpec((1, 1, D), per_b), pl.BlockSpec((1, 1, D), per_b),
                  pl.BlockSpec((D, 128), cst), pl.BlockSpec((128, D), cst)],
        out_specs=[pl.BlockSpec((1, TM, D), tok), pl.BlockSpec((1, TM, D), tok)],
        out_shape=[jax.ShapeDtypeStruct((B, L, D), F32), jax.ShapeDtypeStruct((B, L, D), BF16)],
        compiler_params=_cp("parallel", "parallel"),
        name="post",
    )(x, y, y, bv, bv, gs, cv, zcg,
      pr["lnx_g"], pr["lnx_b"], pr["g2p"], pr["w_oA"], pr["cnorm_g"], pr["cnorm_b"], pr["w_oB"],
      pr["gate_b"], pr["w_out"], gt1, pr["norm2_g"], sh2, sc2, pr["hs"], pr["hb"])


def _top16(s, rowid):
    rank = jnp.full(s.shape, NOT_SELECTED, F32)
    cur = s
    vals = []
    for r in range(PEER_TOPK):
        m = jnp.max(cur, axis=0, keepdims=True)
        idx = jnp.min(jnp.where(cur == m, rowid, 1e9), axis=0, keepdims=True)
        sel = rowid == idx
        rank = jnp.where(sel, float(r), rank)
        cur = jnp.where(sel, NEG_INF, cur)
        vals.append(m)
    return rank, vals


def _peer_prep_kernel(h_ref, wq_ref, keys_ref, r2_ref, na_ref, e1_ref, e2_ref, q_scr):
    TM = h_ref.shape[0]
    K = PEER_TOPK
    q_scr[...] = _bf(_dot(h_ref[...], wq_ref[...]))
    rowid = lax.broadcasted_iota(jnp.int32, (PEER_NKEYS, TM), 0).astype(F32)
    kaid = lax.broadcasted_iota(jnp.int32, (K, TM), 0).astype(F32)

    def head(h, carry):
        off = pl.multiple_of(h * 2 * PEER_HALF, 2 * PEER_HALF)
        s1 = _dot_nt(keys_ref[h, 0], q_scr[:, pl.ds(off, PEER_HALF)])
        s2 = _dot_nt(keys_ref[h, 1], q_scr[:, pl.ds(off + PEER_HALF, PEER_HALF)])
        rank1, vals1 = _top16(s1, rowid)
        rank2, vals2 = _top16(s2, rowid)
        v1 = jnp.concatenate(vals1, axis=0)
        taken = jnp.zeros((K, TM), F32)
        front = v1 + vals2[0]
        for _ in range(K):
            m = jnp.max(front, axis=0, keepdims=True)
            idx = jnp.min(jnp.where(front == m, kaid, 1e9), axis=0, keepdims=True)
            sel = kaid == idx
            taken = taken + jnp.where(sel, 1.0, 0.0)
            nxt = jnp.full((K, TM), NEG_INF, F32)
            for kb in range(1, K):
                nxt = jnp.where(taken == float(kb), vals2[kb], nxt)
            front = jnp.where(sel, v1 + nxt, front)
        e1k = jnp.exp(v1 - vals1[0])
        pref = jnp.zeros((1, TM), F32)
        zrow = jnp.zeros((K, TM), F32)
        for kb in range(K):
            pref = pref + jnp.exp(vals2[kb] - vals2[0])
            zrow = jnp.where(taken == float(kb + 1), pref, zrow)
        z = jnp.sum(e1k * zrow, axis=0, keepdims=True)
        na = jnp.zeros((PEER_NKEYS, TM), F32)
        for ka in range(K):
            na = jnp.where(rank1 == float(ka), taken[ka:ka + 1], na)
        r2_ref[h] = _bf(rank2)
        na_ref[h] = na
        e1_ref[h] = jnp.where(rank1 < float(K), jnp.exp(s1 - vals1[0]) / z, 0.0)
        e2_ref[h] = _bf(jnp.where(rank2 < float(K), jnp.exp(s2 - vals2[0]), 0.0))
        return carry

    lax.fori_loop(0, PEER_HEADS, head, 0)


def _peer_prep(hn, wq, keys, TM):
    T, D = hn.shape
    Q = wq.shape[1]
    shp = jax.ShapeDtypeStruct((PEER_HEADS, PEER_NKEYS, T), F32)
    shp_bf = jax.ShapeDtypeStruct((PEER_HEADS, PEER_NKEYS, T), BF16)
    ospec = pl.BlockSpec((PEER_HEADS, PEER_NKEYS, TM), lambda i: (0, 0, i))
    return pl.pallas_call(
        _peer_prep_kernel,
        grid=(T // TM,),
        in_specs=[pl.BlockSpec((TM, D), lambda i: (i, 0)),
                  pl.BlockSpec((D, Q), lambda i: (0, 0)),
                  pl.BlockSpec((PEER_HEADS, 2, PEER_NKEYS, PEER_HALF), lambda i: (0, 0, 0, 0))],
        out_specs=[ospec, ospec, ospec, ospec],
        out_shape=[shp_bf, shp, shp, shp_bf],
        scratch_shapes=[pltpu.VMEM((TM, Q), BF16)],
        compiler_params=_cp("parallel"),
        name="peer_prep",
    )(hn, wq, keys)


def _peer_dense_kernel(h_ref, r2_ref, na_ref, e1_ref, e2_ref, u_ref, vt_ref, x_ref, gt_ref, fg_ref,
                       o_ref, acc_t, ht_scr, aw_scr, *, experts_per_step, final_norm):
    e = pl.program_id(1)
    TM = h_ref.shape[0]
    SLAB = PEER_NKEYS

    @pl.when(e == 0)
    def _():
        acc_t[...] = jnp.zeros_like(acc_t)
        ht_scr[...] = _bf(h_ref[...].astype(F32).T)

    zero_bf = jnp.zeros((SLAB, TM), BF16)

    def row_tile(ref, h, a):
        t = _bf(jnp.broadcast_to(ref[h, pl.ds(a, 1), :], (16, TM)))
        return jnp.concatenate([t] * (SLAB // 16), axis=0)

    for p in range(experts_per_step // (2 * SLAB)):
        lo = p * 2 * SLAB
        s = _dot(u_ref[lo:lo + 2 * SLAB, :], ht_scr[...])
        act = 0.5 * s * (1.0 + lax.erf(s * SQRT_HALF))
        weights = []
        for half in range(2):
            a = e * (experts_per_step // SLAB) + p * 2 + half
            w = jnp.zeros((SLAB, TM), BF16)
            for h in range(PEER_HEADS):
                chosen = r2_ref[h] < row_tile(na_ref, h, a)
                w = w + jnp.where(chosen, e2_ref[h], zero_bf) * row_tile(e1_ref, h, a)
            weights.append(w)
        aw_scr[lo:lo + 2 * SLAB, :] = _bf(act) * jnp.concatenate(weights, axis=0)
    acc_t[...] += _dot(vt_ref[...], aw_scr[...])

    @pl.when(e == pl.num_programs(1) - 1)
    def _():
        xn = x_ref[...] + gt_ref[0] * acc_t[...].T
        o_ref[...] = _rms(xn, fg_ref[...]) if final_norm else xn


def _peer_dense(x, hn, sel, u, vt, gt2, TM, tiles_per_batch, final_g):
    T, D = x.shape
    E = u.shape[0]
    EB = 1024
    final_norm = final_g is not None
    fg = (final_g if final_norm else jnp.ones((D,), F32)).reshape(1, D)
    sspec = pl.BlockSpec((PEER_HEADS, PEER_NKEYS, TM), lambda i, e: (0, 0, i))
    kern = functools.partial(_peer_dense_kernel, experts_per_step=EB, final_norm=final_norm)
    return pl.pallas_call(
        kern,
        grid=(T // TM, E // EB),
        in_specs=[pl.BlockSpec((TM, D), lambda i, e: (i, 0)),
                  sspec, sspec, sspec, sspec,
                  pl.BlockSpec((EB, D), lambda i, e: (e, 0)),
                  pl.BlockSpec((D, EB), lambda i, e: (0, e)),
                  pl.BlockSpec((TM, D), lambda i, e: (i, 0)),
                  pl.BlockSpec((1, 1, D), lambda i, e: (i // tiles_per_batch, 0, 0)),
                  pl.BlockSpec((1, D), lambda i, e: (0, 0))],
        out_specs=pl.BlockSpec((TM, D), lambda i, e: (i, 0)),
        out_shape=jax.ShapeDtypeStruct((T, D), F32),
        scratch_shapes=[pltpu.VMEM((D, TM), F32), pltpu.VMEM((D, TM), BF16), pltpu.VMEM((EB, TM), BF16)],
        compiler_params=_cp("parallel", "arbitrary"),
        name="peer_dense",
    )(hn, *sel, u, vt, x, gt2, fg)


def _direction_masks():
    t = jnp.arange(CHUNK)
    le = (t[None, :] <= t[:, None]).astype(F32)
    tri = jnp.stack([le, le.T])
    i = jnp.arange(GROUP_W)
    same = (i[:, None] // CHUNK) == (i[None, :] // CHUNK)
    ti, tj = i[:, None] % CHUNK, i[None, :] % CHUNK
    msl = jnp.stack([same & (tj < ti), same & (tj > ti)]).astype(F32)
    minc = jnp.stack([same & (tj <= ti), same & (tj >= ti)]).astype(F32)
    return tri, msl, minc


def _layer_params(l, w_in, shift_mu, w0, w2, a0, a2, g2, k_k, k_a, r_k, lnx_g, lnx_b, w_oA, conv_w,
                  cnorm_g, cnorm_b, w_oB, gate_b, w_out, norm2_g):
    D = D_MODEL
    row = lambda t: t.reshape(1, -1)
    zeros = jnp.zeros((LORA_W, D), F32)
    tri, msl, minc = _direction_masks()
    hs = (jnp.arange(D)[:, None] // HEAD == jnp.arange(128)[None, :]).astype(F32)
    pad_cols = P_RWKV_PAD - P_RWKV
    return dict(
        w_rk=_bf(jnp.pad(w_in[l][:, :P_RWKV], ((0, 0), (0, pad_cols)))),
        w_cg=_bf(w_in[l][:, P_RWKV:]),
        mu=jnp.pad(shift_mu[l], ((0, 0), (0, pad_cols))),
        w0=w0[l].reshape(2, 1, D), a0=a0[l].reshape(2, 1, D),
        w2p=_split(jnp.stack([jnp.concatenate([w2[l, 0], zeros]), jnp.concatenate([zeros, w2[l, 1]])])),
        a2p=_split(jnp.stack([jnp.concatenate([a2[l, 0], zeros]), jnp.concatenate([zeros, a2[l, 1]])])),
        g2p=_bf(jnp.pad(g2[l], ((0, LORA_G_PAD - LORA_G), (0, 0)))),
        k_k=row(k_k[l]), k_a=row(k_a[l]), r_k=row(r_k[l]),
        lnx_g=row(lnx_g[l]), lnx_b=row(lnx_b[l]), w_oA=_bf(w_oA[l]),
        conv_w=conv_w[l], cnorm_g=row(cnorm_g[l]), cnorm_b=row(cnorm_b[l]), w_oB=_bf(w_oB[l]),
        gate_b=row(gate_b[l]), w_out=_bf(w_out[l]), norm2_g=row(norm2_g[l]),
        hs=_bf(hs), hb=_bf(hs.T), tri=_bf(tri), msl=msl, minc=minc)


def _mixer(x, mod, pr, norm1_g, stride, h0, emit):
    B, L, D = x.shape
    sh1, sc1, gt1, sh2, sc2 = (mod[:, i:i + 1, :] for i in range(5))
    z_rk = _proj(x, sh1, sc1, norm1_g, pr["w_rk"])
    y, bv, gs, h_t = _rwkv(z_rk, pr, h0)
    if not emit:
        return None, None, h_t
    z_cg = _proj(x, sh1, sc1, norm1_g, pr["w_cg"])
    cv = _conv(z_cg, pr["conv_w"], stride)
    xn, hn = _post(x, y, bv, gs, cv, z_cg, pr, gt1, sh2, sc2)
    return xn, hn, h_t


def _peer(x, hn, wq, keys, u, v, gt2, final_g=None):
    B, L, D = x.shape
    T = B * L
    TM = min(L, 256)
    sel = _peer_prep(hn.reshape(T, D), wq, keys, TM)
    out = _peer_dense(x.reshape(T, D), hn.reshape(T, D), sel, u, v, gt2, TM, L // TM, final_g)
    return out.reshape(B, L, D)


def kernel(x, c, ctx, c_ctx, ada_w, ada_b, norm1_g, norm2_g, w_in, shift_mu, w0, w2, a0, a2, g2, k_k, k_a, r_k, lnx_g, lnx_b, w_oA, conv_w, cnorm_g, cnorm_b, w_oB, gate_b, w_out, w_q, sub_keys, peer_u, peer_v, final_g):
    B, L, D = x.shape
    depth = ada_w.shape[0]
    xc = ctx
    n_rows = -(-(B + 1) // 8) * 8
    c_rows = jnp.pad(jnp.concatenate([c, c_ctx[None, :]], axis=0), ((0, n_rows - B - 1), (0, 0)))
    zero_state = jnp.zeros((2, B, N_GROUPS, GROUP_W, GROUP_W), F32)
    for l in range(depth):
        last = l == depth - 1
        pr = _layer_params(l, w_in, shift_mu, w0, w2, a0, a2, g2, k_k, k_a, r_k, lnx_g, lnx_b, w_oA,
                           conv_w, cnorm_g, cnorm_b, w_oB, gate_b, w_out, norm2_g)
        mod_all = _modulation(c_rows, ada_w[l], ada_b[l])
        mod = mod_all[:B].reshape(B, 6, D)
        modc = jnp.broadcast_to(mod_all[B].reshape(1, 6, D), (B, 6, D))
        wq = _bf(w_q[l])
        keys = _bf(sub_keys[l])
        u = _bf(peer_u[l])
        v = _bf(peer_v[l]).T

        xc_new, hnc, ctx_states = _mixer(xc, modc, pr, norm1_g[l], 1, zero_state, emit=not last)
        xn, hn, _ = _mixer(x, mod, pr, norm1_g[l], GRID_W, ctx_states, emit=True)
        x = _peer(xn, hn, wq, keys, u, v, mod[:, 5:6, :], final_g if last else None)
        if not last:
            xc = _peer(xc_new, hnc, wq, keys, u, v, modc[:, 5:6, :])
    return x
```

```python
import functools

import jax
import jax.numpy as jnp
from jax import lax
from jax.experimental import pallas as pl
from jax.experimental.pallas import tpu as pltpu

F32 = jnp.float32
BF16 = jnp.bfloat16
HI = lax.Precision.HIGHEST

D_MODEL = 1024
HEAD = 64
HEADS = D_MODEL // HEAD
GROUP_HEADS = 4
GROUP_W = GROUP_HEADS * HEAD
N_GROUPS = HEADS // GROUP_HEADS
CHUNK = 64
LORA_W = 64
LORA_A = 64
LORA_G = 160
LORA_G_PAD = 256
P_RWKV = 3 * D_MODEL + 2 * LORA_W + 2 * LORA_A + LORA_G
P_RWKV_PAD = 3 * D_MODEL + 2 * LORA_W + 2 * LORA_A + LORA_G_PAD
COL_W1 = 3 * D_MODEL
COL_A1 = COL_W1 + 2 * LORA_W
COL_G1 = COL_A1 + 2 * LORA_A
CONV_K = 31
CONV_HALF = CONV_K // 2
GRID_W = 64
PEER_HEADS = 8
PEER_NKEYS = 128
PEER_HALF = 128
PEER_TOPK = 16
NORM_EPS = 1e-6
LN_EPS = 1e-5
GN_EPS = HEAD * 1e-5
VMEM_LIMIT = 56 * 1024 * 1024
NOT_SELECTED = 99.0
NEG_INF = float("-inf")
SQRT_HALF = 0.7071067811865476


def _cp(*sem):
    return pltpu.CompilerParams(dimension_semantics=sem, vmem_limit_bytes=VMEM_LIMIT)


def _dot(a, b):
    return jnp.dot(a, b, preferred_element_type=F32)


def _dot_hi(a, b):
    return jnp.dot(a, b, precision=HI, preferred_element_type=F32)


def _dot_nt(a, b):
    return lax.dot_general(a, b, (((1,), (1,)), ((), ())), preferred_element_type=F32)


def _dot_tn(a, b):
    return lax.dot_general(a, b, (((0,), (0,)), ((), ())), preferred_element_type=F32)


def _bf(a):
    return a.astype(BF16)


def _split(a):
    hi = a.astype(BF16)
    return hi, (a - hi.astype(F32)).astype(BF16)


def _dot_split_lhs(a, b_bf):
    hi, lo = _split(a)
    return _dot(hi, b_bf) + _dot(lo, b_bf)


def _dot_split(a, b_hi, b_lo):
    hi, lo = _split(a)
    return _dot(hi, b_hi) + _dot(hi, b_lo) + _dot(lo, b_hi)


def _head_sum(t, hs_bf, hb_bf):
    return _dot_split_lhs(_dot_split_lhs(t, hs_bf), hb_bf)


def _sigmoid(x):
    return 1.0 / (1.0 + jnp.exp(-x))


def _softplus(x):
    return jnp.maximum(x, 0.0) + jnp.log(1.0 + jnp.exp(-jnp.abs(x)))


def _rms(x, g):
    return x * lax.rsqrt(jnp.mean(x * x, axis=-1, keepdims=True) + NORM_EPS) * g


def _mod_kernel(c_ref, w_ref, b_ref, o_ref):
    c = c_ref[...]
    o_ref[...] = _dot_hi(c * _sigmoid(c), w_ref[...]) + b_ref[...]


def _modulation(c_rows, ada_w, ada_b):
    R, D = c_rows.shape
    N = ada_w.shape[1]
    TN = 512
    return pl.pallas_call(
        _mod_kernel,
        grid=(N // TN,),
        in_specs=[pl.BlockSpec((R, D), lambda j: (0, 0)),
                  pl.BlockSpec((D, TN), lambda j: (0, j)),
                  pl.BlockSpec((1, TN), lambda j: (0, j))],
        out_specs=pl.BlockSpec((R, TN), lambda j: (0, j)),
        out_shape=jax.ShapeDtypeStruct((R, N), F32),
        compiler_params=_cp("arbitrary"),
        name="modulation",
    )(c_rows, ada_w, ada_b.reshape(1, N))


def _proj_kernel(x_ref, sh_ref, sc_ref, g_ref, w_ref, o_ref):
    h = _rms(x_ref[0], g_ref[...]) * (1.0 + sc_ref[0]) + sh_ref[0]
    o_ref[0] = _dot(_bf(h), w_ref[...])


def _proj(x, sh, sc, g, w):
    B, L, D = x.shape
    N = w.shape[1]
    TM = min(L, 256)
    return pl.pallas_call(
        _proj_kernel,
        grid=(B, L // TM),
        in_specs=[pl.BlockSpec((1, TM, D), lambda b, i: (b, i, 0)),
                  pl.BlockSpec((1, 1, D), lambda b, i: (b, 0, 0)),
                  pl.BlockSpec((1, 1, D), lambda b, i: (b, 0, 0)),
                  pl.BlockSpec((1, D), lambda b, i: (0, 0)),
                  pl.BlockSpec((D, N), lambda b, i: (0, 0))],
        out_specs=pl.BlockSpec((1, TM, N), lambda b, i: (b, i, 0)),
        out_shape=jax.ShapeDtypeStruct((B, L, N), F32),
        compiler_params=_cp("parallel", "parallel"),
        name="proj",
    )(x, sh, sc, g.reshape(1, D), w)


_KAP, _RT, _KT, _BT, _KH, _BH, _V = range(7)


def _rwkv_kernel(z_ref, zp_ref, zn_ref, mu_ref, w0_ref, w2h_ref, w2l_ref, a0_ref, a2h_ref, a2l_ref,
                 kk_ref, ka_ref, rk_ref, hs_ref, hb_ref, tri_ref, msl_ref, minc_ref, h0_ref,
                 y_ref, bv_ref, gs_ref, hT_ref, H_scr, nat_scr, pc_scr, *, n_chunks):
    d = pl.program_id(0)
    c = pl.program_id(2)
    cc = jnp.where(d == 0, c, n_chunks - 1 - c)
    C = CHUNK

    @pl.when(c == 0)
    def _():
        H_scr[...] = h0_ref[0, 0]

    first = cc == 0
    last = cc == n_chunks - 1
    row = lax.broadcasted_iota(jnp.int32, (C, 1), 0)

    def shifted(lo, hi):
        z = z_ref[0, :, lo:hi]
        prev_row = jnp.where(first, 0.0, zp_ref[0, 7:8, lo:hi])
        next_row = jnp.where(last, 0.0, zn_ref[0, 0:1, lo:hi])
        zp = jnp.where(row == 0, prev_row, pltpu.roll(z, 1, 0))
        zn = jnp.where(row == C - 1, next_row, pltpu.roll(z, C - 1, 0))
        return z + mu_ref[0:1, lo:hi] * (zp - z) + mu_ref[1:2, lo:hi] * (zn - z)

    hs = hs_ref[...]
    hb = hb_ref[...]

    def head_sum(t):
        return _head_sum(t, hs, hb)

    r = shifted(0, D_MODEL)
    k = shifted(D_MODEL, 2 * D_MODEL)
    v = shifted(2 * D_MODEL, 3 * D_MODEL)
    w1 = shifted(COL_W1, COL_A1)
    a1 = shifted(COL_A1, COL_G1)
    g1 = shifted(COL_G1, P_RWKV_PAD)

    wl = w0_ref[0] + _dot_split(jnp.tanh(w1), w2h_ref[0], w2l_ref[0])
    logw = -jnp.exp(-_softplus(-wl) - 0.5)
    a = _sigmoid(_dot_split(a1, a2h_ref[0], a2l_ref[0]) + a0_ref[0])
    kkr = k * kk_ref[...]
    kk = kkr / jnp.maximum(jnp.sqrt(head_sum(kkr * kkr)), 1e-12)
    kd = k * (1.0 + (a - 1.0) * ka_ref[...])
    bb = kk * a
    bv_ref[0, 0] = head_sum(r * kd * rk_ref[...]) * v
    gs_ref[0, 0] = _sigmoid(g1)

    lw_hi, lw_lo = _split(logw)
    g_in = _dot(tri_ref[0], lw_hi) + _dot(tri_ref[0], lw_lo)
    g_ex = g_in - logw
    g_c = jnp.sum(logw, axis=0, keepdims=True)
    e_inv = jnp.exp(-g_in)
    e_hat = jnp.exp(g_c - g_in)
    nat_scr[_KAP] = kk * jnp.exp(g_ex)
    nat_scr[_RT] = r * jnp.exp(g_in)
    nat_scr[_KT] = kd * e_inv
    nat_scr[_BT] = bb * e_inv
    nat_scr[_KH] = kd * e_hat
    nat_scr[_BH] = bb * e_hat
    nat_scr[_V] = v
    pc_scr[...] = jnp.exp(g_c)

    lane_head = lax.broadcasted_iota(jnp.int32, (C, GROUP_W), 1) // HEAD
    ii = lax.broadcasted_iota(jnp.int32, (GROUP_W, GROUP_W), 0)
    jj = lax.broadcasted_iota(jnp.int32, (GROUP_W, GROUP_W), 1)
    eye = ii == jj
    msl = msl_ref[0] > 0.5
    minc = minc_ref[0] > 0.5

    def stacked(t):
        return jnp.concatenate([jnp.where(lane_head == j, t, 0.0) for j in range(GROUP_HEADS)], axis=0)

    def collapse(t):
        return t[0:C] + t[C:2 * C] + t[2 * C:3 * C] + t[3 * C:4 * C]

    G = range(N_GROUPS)

    def nat(i, g):
        return nat_scr[i, :, g * GROUP_W:(g + 1) * GROUP_W]

    x_kap = [stacked(nat(_KAP, g)) for g in G]
    x_r = [stacked(nat(_RT, g)) for g in G]
    x_v = [_bf(stacked(nat(_V, g))) for g in G]
    x_bk = [jnp.concatenate([_bf(stacked(nat(_BT, g))), _bf(stacked(nat(_KT, g)))], axis=0) for g in G]
    a1m = [_dot_nt(_bf(x_kap[g]), x_bk[g]) for g in G]
    a2m = [_dot_nt(_bf(x_r[g]), x_bk[g]) for g in G]
    n_pow = [jnp.where(msl, a1m[g][:, :GROUP_W], 0.0) for g in G]
    a_kk = [_bf(jnp.where(msl, a1m[g][:, GROUP_W:], 0.0)) for g in G]
    a_rb = [_bf(jnp.where(minc, a2m[g][:, :GROUP_W], 0.0)) for g in G]
    a_rk = [_bf(jnp.where(minc, a2m[g][:, GROUP_W:], 0.0)) for g in G]
    g0 = [_dot(a_kk[g], x_v[g]) for g in G]
    y0 = [_dot(a_rk[g], x_v[g]) for g in G]
    eye_f = jnp.where(eye, 1.0, 0.0)
    p_inv = [eye_f - n_pow[g] for g in G]
    for _ in range(5):
        nb = [_bf(n_pow[g]) for g in G]
        n_pow = [_dot(nb[g], nb[g]) for g in G]
        p_inv = [p_inv[g] + _dot(_bf(p_inv[g]), _bf(n_pow[g])) for g in G]
    w12 = [_bf(_dot(_bf(p_inv[g]), jnp.concatenate([_bf(x_kap[g]), _bf(g0[g])], axis=1))) for g in G]
    aw = [_dot(a_rb[g], w12[g]) for g in G]
    bw = [_dot_tn(_bf(stacked(nat(_BH, g))), w12[g]) for g in G]
    kv = [_dot_tn(_bf(stacked(nat(_KH, g))), x_v[g]) for g in G]
    h_old = [_split(H_scr[g]) for g in G]
    ys = []
    h_new = []
    for g in G:
        h_hi, h_lo = h_old[g]
        qb = _bf(collapse(x_r[g] - aw[g][:, :GROUP_W]))
        ys.append(_dot(qb, h_hi) + _dot(qb, h_lo) + collapse(y0[g] - aw[g][:, GROUP_W:]))
        m_mat = jnp.where(eye, pc_scr[:, g * GROUP_W:(g + 1) * GROUP_W], 0.0) - bw[g][:, :GROUP_W]
        m_hi, m_lo = _split(m_mat)
        h_new.append(_dot(m_hi, h_hi) + _dot(m_hi, h_lo) + _dot(m_lo, h_hi) + kv[g] - bw[g][:, GROUP_W:])
    y_ref[0, 0] = jnp.concatenate(ys, axis=1)
    for g in G:
        H_scr[g] = h_new[g]

    @pl.when(c == n_chunks - 1)
    def _():
        hT_ref[0, 0] = H_scr[...]


def _rwkv(z, pr, h0):
    B, L, N = z.shape
    C = CHUNK
    nC = L // C
    D = D_MODEL
    nb8 = L // 8

    def cidx(d, c):
        return c + d * (nC - 1 - 2 * c)

    zmap = lambda d, b, c: (b, cidx(d, c), 0)
    pmap = lambda d, b, c: (b, jnp.maximum(cidx(d, c) * (C // 8) - 1, 0), 0)
    nmap = lambda d, b, c: (b, jnp.minimum((cidx(d, c) + 1) * (C // 8), nb8 - 1), 0)
    const2 = lambda d, b, c: (0, 0)
    dir3 = lambda d, b, c: (d, 0, 0)
    omap = lambda d, b, c: (d, b, cidx(d, c), 0)
    smap = lambda d, b, c: (d, b, 0, 0, 0)
    kern = functools.partial(_rwkv_kernel, n_chunks=nC)
    return pl.pallas_call(
        kern,
        grid=(2, B, nC),
        in_specs=[pl.BlockSpec((1, C, N), zmap),
                  pl.BlockSpec((1, 8, N), pmap),
                  pl.BlockSpec((1, 8, N), nmap),
                  pl.BlockSpec((2, N), const2),
                  pl.BlockSpec((1, 1, D), dir3),
                  pl.BlockSpec((1, 2 * LORA_W, D), dir3),
                  pl.BlockSpec((1, 2 * LORA_W, D), dir3),
                  pl.BlockSpec((1, 1, D), dir3),
                  pl.BlockSpec((1, 2 * LORA_A, D), dir3),
                  pl.BlockSpec((1, 2 * LORA_A, D), dir3),
                  pl.BlockSpec((1, D), const2),
                  pl.BlockSpec((1, D), const2),
                  pl.BlockSpec((1, D), const2),
                  pl.BlockSpec((D, 128), const2),
                  pl.BlockSpec((128, D), const2),
                  pl.BlockSpec((1, C, C), dir3),
                  pl.BlockSpec((1, GROUP_W, GROUP_W), dir3),
                  pl.BlockSpec((1, GROUP_W, GROUP_W), dir3),
                  pl.BlockSpec((1, 1, N_GROUPS, GROUP_W, GROUP_W), smap)],
        out_specs=[pl.BlockSpec((1, 1, C, D), omap),
                   pl.BlockSpec((1, 1, C, D), omap),
                   pl.BlockSpec((1, 1, C, LORA_G_PAD), omap),
                   pl.BlockSpec((1, 1, N_GROUPS, GROUP_W, GROUP_W), smap)],
        out_shape=[jax.ShapeDtypeStruct((2, B, L, D), F32),
                   jax.ShapeDtypeStruct((2, B, L, D), F32),
                   jax.ShapeDtypeStruct((2, B, L, LORA_G_PAD), F32),
                   jax.ShapeDtypeStruct((2, B, N_GROUPS, GROUP_W, GROUP_W), F32)],
        scratch_shapes=[pltpu.VMEM((N_GROUPS, GROUP_W, GROUP_W), F32),
                        pltpu.VMEM((7, C, D), F32),
                        pltpu.VMEM((1, D), F32)],
        compiler_params=_cp("arbitrary", "arbitrary", "arbitrary"),
        name="rwkv",
    )(z, z, z, pr["mu"], pr["w0"], *pr["w2p"], pr["a0"], *pr["a2p"], pr["k_k"], pr["k_a"], pr["r_k"],
      pr["hs"], pr["hb"], pr["tri"], pr["msl"], pr["minc"], h0)


def _conv_kernel(za_ref, zb_ref, w_ref, o_ref, upad, *, L, stride, rows_per_step):
    pad = CONV_HALF * stride
    TC = za_ref.shape[-1]
    upad[0:pad, :] = jnp.zeros((pad, TC), F32)
    upad[pad + L:pad + L + pad, :] = jnp.zeros((pad, TC), F32)
    upad[pad:pad + L, :] = za_ref[0] * _sigmoid(zb_ref[0])
    RB = rows_per_step

    def block(r0):
        acc = jnp.zeros((RB, TC), F32)
        for j in range(CONV_K):
            acc = acc + w_ref[j:j + 1, :] * upad[pl.ds(r0 + j * stride, RB), :]
        o_ref[0, pl.ds(r0, RB), :] = acc

    if stride % 8 == 0:
        def body(i, carry):
            block(pl.multiple_of(i * RB, RB))
            return carry
        lax.fori_loop(0, L // RB, body, 0)
    else:
        for i in range(L // RB):
            block(i * RB)


def _conv(zcg, conv_w, stride):
    B, L, _ = zcg.shape
    D = D_MODEL
    TC = 128
    nct = D // TC
    wpad = jnp.pad(conv_w, ((0, 32 - CONV_K), (0, 0)))
    RB = min(L, 128)
    kern = functools.partial(_conv_kernel, L=L, stride=stride, rows_per_step=RB)
    return pl.pallas_call(
        kern,
        grid=(B, nct),
        in_specs=[pl.BlockSpec((1, L, TC), lambda b, j: (b, 0, j)),
                  pl.BlockSpec((1, L, TC), lambda b, j: (b, 0, j + nct)),
                  pl.BlockSpec((32, TC), lambda b, j: (0, j))],
        out_specs=pl.BlockSpec((1, L, TC), lambda b, j: (b, 0, j)),
        out_shape=jax.ShapeDtypeStruct((B, L, D), F32),
        scratch_shapes=[pltpu.VMEM((L + 2 * CONV_HALF * stride, TC), F32)],
        compiler_params=_cp("parallel", "parallel"),
        name="conv",
    )(zcg, zcg, wpad)


def _post_kernel(x_ref, yf_ref, yb_ref, bf_ref, bb_ref, gs_ref, cv_ref, zg_ref,
                 lng_ref, lnb_ref, g2_ref, woa_ref, cng_ref, cnb_ref, wob_ref, gb_ref, wout_ref,
                 gt_ref, n2g_ref, sh2_ref, sc2_ref, hs_ref, hb_ref, xo_ref, hn_ref):
    D = D_MODEL
    hs = hs_ref[...]
    hb = hb_ref[...]

    def head_mean(t):
        return _head_sum(t, hs, hb) * (1.0 / HEAD)

    o = yf_ref[0, 0] + yb_ref[0, 0]
    oc = o - head_mean(o)
    on = oc * lax.rsqrt(head_mean(oc * oc) + GN_EPS) * lng_ref[...] + lnb_ref[...]
    on = on + bf_ref[0, 0] + bb_ref[0, 0]
    gate = _dot(_bf(gs_ref[0, 0]), g2_ref[...])
    y_a = _dot(_bf(on * gate), woa_ref[...])

    cv = cv_ref[0]
    cm = jnp.mean(cv, axis=-1, keepdims=True)
    cc = cv - cm
    cn = cc * lax.rsqrt(jnp.mean(cc * cc, axis=-1, keepdims=True) + LN_EPS) * cng_ref[...] + cnb_ref[...]
    y_b = _dot(_bf(cn * _sigmoid(cn)), wob_ref[...])

    gates = _sigmoid(zg_ref[0] + gb_ref[...])
    m = gates[:, :D] * y_a + gates[:, D:] * y_b
    xn = x_ref[0] + gt_ref[0] * _dot(_bf(m), wout_ref[...])
    xo_ref[0] = xn
    hn_ref[0] = _bf(_rms(xn, n2g_ref[...]) * (1.0 + sc2_ref[0]) + sh2_ref[0])


def _post(x, y, bv, gs, cv, zcg, pr, gt1, sh2, sc2):
    B, L, D = x.shape
    TM = min(L, 256)
    tok = lambda b, i: (b, i, 0)
    fwd = lambda b, i: (0, b, i, 0)
    bwd = lambda b, i: (1, b, i, 0)
    cst = lambda b, i: (0, 0)
    per_b = lambda b, i: (b, 0, 0)
    row = pl.BlockSpec((1, D), cst)
    mat = pl.BlockSpec((D, D), cst)
    return pl.pallas_call(
        _post_kernel,
        grid=(B, L // TM),
        in_specs=[pl.BlockSpec((1, TM, D), tok),
                  pl.BlockSpec((1, 1, TM, D), fwd), pl.BlockSpec((1, 1, TM, D), bwd),
                  pl.BlockSpec((1, 1, TM, D), fwd), pl.BlockSpec((1, 1, TM, D), bwd),
                  pl.BlockSpec((1, 1, TM, LORA_G_PAD), fwd),
                  pl.BlockSpec((1, TM, D), tok),
                  pl.BlockSpec((1, TM, 2 * D), lambda b, i: (b, i, 1)),
                  row, row, pl.BlockSpec((LORA_G_PAD, D), cst), mat,
                  row, row, mat, pl.BlockSpec((1, 2 * D), cst), mat,
                  pl.BlockSpec((1, 1, D), per_b), row,
                  pl.BlockSpec((1, 1, D), per_b), pl.BlockSpec((1, 1, D), per_b),
                  pl.BlockSpec((D, 128), cst), pl.BlockSpec((128, D), cst)],
        out_specs=[pl.BlockSpec((1, TM, D), tok), pl.BlockSpec((1, TM, D), tok)],
        out_shape=[jax.ShapeDtypeStruct((B, L, D), F32), jax.ShapeDtypeStruct((B, L, D), BF16)],
        compiler_params=_cp("parallel", "parallel"),
        name="post",
    )(x, y, y, bv, bv, gs, cv, zcg,
      pr["lnx_g"], pr["lnx_b"], pr["g2p"], pr["w_oA"], pr["cnorm_g"], pr["cnorm_b"], pr["w_oB"],
      pr["gate_b"], pr["w_out"], gt1, pr["norm2_g"], sh2, sc2, pr["hs"], pr["hb"])


def _top16(s, rowid):
    rank = jnp.full(s.shape, NOT_SELECTED, F32)
    cur = s
    vals = []
    for r in range(PEER_TOPK):
        m = jnp.max(cur, axis=0, keepdims=True)
        idx = jnp.min(jnp.where(cur == m, rowid, 1e9), axis=0, keepdims=True)
        sel = rowid == idx
        rank = jnp.where(sel, float(r), rank)
        cur = jnp.where(sel, NEG_INF, cur)
        vals.append(m)
    return rank, vals


def _top16_untied(s):
    rank = jnp.full(s.shape, NOT_SELECTED, F32)
    cur = s
    vals = []
    for r in range(PEER_TOPK):
        m = jnp.max(cur, axis=0, keepdims=True)
        sel = cur == m
        rank = jnp.where(sel, float(r), rank)
        cur = jnp.where(sel, NEG_INF, cur)
        vals.append(m)
    n_ranked = jnp.sum(jnp.where(rank < float(PEER_TOPK), 1.0, 0.0), axis=0, keepdims=True)
    return rank, vals, n_ranked


def _peer_prep_kernel(h_ref, wq_ref, keys_ref, r2_ref, na_ref, e1_ref, e2_ref, q_scr, rk_scr, vl_scr):
    TM = h_ref.shape[0]
    K = PEER_TOPK
    q_scr[...] = _bf(_dot(h_ref[...], wq_ref[...]))
    rowid = lax.broadcasted_iota(jnp.int32, (PEER_NKEYS, TM), 0).astype(F32)
    kaid = lax.broadcasted_iota(jnp.int32, (K, TM), 0).astype(F32)

    def head(h, carry):
        off = pl.multiple_of(h * 2 * PEER_HALF, 2 * PEER_HALF)
        s1 = _dot_nt(keys_ref[h, 0], q_scr[:, pl.ds(off, PEER_HALF)])
        s2 = _dot_nt(keys_ref[h, 1], q_scr[:, pl.ds(off + PEER_HALF, PEER_HALF)])
        n_max = jnp.zeros((1, TM), F32)
        for half, s in ((0, s1), (1, s2)):
            rank, vals, n_ranked = _top16_untied(s)
            rk_scr[half] = rank
            vl_scr[half] = jnp.concatenate(vals, axis=0)
            n_max = jnp.maximum(n_max, n_ranked)

        @pl.when(jnp.max(n_max) > float(K))
        def _():
            for half, s in ((0, s1), (1, s2)):
                rank, vals = _top16(s, rowid)
                rk_scr[half] = rank
                vl_scr[half] = jnp.concatenate(vals, axis=0)

        rank1 = rk_scr[0]
        rank2 = rk_scr[1]
        v1 = vl_scr[0]
        vals1 = [v1[r:r + 1] for r in range(K)]
        vals2 = [vl_scr[1, r:r + 1, :] for r in range(K)]
        taken = jnp.zeros((K, TM), F32)
        front = v1 + vals2[0]
        for _ in range(K):
            m = jnp.max(front, axis=0, keepdims=True)
            idx = jnp.min(jnp.where(front == m, kaid, 1e9), axis=0, keepdims=True)
            sel = kaid == idx
            taken = taken + jnp.where(sel, 1.0, 0.0)
            nxt = jnp.full((K, TM), NEG_INF, F32)
            for kb in range(1, K):
                nxt = jnp.where(taken == float(kb), vals2[kb], nxt)
            front = jnp.where(sel, v1 + nxt, front)
        e1k = jnp.exp(v1 - vals1[0])
        pref = jnp.zeros((1, TM), F32)
        zrow = jnp.zeros((K, TM), F32)
        for kb in range(K):
            pref = pref + jnp.exp(vals2[kb] - vals2[0])
            zrow = jnp.where(taken == float(kb + 1), pref, zrow)
        z = jnp.sum(e1k * zrow, axis=0, keepdims=True)
        na = jnp.zeros((PEER_NKEYS, TM), F32)
        for ka in range(K):
            na = jnp.where(rank1 == float(ka), taken[ka:ka + 1], na)
        r2_ref[h] = _bf(rank2)
        na_ref[h] = na
        e1_ref[h] = jnp.where(rank1 < float(K), jnp.exp(s1 - vals1[0]) / z, 0.0)
        e2_ref[h] = _bf(jnp.where(rank2 < float(K), jnp.exp(s2 - vals2[0]), 0.0))
        return carry

    lax.fori_loop(0, PEER_HEADS, head, 0)


def _peer_prep(hn, wq, keys, TM):
    T, D = hn.shape
    Q = wq.shape[1]
    shp = jax.ShapeDtypeStruct((PEER_HEADS, PEER_NKEYS, T), F32)
    shp_bf = jax.ShapeDtypeStruct((PEER_HEADS, PEER_NKEYS, T), BF16)
    ospec = pl.BlockSpec((PEER_HEADS, PEER_NKEYS, TM), lambda i: (0, 0, i))
    return pl.pallas_call(
        _peer_prep_kernel,
        grid=(T // TM,),
        in_specs=[pl.BlockSpec((TM, D), lambda i: (i, 0)),
                  pl.BlockSpec((D, Q), lambda i: (0, 0)),
                  pl.BlockSpec((PEER_HEADS, 2, PEER_NKEYS, PEER_HALF), lambda i: (0, 0, 0, 0))],
        out_specs=[ospec, ospec, ospec, ospec],
        out_shape=[shp_bf, shp, shp, shp_bf],
        scratch_shapes=[pltpu.VMEM((TM, Q), BF16),
                        pltpu.VMEM((2, PEER_NKEYS, TM), F32), pltpu.VMEM((2, PEER_TOPK, TM), F32)],
        compiler_params=_cp("parallel"),
        name="peer_prep",
    )(hn, wq, keys)


def _peer_dense_kernel(h_ref, r2_ref, na_ref, e1_ref, e2_ref, u_ref, vt_ref, x_ref, gt_ref, fg_ref,
                       o_ref, acc_t, ht_scr, s0_scr, s1_scr, aw0_scr, aw1_scr, *, block, n_blocks, final_norm):
    jj = pl.program_id(1)
    TM = h_ref.shape[0]
    SLAB = PEER_NKEYS
    EB = block

    @pl.when(jj == 0)
    def _():
        acc_t[...] = jnp.zeros_like(acc_t)
        s1_scr[...] = jnp.zeros_like(s1_scr)
        aw0_scr[...] = jnp.zeros_like(aw0_scr)
        ht_scr[...] = _bf(h_ref[...].astype(F32).T)

    zero_bf = jnp.zeros((SLAB, TM), BF16)

    def row_tile(ref, h, a):
        t = _bf(jnp.broadcast_to(ref[h, pl.ds(a, 1), :], (16, TM)))
        return jnp.concatenate([t] * (SLAB // 16), axis=0)

    def vpu_stage(k, s_ref, aw_ref):
        live = jnp.logical_and(k >= 0, k < n_blocks)
        kc = jnp.clip(k, 0, n_blocks - 1)
        for p in range(EB // (2 * SLAB)):
            lo = p * 2 * SLAB
            s = s_ref[lo:lo + 2 * SLAB, :]
            act = 0.5 * s * (1.0 + lax.erf(s * SQRT_HALF))
            weights = []
            for half in range(2):
                a = kc * (EB // SLAB) + p * 2 + half
                w = jnp.zeros((SLAB, TM), BF16)
                for h in range(PEER_HEADS):
                    chosen = r2_ref[h] < row_tile(na_ref, h, a)
                    w = w + jnp.where(chosen, e2_ref[h], zero_bf) * row_tile(e1_ref, h, a)
                weights.append(w)
            aw = _bf(act) * jnp.concatenate(weights, axis=0)
            aw_ref[lo:lo + 2 * SLAB, :] = jnp.where(live, aw, jnp.zeros_like(aw))

    out0 = _dot(vt_ref[:, 0:EB], aw0_scr[...])
    vpu_stage(2 * jj - 1, s1_scr, aw1_scr)
    s0_scr[...] = _dot(u_ref[0:EB, :], ht_scr[...])
    out1 = _dot(vt_ref[:, EB:2 * EB], aw1_scr[...])
    vpu_stage(2 * jj, s0_scr, aw0_scr)
    s1_scr[...] = _dot(u_ref[EB:2 * EB, :], ht_scr[...])
    acc_t[...] += out0 + out1

    @pl.when(jj == pl.num_programs(1) - 1)
    def _():
        xn = x_ref[...] + gt_ref[0] * acc_t[...].T
        o_ref[...] = _rms(xn, fg_ref[...]) if final_norm else xn


def _peer_dense(x, hn, sel, u, vt, gt2, TM, tiles_per_batch, final_g):
    T, D = x.shape
    E = u.shape[0]
    EB = 1024
    final_norm = final_g is not None
    fg = (final_g if final_norm else jnp.ones((D,), F32)).reshape(1, D)
    sspec = pl.BlockSpec((PEER_HEADS, PEER_NKEYS, TM), lambda i, e: (0, 0, i))
    nE = E // EB
    nP = nE // 2
    kern = functools.partial(_peer_dense_kernel, block=EB, n_blocks=nE, final_norm=final_norm)
    return pl.pallas_call(
        kern,
        grid=(T // TM, nP + 1),
        in_specs=[pl.BlockSpec((TM, D), lambda i, e: (i, 0)),
                  sspec, sspec, sspec, sspec,
                  pl.BlockSpec((2 * EB, D), lambda i, e: (jnp.minimum(e, nP - 1), 0)),
                  pl.BlockSpec((D, 2 * EB), lambda i, e: (0, jnp.maximum(e - 1, 0))),
                  pl.BlockSpec((TM, D), lambda i, e: (i, 0)),
                  pl.BlockSpec((1, 1, D), lambda i, e: (i // tiles_per_batch, 0, 0)),
                  pl.BlockSpec((1, D), lambda i, e: (0, 0))],
        out_specs=pl.BlockSpec((TM, D), lambda i, e: (i, 0)),
        out_shape=jax.ShapeDtypeStruct((T, D), F32),
        scratch_shapes=[pltpu.VMEM((D, TM), F32), pltpu.VMEM((D, TM), BF16),
                        pltpu.VMEM((EB, TM), F32), pltpu.VMEM((EB, TM), F32),
                        pltpu.VMEM((EB, TM), BF16), pltpu.VMEM((EB, TM), BF16)],
        compiler_params=_cp("parallel", "arbitrary"),
        name="peer_dense",
    )(hn, *sel, u, vt, x, gt2, fg)


def _direction_masks():
    t = jnp.arange(CHUNK)
    le = (t[None, :] <= t[:, None]).astype(F32)
    tri = jnp.stack([le, le.T])
    i = jnp.arange(GROUP_W)
    same = (i[:, None] // CHUNK) == (i[None, :] // CHUNK)
    ti, tj = i[:, None] % CHUNK, i[None, :] % CHUNK
    msl = jnp.stack([same & (tj < ti), same & (tj > ti)]).astype(F32)
    minc = jnp.stack([same & (tj <= ti), same & (tj >= ti)]).astype(F32)
    return tri, msl, minc


def _layer_params(l, w_in, shift_mu, w0, w2, a0, a2, g2, k_k, k_a, r_k, lnx_g, lnx_b, w_oA, conv_w,
                  cnorm_g, cnorm_b, w_oB, gate_b, w_out, norm2_g):
    D = D_MODEL
    row = lambda t: t.reshape(1, -1)
    zeros = jnp.zeros((LORA_W, D), F32)
    tri, msl, minc = _direction_masks()
    hs = (jnp.arange(D)[:, None] // HEAD == jnp.arange(128)[None, :]).astype(F32)
    pad_cols = P_RWKV_PAD - P_RWKV
    return dict(
        w_rk=_bf(jnp.pad(w_in[l][:, :P_RWKV], ((0, 0), (0, pad_cols)))),
        w_cg=_bf(w_in[l][:, P_RWKV:]),
        mu=jnp.pad(shift_mu[l], ((0, 0), (0, pad_cols))),
        w0=w0[l].reshape(2, 1, D), a0=a0[l].reshape(2, 1, D),
        w2p=_split(jnp.stack([jnp.concatenate([w2[l, 0], zeros]), jnp.concatenate([zeros, w2[l, 1]])])),
        a2p=_split(jnp.stack([jnp.concatenate([a2[l, 0], zeros]), jnp.concatenate([zeros, a2[l, 1]])])),
        g2p=_bf(jnp.pad(g2[l], ((0, LORA_G_PAD - LORA_G), (0, 0)))),
        k_k=row(k_k[l]), k_a=row(k_a[l]), r_k=row(r_k[l]),
        lnx_g=row(lnx_g[l]), lnx_b=row(lnx_b[l]), w_oA=_bf(w_oA[l]),
        conv_w=conv_w[l], cnorm_g=row(cnorm_g[l]), cnorm_b=row(cnorm_b[l]), w_oB=_bf(w_oB[l]),
        gate_b=row(gate_b[l]), w_out=_bf(w_out[l]), norm2_g=row(norm2_g[l]),
        hs=_bf(hs), hb=_bf(hs.T), tri=_bf(tri), msl=msl, minc=minc)


def _mixer(x, mod, pr, norm1_g, stride, h0, emit):
    B, L, D = x.shape
    sh1, sc1, gt1, sh2, sc2 = (mod[:, i:i + 1, :] for i in range(5))
    z_rk = _proj(x, sh1, sc1, norm1_g, pr["w_rk"])
    y, bv, gs, h_t = _rwkv(z_rk, pr, h0)
    if not emit:
        return None, None, h_t
    z_cg = _proj(x, sh1, sc1, norm1_g, pr["w_cg"])
    cv = _conv(z_cg, pr["conv_w"], stride)
    xn, hn = _post(x, y, bv, gs, cv, z_cg, pr, gt1, sh2, sc2)
    return xn, hn, h_t


def _peer(x, hn, wq, keys, u, v, gt2, final_g=None):
    B, L, D = x.shape
    T = B * L
    TM = min(L, 256)
    sel = _peer_prep(hn.reshape(T, D), wq, keys, TM)
    out = _peer_dense(x.reshape(T, D), hn.reshape(T, D), sel, u, v, gt2, TM, L // TM, final_g)
    return out.reshape(B, L, D)


def kernel(x, c, ctx, c_ctx, ada_w, ada_b, norm1_g, norm2_g, w_in, shift_mu, w0, w2, a0, a2, g2, k_k, k_a, r_k, lnx_g, lnx_b, w_oA, conv_w, cnorm_g, cnorm_b, w_oB, gate_b, w_out, w_q, sub_keys, peer_u, peer_v, final_g):
    B, L, D = x.shape
    depth = ada_w.shape[0]
    xc = ctx
    n_rows = -(-(B + 1) // 8) * 8
    c_rows = jnp.pad(jnp.concatenate([c, c_ctx[None, :]], axis=0), ((0, n_rows - B - 1), (0, 0)))
    zero_state = jnp.zeros((2, B, N_GROUPS, GROUP_W, GROUP_W), F32)
    for l in range(depth):
        last = l == depth - 1
        pr = _layer_params(l, w_in, shift_mu, w0, w2, a0, a2, g2, k_k, k_a, r_k, lnx_g, lnx_b, w_oA,
                           conv_w, cnorm_g, cnorm_b, w_oB, gate_b, w_out, norm2_g)
        mod_all = _modulation(c_rows, ada_w[l], ada_b[l])
        mod = mod_all[:B].reshape(B, 6, D)
        modc = jnp.broadcast_to(mod_all[B].reshape(1, 6, D), (B, 6, D))
        wq = _bf(w_q[l])
        keys = _bf(sub_keys[l])
        u = _bf(peer_u[l])
        v = _bf(peer_v[l]).T

        xc_new, hnc, ctx_states = _mixer(xc, modc, pr, norm1_g[l], 1, zero_state, emit=not last)
        xn, hn, _ = _mixer(x, mod, pr, norm1_g[l], GRID_W, ctx_states, emit=True)
        x = _peer(xn, hn, wq, keys, u, v, mod[:, 5:6, :], final_g if last else None)
        if not last:
            xc = _peer(xc_new, hnc, wq, keys, u, v, modc[:, 5:6, :])
    return x
```

```python
import functools

import jax
import jax.numpy as jnp
from jax import lax
from jax.experimental import pallas as pl
from jax.experimental.pallas import tpu as pltpu

F32 = jnp.float32
BF16 = jnp.bfloat16
HI = lax.Precision.HIGHEST

D_MODEL = 1024
HEAD = 64
HEADS = D_MODEL // HEAD
GROUP_HEADS = 4
GROUP_W = GROUP_HEADS * HEAD
N_GROUPS = HEADS // GROUP_HEADS
CHUNK = 64
LORA_W = 64
LORA_A = 64
LORA_G = 160
LORA_G_PAD = 256
P_RWKV = 3 * D_MODEL + 2 * LORA_W + 2 * LORA_A + LORA_G
P_RWKV_PAD = 3 * D_MODEL + 2 * LORA_W + 2 * LORA_A + LORA_G_PAD
COL_W1 = 3 * D_MODEL
COL_A1 = COL_W1 + 2 * LORA_W
COL_G1 = COL_A1 + 2 * LORA_A
CONV_K = 31
CONV_HALF = CONV_K // 2
GRID_W = 64
PEER_HEADS = 8
PEER_NKEYS = 128
PEER_HALF = 128
PEER_TOPK = 16
NORM_EPS = 1e-6
LN_EPS = 1e-5
GN_EPS = HEAD * 1e-5
VMEM_LIMIT = 56 * 1024 * 1024
NOT_SELECTED = 99.0
NEG_INF = float("-inf")
SQRT_HALF = 0.7071067811865476


def _cp(*sem):
    return pltpu.CompilerParams(dimension_semantics=sem, vmem_limit_bytes=VMEM_LIMIT)


def _dot(a, b):
    return jnp.dot(a, b, preferred_element_type=F32)


def _dot_hi(a, b):
    return jnp.dot(a, b, precision=HI, preferred_element_type=F32)


def _dot_nt(a, b):
    return lax.dot_general(a, b, (((1,), (1,)), ((), ())), preferred_element_type=F32)


def _dot_tn(a, b):
    return lax.dot_general(a, b, (((0,), (0,)), ((), ())), preferred_element_type=F32)


def _bf(a):
    return a.astype(BF16)


def _split(a):
    hi = a.astype(BF16)
    return hi, (a - hi.astype(F32)).astype(BF16)


def _dot_split_lhs(a, b_bf):
    hi, lo = _split(a)
    return _dot(hi, b_bf) + _dot(lo, b_bf)


def _dot_split(a, b_hi, b_lo):
    hi, lo = _split(a)
    return _dot(hi, b_hi) + _dot(hi, b_lo) + _dot(lo, b_hi)


def _head_sum(t, hs_bf, hb_bf):
    return _dot_split_lhs(_dot_split_lhs(t, hs_bf), hb_bf)


def _sigmoid(x):
    return 1.0 / (1.0 + jnp.exp(-x))


def _softplus(x):
    return jnp.maximum(x, 0.0) + jnp.log(1.0 + jnp.exp(-jnp.abs(x)))


def _rms(x, g):
    return x * lax.rsqrt(jnp.mean(x * x, axis=-1, keepdims=True) + NORM_EPS) * g


def _mod_kernel(c_ref, w_ref, b_ref, o_ref):
    c = c_ref[...]
    o_ref[...] = _dot_hi(c * _sigmoid(c), w_ref[...]) + b_ref[...]


def _modulation(c_rows, ada_w, ada_b):
    R, D = c_rows.shape
    N = ada_w.shape[1]
    TN = 512
    return pl.pallas_call(
        _mod_kernel,
        grid=(N // TN,),
        in_specs=[pl.BlockSpec((R, D), lambda j: (0, 0)),
                  pl.BlockSpec((D, TN), lambda j: (0, j)),
                  pl.BlockSpec((1, TN), lambda j: (0, j))],
        out_specs=pl.BlockSpec((R, TN), lambda j: (0, j)),
        out_shape=jax.ShapeDtypeStruct((R, N), F32),
        compiler_params=_cp("arbitrary"),
        name="modulation",
    )(c_rows, ada_w, ada_b.reshape(1, N))


def _proj_kernel(x_ref, sh_ref, sc_ref, g_ref, w_ref, o_ref):
    h = _rms(x_ref[0], g_ref[...]) * (1.0 + sc_ref[0]) + sh_ref[0]
    o_ref[0] = _dot(_bf(h), w_ref[...])


def _proj(x, sh, sc, g, w):
    B, L, D = x.shape
    N = w.shape[1]
    TM = min(L, 256)
    return pl.pallas_call(
        _proj_kernel,
        grid=(B, L // TM),
        in_specs=[pl.BlockSpec((1, TM, D), lambda b, i: (b, i, 0)),
                  pl.BlockSpec((1, 1, D), lambda b, i: (b, 0, 0)),
                  pl.BlockSpec((1, 1, D), lambda b, i: (b, 0, 0)),
                  pl.BlockSpec((1, D), lambda b, i: (0, 0)),
                  pl.BlockSpec((D, N), lambda b, i: (0, 0))],
        out_specs=pl.BlockSpec((1, TM, N), lambda b, i: (b, i, 0)),
        out_shape=jax.ShapeDtypeStruct((B, L, N), F32),
        compiler_params=_cp("parallel", "parallel"),
        name="proj",
    )(x, sh, sc, g.reshape(1, D), w)


_KAP, _RT, _KT, _BT, _KH, _BH, _V = range(7)


def _rwkv_kernel(z_ref, zp_ref, zn_ref, mu_ref, w0_ref, w2h_ref, w2l_ref, a0_ref, a2h_ref, a2l_ref,
                 kk_ref, ka_ref, rk_ref, hs_ref, hb_ref, tri_ref, msl_ref, minc_ref, h0_ref,
                 y_ref, bv_ref, gs_ref, hT_ref, H_scr, nat_scr, pc_scr, *, n_chunks):
    d = pl.program_id(0)
    c = pl.program_id(2)
    cc = jnp.where(d == 0, c, n_chunks - 1 - c)
    C = CHUNK

    @pl.when(c == 0)
    def _():
        H_scr[...] = h0_ref[0, 0]

    first = cc == 0
    last = cc == n_chunks - 1
    row = lax.broadcasted_iota(jnp.int32, (C, 1), 0)

    def shifted(lo, hi):
        z = z_ref[0, :, lo:hi]
        prev_row = jnp.where(first, 0.0, zp_ref[0, 7:8, lo:hi])
        next_row = jnp.where(last, 0.0, zn_ref[0, 0:1, lo:hi])
        zp = jnp.where(row == 0, prev_row, pltpu.roll(z, 1, 0))
        zn = jnp.where(row == C - 1, next_row, pltpu.roll(z, C - 1, 0))
        return z + mu_ref[0:1, lo:hi] * (zp - z) + mu_ref[1:2, lo:hi] * (zn - z)

    hs = hs_ref[...]
    hb = hb_ref[...]

    def head_sum(t):
        return _head_sum(t, hs, hb)

    r = shifted(0, D_MODEL)
    k = shifted(D_MODEL, 2 * D_MODEL)
    v = shifted(2 * D_MODEL, 3 * D_MODEL)
    w1 = shifted(COL_W1, COL_A1)
    a1 = shifted(COL_A1, COL_G1)
    g1 = shifted(COL_G1, P_RWKV_PAD)

    wl = w0_ref[0] + _dot_split(jnp.tanh(w1), w2h_ref[0], w2l_ref[0])
    logw = -jnp.exp(-_softplus(-wl) - 0.5)
    a = _sigmoid(_dot_split(a1, a2h_ref[0], a2l_ref[0]) + a0_ref[0])
    kkr = k * kk_ref[...]
    kk = kkr / jnp.maximum(jnp.sqrt(head_sum(kkr * kkr)), 1e-12)
    kd = k * (1.0 + (a - 1.0) * ka_ref[...])
    bb = kk * a
    bv_ref[0, 0] = head_sum(r * kd * rk_ref[...]) * v
    gs_ref[0, 0] = _sigmoid(g1)

    lw_hi, lw_lo = _split(logw)
    g_in = _dot(tri_ref[0], lw_hi) + _dot(tri_ref[0], lw_lo)
    g_ex = g_in - logw
    g_c = jnp.sum(logw, axis=0, keepdims=True)
    e_inv = jnp.exp(-g_in)
    e_hat = jnp.exp(g_c - g_in)
    nat_scr[_KAP] = kk * jnp.exp(g_ex)
    nat_scr[_RT] = r * jnp.exp(g_in)
    nat_scr[_KT] = kd * e_inv
    nat_scr[_BT] = bb * e_inv
    nat_scr[_KH] = kd * e_hat
    nat_scr[_BH] = bb * e_hat
    nat_scr[_V] = v
    pc_scr[...] = jnp.exp(g_c)

    lane_head = lax.broadcasted_iota(jnp.int32, (C, GROUP_W), 1) // HEAD
    ii = lax.broadcasted_iota(jnp.int32, (GROUP_W, GROUP_W), 0)
    jj = lax.broadcasted_iota(jnp.int32, (GROUP_W, GROUP_W), 1)
    eye = ii == jj
    msl = msl_ref[0] > 0.5
    minc = minc_ref[0] > 0.5

    def stacked(t):
        return jnp.concatenate([jnp.where(lane_head == j, t, 0.0) for j in range(GROUP_HEADS)], axis=0)

    def collapse(t):
        return t[0:C] + t[C:2 * C] + t[2 * C:3 * C] + t[3 * C:4 * C]

    G = range(N_GROUPS)

    def nat(i, g):
        return nat_scr[i, :, g * GROUP_W:(g + 1) * GROUP_W]

    x_kap = [stacked(nat(_KAP, g)) for g in G]
    x_v = [_bf(stacked(nat(_V, g))) for g in G]
    x_bk = [jnp.concatenate([_bf(stacked(nat(_BT, g))), _bf(stacked(nat(_KT, g)))], axis=0) for g in G]
    kr = [_bf(jnp.concatenate([nat(_KAP, g), nat(_RT, g)], axis=0)) for g in G]
    akr = [_dot_nt(kr[g], x_bk[g]) for g in G]

    def tiled(t):
        return jnp.concatenate([t] * GROUP_HEADS, axis=0)

    n_pow = [jnp.where(msl, tiled(akr[g][:C, :GROUP_W]), 0.0) for g in G]
    msl_c = collapse(msl_ref[0]) > 0.5
    minc_c = collapse(minc_ref[0]) > 0.5
    n_side = [jnp.where(msl_c, akr[g][:C, :GROUP_W], 0.0) for g in G]
    a_kk = [_bf(jnp.where(msl_c, akr[g][:C, GROUP_W:], 0.0)) for g in G]
    a_rb = [_bf(jnp.where(minc_c, akr[g][C:, :GROUP_W], 0.0)) for g in G]
    a_rk = [_bf(jnp.where(minc_c, akr[g][C:, GROUP_W:], 0.0)) for g in G]
    g0 = [stacked(_dot(a_kk[g], x_v[g])) for g in G]
    y0 = [_dot(a_rk[g], x_v[g]) for g in G]
    eye_side = collapse(jnp.where(eye, 1.0, 0.0))
    p_inv = [eye_side - n_side[g] for g in G]
    for _ in range(5):
        nb = [_bf(n_pow[g]) for g in G]
        n_pow = [_dot(nb[g], nb[g]) for g in G]
        p_inv = [p_inv[g] + _dot(_bf(p_inv[g]), _bf(n_pow[g])) for g in G]
    w_nat = [_dot(_bf(p_inv[g]), jnp.concatenate([_bf(x_kap[g]), _bf(g0[g])], axis=1)) for g in G]
    w12 = [_bf(jnp.concatenate([stacked(w_nat[g][:, :GROUP_W]), stacked(w_nat[g][:, GROUP_W:])], axis=1))
           for g in G]
    aw = [_dot(a_rb[g], w12[g]) for g in G]
    bw = [_dot_tn(_bf(stacked(nat(_BH, g))), w12[g]) for g in G]
    kv = [_dot_tn(_bf(stacked(nat(_KH, g))), x_v[g]) for g in G]
    h_old = [_split(H_scr[g]) for g in G]
    ys = []
    h_new = []
    for g in G:
        h_hi, h_lo = h_old[g]
        qb = _bf(nat(_RT, g) - aw[g][:, :GROUP_W])
        ys.append(_dot(qb, h_hi) + _dot(qb, h_lo) + y0[g] - aw[g][:, GROUP_W:])
        m_mat = jnp.where(eye, pc_scr[:, g * GROUP_W:(g + 1) * GROUP_W], 0.0) - bw[g][:, :GROUP_W]
        m_hi, m_lo = _split(m_mat)
        h_new.append(_dot(m_hi, h_hi) + _dot(m_hi, h_lo) + _dot(m_lo, h_hi) + kv[g] - bw[g][:, GROUP_W:])
    y_ref[0, 0] = jnp.concatenate(ys, axis=1)
    for g in G:
        H_scr[g] = h_new[g]

    @pl.when(c == n_chunks - 1)
    def _():
        hT_ref[0, 0] = H_scr[...]


def _rwkv(z, pr, h0):
    B, L, N = z.shape
    C = CHUNK
    nC = L // C
    D = D_MODEL
    nb8 = L // 8

    def cidx(d, c):
        return c + d * (nC - 1 - 2 * c)

    zmap = lambda d, b, c: (b, cidx(d, c), 0)
    pmap = lambda d, b, c: (b, jnp.maximum(cidx(d, c) * (C // 8) - 1, 0), 0)
    nmap = lambda d, b, c: (b, jnp.minimum((cidx(d, c) + 1) * (C // 8), nb8 - 1), 0)
    const2 = lambda d, b, c: (0, 0)
    dir3 = lambda d, b, c: (d, 0, 0)
    omap = lambda d, b, c: (d, b, cidx(d, c), 0)
    smap = lambda d, b, c: (d, b, 0, 0, 0)
    kern = functools.partial(_rwkv_kernel, n_chunks=nC)
    return pl.pallas_call(
        kern,
        grid=(2, B, nC),
        in_specs=[pl.BlockSpec((1, C, N), zmap),
                  pl.BlockSpec((1, 8, N), pmap),
                  pl.BlockSpec((1, 8, N), nmap),
                  pl.BlockSpec((2, N), const2),
                  pl.BlockSpec((1, 1, D), dir3),
                  pl.BlockSpec((1, 2 * LORA_W, D), dir3),
                  pl.BlockSpec((1, 2 * LORA_W, D), dir3),
                  pl.BlockSpec((1, 1, D), dir3),
                  pl.BlockSpec((1, 2 * LORA_A, D), dir3),
                  pl.BlockSpec((1, 2 * LORA_A, D), dir3),
                  pl.BlockSpec((1, D), const2),
                  pl.BlockSpec((1, D), const2),
                  pl.BlockSpec((1, D), const2),
                  pl.BlockSpec((D, 128), const2),
                  pl.BlockSpec((128, D), const2),
                  pl.BlockSpec((1, C, C), dir3),
                  pl.BlockSpec((1, GROUP_W, GROUP_W), dir3),
                  pl.BlockSpec((1, GROUP_W, GROUP_W), dir3),
                  pl.BlockSpec((1, 1, N_GROUPS, GROUP_W, GROUP_W), smap)],
        out_specs=[pl.BlockSpec((1, 1, C, D), omap),
                   pl.BlockSpec((1, 1, C, D), omap),
                   pl.BlockSpec((1, 1, C, LORA_G_PAD), omap),
                   pl.BlockSpec((1, 1, N_GROUPS, GROUP_W, GROUP_W), smap)],
        out_shape=[jax.ShapeDtypeStruct((2, B, L, D), F32),
                   jax.ShapeDtypeStruct((2, B, L, D), F32),
                   jax.ShapeDtypeStruct((2, B, L, LORA_G_PAD), F32),
                   jax.ShapeDtypeStruct((2, B, N_GROUPS, GROUP_W, GROUP_W), F32)],
        scratch_shapes=[pltpu.VMEM((N_GROUPS, GROUP_W, GROUP_W), F32),
                        pltpu.VMEM((7, C, D), F32),
                        pltpu.VMEM((1, D), F32)],
        compiler_params=_cp("arbitrary", "arbitrary", "arbitrary"),
        name="rwkv",
    )(z, z, z, pr["mu"], pr["w0"], *pr["w2p"], pr["a0"], *pr["a2p"], pr["k_k"], pr["k_a"], pr["r_k"],
      pr["hs"], pr["hb"], pr["tri"], pr["msl"], pr["minc"], h0)


def _conv_kernel(za_ref, zb_ref, w_ref, o_ref, upad, *, L, stride, rows_per_step):
    pad = CONV_HALF * stride
    TC = za_ref.shape[-1]
    upad[0:pad, :] = jnp.zeros((pad, TC), F32)
    upad[pad + L:pad + L + pad, :] = jnp.zeros((pad, TC), F32)
    upad[pad:pad + L, :] = za_ref[0] * _sigmoid(zb_ref[0])
    RB = rows_per_step

    def block(r0):
        acc = jnp.zeros((RB, TC), F32)
        for j in range(CONV_K):
            acc = acc + w_ref[j:j + 1, :] * upad[pl.ds(r0 + j * stride, RB), :]
        o_ref[0, pl.ds(r0, RB), :] = acc

    if stride % 8 == 0:
        def body(i, carry):
            block(pl.multiple_of(i * RB, RB))
            return carry
        lax.fori_loop(0, L // RB, body, 0)
    else:
        for i in range(L // RB):
            block(i * RB)


def _conv(zcg, conv_w, stride):
    B, L, _ = zcg.shape
    D = D_MODEL
    TC = 128
    nct = D // TC
    wpad = jnp.pad(conv_w, ((0, 32 - CONV_K), (0, 0)))
    RB = min(L, 128)
    kern = functools.partial(_conv_kernel, L=L, stride=stride, rows_per_step=RB)
    return pl.pallas_call(
        kern,
        grid=(B, nct),
        in_specs=[pl.BlockSpec((1, L, TC), lambda b, j: (b, 0, j)),
                  pl.BlockSpec((1, L, TC), lambda b, j: (b, 0, j + nct)),
                  pl.BlockSpec((32, TC), lambda b, j: (0, j))],
        out_specs=pl.BlockSpec((1, L, TC), lambda b, j: (b, 0, j)),
        out_shape=jax.ShapeDtypeStruct((B, L, D), F32),
        scratch_shapes=[pltpu.VMEM((L + 2 * CONV_HALF * stride, TC), F32)],
        compiler_params=_cp("parallel", "parallel"),
        name="conv",
    )(zcg, zcg, wpad)


def _post_kernel(x_ref, yf_ref, yb_ref, bf_ref, bb_ref, gs_ref, cv_ref, zg_ref,
                 lng_ref, lnb_ref, g2_ref, woa_ref, cng_ref, cnb_ref, wob_ref, gb_ref, wout_ref,
                 gt_ref, n2g_ref, sh2_ref, sc2_ref, hs_ref, hb_ref, xo_ref, hn_ref):
    D = D_MODEL
    hs = hs_ref[...]
    hb = hb_ref[...]

    def head_mean(t):
        return _head_sum(t, hs, hb) * (1.0 / HEAD)

    o = yf_ref[0, 0] + yb_ref[0, 0]
    oc = o - head_mean(o)
    on = oc * lax.rsqrt(head_mean(oc * oc) + GN_EPS) * lng_ref[...] + lnb_ref[...]
    on = on + bf_ref[0, 0] + bb_ref[0, 0]
    gate = _dot(_bf(gs_ref[0, 0]), g2_ref[...])
    y_a = _dot(_bf(on * gate), woa_ref[...])

    cv = cv_ref[0]
    cm = jnp.mean(cv, axis=-1, keepdims=True)
    cc = cv - cm
    cn = cc * lax.rsqrt(jnp.mean(cc * cc, axis=-1, keepdims=True) + LN_EPS) * cng_ref[...] + cnb_ref[...]
    y_b = _dot(_bf(cn * _sigmoid(cn)), wob_ref[...])

    gates = _sigmoid(zg_ref[0] + gb_ref[...])
    m = gates[:, :D] * y_a + gates[:, D:] * y_b
    xn = x_ref[0] + gt_ref[0] * _dot(_bf(m), wout_ref[...])
    xo_ref[0] = xn
    hn_ref[0] = _bf(_rms(xn, n2g_ref[...]) * (1.0 + sc2_ref[0]) + sh2_ref[0])


def _post(x, y, bv, gs, cv, zcg, pr, gt1, sh2, sc2):
    B, L, D = x.shape
    TM = min(L, 256)
    tok = lambda b, i: (b, i, 0)
    fwd = lambda b, i: (0, b, i, 0)
    bwd = lambda b, i: (1, b, i, 0)
    cst = lambda b, i: (0, 0)
    per_b = lambda b, i: (b, 0, 0)
    row = pl.BlockSpec((1, D), cst)
    mat = pl.BlockSpec((D, D), cst)
    return pl.pallas_call(
        _post_kernel,
        grid=(B, L // TM),
        in_specs=[pl.BlockSpec((1, TM, D), tok),
                  pl.BlockSpec((1, 1, TM, D), fwd), pl.BlockSpec((1, 1, TM, D), bwd),
                  pl.BlockSpec((1, 1, TM, D), fwd), pl.BlockSpec((1, 1, TM, D), bwd),
                  pl.BlockSpec((1, 1, TM, LORA_G_PAD), fwd),
                  pl.BlockSpec((1, TM, D), tok),
                  pl.BlockSpec((1, TM, 2 * D), lambda b, i: (b, i, 1)),
                  row, row, pl.BlockSpec((LORA_G_PAD, D), cst), mat,
                  row, row, mat, pl.BlockSpec((1, 2 * D), cst), mat,
                  pl.BlockSpec((1, 1, D), per_b), row,
                  pl.BlockSpec((1, 1, D), per_b), pl.BlockSpec((1, 1, D), per_b),
                  pl.BlockSpec((D, 128), cst), pl.BlockSpec((128, D), cst)],
        out_specs=[pl.BlockSpec((1, TM, D), tok), pl.BlockSpec((1, TM, D), tok)],
        out_shape=[jax.ShapeDtypeStruct((B, L, D), F32), jax.ShapeDtypeStruct((B, L, D), BF16)],
        compiler_params=_cp("parallel", "parallel"),
        name="post",
    )(x, y, y, bv, bv, gs, cv, zcg,
      pr["lnx_g"], pr["lnx_b"], pr["g2p"], pr["w_oA"], pr["cnorm_g"], pr["cnorm_b"], pr["w_oB"],
      pr["gate_b"], pr["w_out"], gt1, pr["norm2_g"], sh2, sc2, pr["hs"], pr["hb"])


def _top16(s, rowid):
    rank = jnp.full(s.shape, NOT_SELECTED, F32)
    cur = s
    vals = []
    for r in range(PEER_TOPK):
        m = jnp.max(cur, axis=0, keepdims=True)
        idx = jnp.min(jnp.where(cur == m, rowid, 1e9), axis=0, keepdims=True)
        sel = rowid == idx
        rank = jnp.where(sel, float(r), rank)
        cur = jnp.where(sel, NEG_INF, cur)
        vals.append(m)
    return rank, vals


def _top16_untied(s):
    rank = jnp.full(s.shape, NOT_SELECTED, F32)
    cur = s
    vals = []
    for r in range(PEER_TOPK):
        m = jnp.max(cur, axis=0, keepdims=True)
        sel = cur == m
        rank = jnp.where(sel, float(r), rank)
        cur = jnp.where(sel, NEG_INF, cur)
        vals.append(m)
    n_ranked = jnp.sum(jnp.where(rank < float(PEER_TOPK), 1.0, 0.0), axis=0, keepdims=True)
    return rank, vals, n_ranked


def _peer_prep_kernel(h_ref, wq_ref, keys_ref, r2_ref, na_ref, e1_ref, e2_ref, q_scr, rk_scr, vl_scr):
    TM = h_ref.shape[0]
    K = PEER_TOPK
    q_scr[...] = _bf(_dot(h_ref[...], wq_ref[...]))
    rowid = lax.broadcasted_iota(jnp.int32, (PEER_NKEYS, TM), 0).astype(F32)
    kaid = lax.broadcasted_iota(jnp.int32, (K, TM), 0).astype(F32)

    def head(h, carry):
        off = pl.multiple_of(h * 2 * PEER_HALF, 2 * PEER_HALF)
        s1 = _dot_nt(keys_ref[h, 0], q_scr[:, pl.ds(off, PEER_HALF)])
        s2 = _dot_nt(keys_ref[h, 1], q_scr[:, pl.ds(off + PEER_HALF, PEER_HALF)])
        n_max = jnp.zeros((1, TM), F32)
        for half, s in ((0, s1), (1, s2)):
            rank, vals, n_ranked = _top16_untied(s)
            rk_scr[half] = rank
            vl_scr[half] = jnp.concatenate(vals, axis=0)
            n_max = jnp.maximum(n_max, n_ranked)

        @pl.when(jnp.max(n_max) > float(K))
        def _():
            for half, s in ((0, s1), (1, s2)):
                rank, vals = _top16(s, rowid)
                rk_scr[half] = rank
                vl_scr[half] = jnp.concatenate(vals, axis=0)

        rank1 = rk_scr[0]
        rank2 = rk_scr[1]
        v1 = vl_scr[0]
        vals1 = [v1[r:r + 1] for r in range(K)]
        vals2 = [vl_scr[1, r:r + 1, :] for r in range(K)]
        taken = jnp.zeros((K, TM), F32)
        front = v1 + vals2[0]
        for _ in range(K):
            m = jnp.max(front, axis=0, keepdims=True)
            idx = jnp.min(jnp.where(front == m, kaid, 1e9), axis=0, keepdims=True)
            sel = kaid == idx
            taken = taken + jnp.where(sel, 1.0, 0.0)
            nxt = jnp.full((K, TM), NEG_INF, F32)
            for kb in range(1, K):
                nxt = jnp.where(taken == float(kb), vals2[kb], nxt)
            front = jnp.where(sel, v1 + nxt, front)
        e1k = jnp.exp(v1 - vals1[0])
        pref = jnp.zeros((1, TM), F32)
        zrow = jnp.zeros((K, TM), F32)
        for kb in range(K):
            pref = pref + jnp.exp(vals2[kb] - vals2[0])
            zrow = jnp.where(taken == float(kb + 1), pref, zrow)
        z = jnp.sum(e1k * zrow, axis=0, keepdims=True)
        na = jnp.zeros((PEER_NKEYS, TM), F32)
        for ka in range(K):
            na = jnp.where(rank1 == float(ka), taken[ka:ka + 1], na)
        r2_ref[h] = _bf(rank2)
        na_ref[h] = na
        e1_ref[h] = jnp.where(rank1 < float(K), jnp.exp(s1 - vals1[0]) / z, 0.0)
        e2_ref[h] = _bf(jnp.where(rank2 < float(K), jnp.exp(s2 - vals2[0]), 0.0))
        return carry

    lax.fori_loop(0, PEER_HEADS, head, 0)


def _peer_prep(hn, wq, keys, TM):
    T, D = hn.shape
    Q = wq.shape[1]
    shp = jax.ShapeDtypeStruct((PEER_HEADS, PEER_NKEYS, T), F32)
    shp_bf = jax.ShapeDtypeStruct((PEER_HEADS, PEER_NKEYS, T), BF16)
    ospec = pl.BlockSpec((PEER_HEADS, PEER_NKEYS, TM), lambda i: (0, 0, i))
    return pl.pallas_call(
        _peer_prep_kernel,
        grid=(T // TM,),
        in_specs=[pl.BlockSpec((TM, D), lambda i: (i, 0)),
                  pl.BlockSpec((D, Q), lambda i: (0, 0)),
                  pl.BlockSpec((PEER_HEADS, 2, PEER_NKEYS, PEER_HALF), lambda i: (0, 0, 0, 0))],
        out_specs=[ospec, ospec, ospec, ospec],
        out_shape=[shp_bf, shp, shp, shp_bf],
        scratch_shapes=[pltpu.VMEM((TM, Q), BF16),
                        pltpu.VMEM((2, PEER_NKEYS, TM), F32), pltpu.VMEM((2, PEER_TOPK, TM), F32)],
        compiler_params=_cp("parallel"),
        name="peer_prep",
    )(hn, wq, keys)


def _peer_dense_kernel(h_ref, r2_ref, na_ref, e1_ref, e2_ref, u_ref, vt_ref, x_ref, gt_ref, fg_ref,
                       o_ref, acc_t, ht_scr, s0_scr, s1_scr, aw0_scr, aw1_scr, *, block, n_blocks, final_norm):
    jj = pl.program_id(1)
    TM = h_ref.shape[0]
    SLAB = PEER_NKEYS
    EB = block

    @pl.when(jj == 0)
    def _():
        acc_t[...] = jnp.zeros_like(acc_t)
        s1_scr[...] = jnp.zeros_like(s1_scr)
        aw0_scr[...] = jnp.zeros_like(aw0_scr)
        ht_scr[...] = _bf(h_ref[...].astype(F32).T)

    zero_bf = jnp.zeros((SLAB, TM), BF16)

    def row_tile(ref, h, a):
        t = _bf(jnp.broadcast_to(ref[h, pl.ds(a, 1), :], (16, TM)))
        return jnp.concatenate([t] * (SLAB // 16), axis=0)

    def vpu_stage(k, s_ref, aw_ref):
        live = jnp.logical_and(k >= 0, k < n_blocks)
        kc = jnp.clip(k, 0, n_blocks - 1)
        for p in range(EB // (2 * SLAB)):
            lo = p * 2 * SLAB
            s = s_ref[lo:lo + 2 * SLAB, :]
            act = 0.5 * s * (1.0 + lax.erf(s * SQRT_HALF))
            weights = []
            for half in range(2):
                a = kc * (EB // SLAB) + p * 2 + half
                w = jnp.zeros((SLAB, TM), BF16)
                for h in range(PEER_HEADS):
                    chosen = r2_ref[h] < row_tile(na_ref, h, a)
                    w = w + jnp.where(chosen, e2_ref[h], zero_bf) * row_tile(e1_ref, h, a)
                weights.append(w)
            aw = _bf(act) * jnp.concatenate(weights, axis=0)
            aw_ref[lo:lo + 2 * SLAB, :] = jnp.where(live, aw, jnp.zeros_like(aw))

    out0 = _dot(vt_ref[:, 0:EB], aw0_scr[...])
    vpu_stage(2 * jj - 1, s1_scr, aw1_scr)
    s0_scr[...] = _dot(u_ref[0:EB, :], ht_scr[...])
    out1 = _dot(vt_ref[:, EB:2 * EB], aw1_scr[...])
    vpu_stage(2 * jj, s0_scr, aw0_scr)
    s1_scr[...] = _dot(u_ref[EB:2 * EB, :], ht_scr[...])
    acc_t[...] += out0 + out1

    @pl.when(jj == pl.num_programs(1) - 1)
    def _():
        xn = x_ref[...] + gt_ref[0] * acc_t[...].T
        o_ref[...] = _rms(xn, fg_ref[...]) if final_norm else xn


def _peer_dense(x, hn, sel, u, vt, gt2, TM, tiles_per_batch, final_g):
    T, D = x.shape
    E = u.shape[0]
    EB = 1024
    final_norm = final_g is not None
    fg = (final_g if final_norm else jnp.ones((D,), F32)).reshape(1, D)
    sspec = pl.BlockSpec((PEER_HEADS, PEER_NKEYS, TM), lambda i, e: (0, 0, i))
    nE = E // EB
    nP = nE // 2
    kern = functools.partial(_peer_dense_kernel, block=EB, n_blocks=nE, final_norm=final_norm)
    return pl.pallas_call(
        kern,
        grid=(T // TM, nP + 1),
        in_specs=[pl.BlockSpec((TM, D), lambda i, e: (i, 0)),
                  sspec, sspec, sspec, sspec,
                  pl.BlockSpec((2 * EB, D), lambda i, e: (jnp.minimum(e, nP - 1), 0)),
                  pl.BlockSpec((D, 2 * EB), lambda i, e: (0, jnp.maximum(e - 1, 0))),
                  pl.BlockSpec((TM, D), lambda i, e: (i, 0)),
                  pl.BlockSpec((1, 1, D), lambda i, e: (i // tiles_per_batch, 0, 0)),
                  pl.BlockSpec((1, D), lambda i, e: (0, 0))],
        out_specs=pl.BlockSpec((TM, D), lambda i, e: (i, 0)),
        out_shape=jax.ShapeDtypeStruct((T, D), F32),
        scratch_shapes=[pltpu.VMEM((D, TM), F32), pltpu.VMEM((D, TM), BF16),
                        pltpu.VMEM((EB, TM), F32), pltpu.VMEM((EB, TM), F32),
                        pltpu.VMEM((EB, TM), BF16), pltpu.VMEM((EB, TM), BF16)],
        compiler_params=_cp("parallel", "arbitrary"),
        name="peer_dense",
    )(hn, *sel, u, vt, x, gt2, fg)


def _direction_masks():
    t = jnp.arange(CHUNK)
    le = (t[None, :] <= t[:, None]).astype(F32)
    tri = jnp.stack([le, le.T])
    i = jnp.arange(GROUP_W)
    same = (i[:, None] // CHUNK) == (i[None, :] // CHUNK)
    ti, tj = i[:, None] % CHUNK, i[None, :] % CHUNK
    msl = jnp.stack([same & (tj < ti), same & (tj > ti)]).astype(F32)
    minc = jnp.stack([same & (tj <= ti), same & (tj >= ti)]).astype(F32)
    return tri, msl, minc


def _layer_params(l, w_in, shift_mu, w0, w2, a0, a2, g2, k_k, k_a, r_k, lnx_g, lnx_b, w_oA, conv_w,
                  cnorm_g, cnorm_b, w_oB, gate_b, w_out, norm2_g):
    D = D_MODEL
    row = lambda t: t.reshape(1, -1)
    zeros = jnp.zeros((LORA_W, D), F32)
    tri, msl, minc = _direction_masks()
    hs = (jnp.arange(D)[:, None] // HEAD == jnp.arange(128)[None, :]).astype(F32)
    pad_cols = P_RWKV_PAD - P_RWKV
    return dict(
        w_rk=_bf(jnp.pad(w_in[l][:, :P_RWKV], ((0, 0), (0, pad_cols)))),
        w_cg=_bf(w_in[l][:, P_RWKV:]),
        mu=jnp.pad(shift_mu[l], ((0, 0), (0, pad_cols))),
        w0=w0[l].reshape(2, 1, D), a0=a0[l].reshape(2, 1, D),
        w2p=_split(jnp.stack([jnp.concatenate([w2[l, 0], zeros]), jnp.concatenate([zeros, w2[l, 1]])])),
        a2p=_split(jnp.stack([jnp.concatenate([a2[l, 0], zeros]), jnp.concatenate([zeros, a2[l, 1]])])),
        g2p=_bf(jnp.pad(g2[l], ((0, LORA_G_PAD - LORA_G), (0, 0)))),
        k_k=row(k_k[l]), k_a=row(k_a[l]), r_k=row(r_k[l]),
        lnx_g=row(lnx_g[l]), lnx_b=row(lnx_b[l]), w_oA=_bf(w_oA[l]),
        conv_w=conv_w[l], cnorm_g=row(cnorm_g[l]), cnorm_b=row(cnorm_b[l]), w_oB=_bf(w_oB[l]),
        gate_b=row(gate_b[l]), w_out=_bf(w_out[l]), norm2_g=row(norm2_g[l]),
        hs=_bf(hs), hb=_bf(hs.T), tri=_bf(tri), msl=msl, minc=minc)


def _mixer(x, mod, pr, norm1_g, stride, h0, emit):
    B, L, D = x.shape
    sh1, sc1, gt1, sh2, sc2 = (mod[:, i:i + 1, :] for i in range(5))
    z_rk = _proj(x, sh1, sc1, norm1_g, pr["w_rk"])
    y, bv, gs, h_t = _rwkv(z_rk, pr, h0)
    if not emit:
        return None, None, h_t
    z_cg = _proj(x, sh1, sc1, norm1_g, pr["w_cg"])
    cv = _conv(z_cg, pr["conv_w"], stride)
    xn, hn = _post(x, y, bv, gs, cv, z_cg, pr, gt1, sh2, sc2)
    return xn, hn, h_t


def _peer(x, hn, wq, keys, u, v, gt2, final_g=None):
    B, L, D = x.shape
    T = B * L
    TM = min(L, 256)
    sel = _peer_prep(hn.reshape(T, D), wq, keys, TM)
    out = _peer_dense(x.reshape(T, D), hn.reshape(T, D), sel, u, v, gt2, TM, L // TM, final_g)
    return out.reshape(B, L, D)


def kernel(x, c, ctx, c_ctx, ada_w, ada_b, norm1_g, norm2_g, w_in, shift_mu, w0, w2, a0, a2, g2, k_k, k_a, r_k, lnx_g, lnx_b, w_oA, conv_w, cnorm_g, cnorm_b, w_oB, gate_b, w_out, w_q, sub_keys, peer_u, peer_v, final_g):
    B, L, D = x.shape
    depth = ada_w.shape[0]
    xc = ctx
    n_rows = -(-(B + 1) // 8) * 8
    c_rows = jnp.pad(jnp.concatenate([c, c_ctx[None, :]], axis=0), ((0, n_rows - B - 1), (0, 0)))
    zero_state = jnp.zeros((2, B, N_GROUPS, GROUP_W, GROUP_W), F32)
    for l in range(depth):
        last = l == depth - 1
        pr = _layer_params(l, w_in, shift_mu, w0, w2, a0, a2, g2, k_k, k_a, r_k, lnx_g, lnx_b, w_oA,
                           conv_w, cnorm_g, cnorm_b, w_oB, gate_b, w_out, norm2_g)
        mod_all = _modulation(c_rows, ada_w[l], ada_b[l])
        mod = mod_all[:B].reshape(B, 6, D)
        modc = jnp.broadcast_to(mod_all[B].reshape(1, 6, D), (B, 6, D))
        wq = _bf(w_q[l])
        keys = _bf(sub_keys[l])
        u = _bf(peer_u[l])
        v = _bf(peer_v[l]).T

        xc_new, hnc, ctx_states = _mixer(xc, modc, pr, norm1_g[l], 1, zero_state, emit=not last)
        xn, hn, _ = _mixer(x, mod, pr, norm1_g[l], GRID_W, ctx_states, emit=True)
        x = _peer(xn, hn, wq, keys, u, v, mod[:, 5:6, :], final_g if last else None)
        if not last:
            xc = _peer(xc_new, hnc, wq, keys, u, v, modc[:, 5:6, :])
    return x
```

```python
import functools

import jax
import jax.numpy as jnp
from jax import lax
from jax.experimental import pallas as pl
from jax.experimental.pallas import tpu as pltpu

F32 = jnp.float32
BF16 = jnp.bfloat16
HI = lax.Precision.HIGHEST

D_MODEL = 1024
HEAD = 64
HEADS = D_MODEL // HEAD
GROUP_HEADS = 4
GROUP_W = GROUP_HEADS * HEAD
N_GROUPS = HEADS // GROUP_HEADS
CHUNK = 64
RWKV_ROWS_PER_STEP = 2
LORA_W = 64
LORA_A = 64
LORA_G = 160
LORA_G_PAD = 256
P_RWKV = 3 * D_MODEL + 2 * LORA_W + 2 * LORA_A + LORA_G
P_RWKV_PAD = 3 * D_MODEL + 2 * LORA_W + 2 * LORA_A + LORA_G_PAD
COL_W1 = 3 * D_MODEL
COL_A1 = COL_W1 + 2 * LORA_W
COL_G1 = COL_A1 + 2 * LORA_A
CONV_K = 31
CONV_HALF = CONV_K // 2
GRID_W = 64
PEER_HEADS = 8
PEER_NKEYS = 128
PEER_HALF = 128
PEER_TOPK = 16
NORM_EPS = 1e-6
LN_EPS = 1e-5
GN_EPS = HEAD * 1e-5
VMEM_LIMIT = 56 * 1024 * 1024
NOT_SELECTED = 99.0
NEG_INF = float("-inf")
SQRT_HALF = 0.7071067811865476


def _cp(*sem):
    return pltpu.CompilerParams(dimension_semantics=sem, vmem_limit_bytes=VMEM_LIMIT)


def _dot(a, b):
    return jnp.dot(a, b, preferred_element_type=F32)


def _dot_hi(a, b):
    return jnp.dot(a, b, precision=HI, preferred_element_type=F32)


def _dot_nt(a, b):
    return lax.dot_general(a, b, (((1,), (1,)), ((), ())), preferred_element_type=F32)


def _dot_tn(a, b):
    return lax.dot_general(a, b, (((0,), (0,)), ((), ())), preferred_element_type=F32)


def _bf(a):
    return a.astype(BF16)


def _split(a):
    hi = a.astype(BF16)
    return hi, (a - hi.astype(F32)).astype(BF16)


def _dot_split_lhs(a, b_bf):
    hi, lo = _split(a)
    return _dot(hi, b_bf) + _dot(lo, b_bf)


def _dot_split(a, b_hi, b_lo):
    hi, lo = _split(a)
    return _dot(hi, b_hi) + _dot(hi, b_lo) + _dot(lo, b_hi)


def _head_sum(t, hs_bf, hb_bf):
    return _dot_split_lhs(_dot_split_lhs(t, hs_bf), hb_bf)


def _sigmoid(x):
    return 1.0 / (1.0 + jnp.exp(-x))


def _softplus(x):
    return jnp.maximum(x, 0.0) + jnp.log(1.0 + jnp.exp(-jnp.abs(x)))


def _rms(x, g):
    return x * lax.rsqrt(jnp.mean(x * x, axis=-1, keepdims=True) + NORM_EPS) * g


def _mod_kernel(c_ref, w_ref, b_ref, o_ref):
    c = c_ref[...]
    o_ref[...] = _dot_hi(c * _sigmoid(c), w_ref[...]) + b_ref[...]


def _modulation(c_rows, ada_w, ada_b):
    R, D = c_rows.shape
    N = ada_w.shape[1]
    TN = 512
    return pl.pallas_call(
        _mod_kernel,
        grid=(N // TN,),
        in_specs=[pl.BlockSpec((R, D), lambda j: (0, 0)),
                  pl.BlockSpec((D, TN), lambda j: (0, j)),
                  pl.BlockSpec((1, TN), lambda j: (0, j))],
        out_specs=pl.BlockSpec((R, TN), lambda j: (0, j)),
        out_shape=jax.ShapeDtypeStruct((R, N), F32),
        compiler_params=_cp("arbitrary"),
        name="modulation",
    )(c_rows, ada_w, ada_b.reshape(1, N))


def _proj_kernel(x_ref, sh_ref, sc_ref, g_ref, w_ref, o_ref):
    h = _rms(x_ref[0], g_ref[...]) * (1.0 + sc_ref[0]) + sh_ref[0]
    o_ref[0] = _dot(_bf(h), w_ref[...])


def _proj(x, sh, sc, g, w):
    B, L, D = x.shape
    N = w.shape[1]
    TM = min(L, 256)
    return pl.pallas_call(
        _proj_kernel,
        grid=(B, L // TM),
        in_specs=[pl.BlockSpec((1, TM, D), lambda b, i: (b, i, 0)),
                  pl.BlockSpec((1, 1, D), lambda b, i: (b, 0, 0)),
                  pl.BlockSpec((1, 1, D), lambda b, i: (b, 0, 0)),
                  pl.BlockSpec((1, D), lambda b, i: (0, 0)),
                  pl.BlockSpec((D, N), lambda b, i: (0, 0))],
        out_specs=pl.BlockSpec((1, TM, N), lambda b, i: (b, i, 0)),
        out_shape=jax.ShapeDtypeStruct((B, L, N), F32),
        compiler_params=_cp("parallel", "parallel"),
        name="proj",
    )(x, sh, sc, g.reshape(1, D), w)


_KAP, _RT, _KT, _BT, _KH, _BH, _V = range(7)


def _rwkv_kernel(z_ref, zp_ref, zn_ref, mu_ref, w0_ref, w2h_ref, w2l_ref, a0_ref, a2h_ref, a2l_ref,
                 kk_ref, ka_ref, rk_ref, hs_ref, hb_ref, tri_ref, msl_ref, minc_ref, h0_ref,
                 y_ref, bv_ref, gs_ref, hT_ref, H_scr, nat_scr, pc_scr, *, n_chunks):
    d = pl.program_id(0)
    c = pl.program_id(2)
    cc = jnp.where(d == 0, c, n_chunks - 1 - c)
    C = CHUNK

    R = z_ref.shape[0]

    @pl.when(c == 0)
    def _():
        H_scr[...] = h0_ref[0]

    first = cc == 0
    last = cc == n_chunks - 1
    row = lax.broadcasted_iota(jnp.int32, (C, 1), 0)
    hs = hs_ref[...]
    hb = hb_ref[...]

    def head_sum(t):
        return _head_sum(t, hs, hb)

    def prepare(rr):
        def shifted(lo, hi):
            z = z_ref[rr, :, lo:hi]
            prev_row = jnp.where(first, 0.0, zp_ref[rr, 7:8, lo:hi])
            next_row = jnp.where(last, 0.0, zn_ref[rr, 0:1, lo:hi])
            zp = jnp.where(row == 0, prev_row, pltpu.roll(z, 1, 0))
            zn = jnp.where(row == C - 1, next_row, pltpu.roll(z, C - 1, 0))
            return z + mu_ref[0:1, lo:hi] * (zp - z) + mu_ref[1:2, lo:hi] * (zn - z)

        r = shifted(0, D_MODEL)
        k = shifted(D_MODEL, 2 * D_MODEL)
        v = shifted(2 * D_MODEL, 3 * D_MODEL)
        w1 = shifted(COL_W1, COL_A1)
        a1 = shifted(COL_A1, COL_G1)
        g1 = shifted(COL_G1, P_RWKV_PAD)

        wl = w0_ref[0] + _dot_split(jnp.tanh(w1), w2h_ref[0], w2l_ref[0])
        logw = -jnp.exp(-_softplus(-wl) - 0.5)
        a = _sigmoid(_dot_split(a1, a2h_ref[0], a2l_ref[0]) + a0_ref[0])
        kkr = k * kk_ref[...]
        kk = kkr / jnp.maximum(jnp.sqrt(head_sum(kkr * kkr)), 1e-12)
        kd = k * (1.0 + (a - 1.0) * ka_ref[...])
        bb = kk * a
        bv_ref[0, rr] = head_sum(r * kd * rk_ref[...]) * v
        gs_ref[0, rr] = _sigmoid(g1)

        lw_hi, lw_lo = _split(logw)
        g_in = _dot(tri_ref[0], lw_hi) + _dot(tri_ref[0], lw_lo)
        g_ex = g_in - logw
        g_c = jnp.sum(logw, axis=0, keepdims=True)
        e_inv = jnp.exp(-g_in)
        e_hat = jnp.exp(g_c - g_in)
        nat_scr[rr, _KAP] = kk * jnp.exp(g_ex)
        nat_scr[rr, _RT] = r * jnp.exp(g_in)
        nat_scr[rr, _KT] = kd * e_inv
        nat_scr[rr, _BT] = bb * e_inv
        nat_scr[rr, _KH] = kd * e_hat
        nat_scr[rr, _BH] = bb * e_hat
        nat_scr[rr, _V] = v
        pc_scr[rr] = jnp.exp(g_c)

    for rr in range(R):
        prepare(rr)

    lane_head = lax.broadcasted_iota(jnp.int32, (C, GROUP_W), 1) // HEAD
    ii = lax.broadcasted_iota(jnp.int32, (GROUP_W, GROUP_W), 0)
    jj = lax.broadcasted_iota(jnp.int32, (GROUP_W, GROUP_W), 1)
    eye = ii == jj
    msl = msl_ref[0] > 0.5
    minc = minc_ref[0] > 0.5

    def stacked(t):
        return jnp.concatenate([jnp.where(lane_head == j, t, 0.0) for j in range(GROUP_HEADS)], axis=0)

    def collapse(t):
        return t[0:C] + t[C:2 * C] + t[2 * C:3 * C] + t[3 * C:4 * C]

    G = range(R * N_GROUPS)

    def lanes(q):
        return slice((q % N_GROUPS) * GROUP_W, (q % N_GROUPS + 1) * GROUP_W)

    def nat(i, q):
        return nat_scr[q // N_GROUPS, i, :, lanes(q)]

    x_kap = [stacked(nat(_KAP, g)) for g in G]
    x_v = [_bf(stacked(nat(_V, g))) for g in G]
    x_bk = [jnp.concatenate([_bf(stacked(nat(_BT, g))), _bf(stacked(nat(_KT, g)))], axis=0) for g in G]
    kr = [_bf(jnp.concatenate([nat(_KAP, g), nat(_RT, g)], axis=0)) for g in G]
    akr = [_dot_nt(kr[g], x_bk[g]) for g in G]

    def tiled(t):
        return jnp.concatenate([t] * GROUP_HEADS, axis=0)

    n_pow = [jnp.where(msl, tiled(akr[g][:C, :GROUP_W]), 0.0) for g in G]
    msl_c = collapse(msl_ref[0]) > 0.5
    minc_c = collapse(minc_ref[0]) > 0.5
    n_side = [jnp.where(msl_c, akr[g][:C, :GROUP_W], 0.0) for g in G]
    a_kk = [_bf(jnp.where(msl_c, akr[g][:C, GROUP_W:], 0.0)) for g in G]
    a_rb = [_bf(jnp.where(minc_c, akr[g][C:, :GROUP_W], 0.0)) for g in G]
    a_rk = [_bf(jnp.where(minc_c, akr[g][C:, GROUP_W:], 0.0)) for g in G]
    g0 = [stacked(_dot(a_kk[g], x_v[g])) for g in G]
    y0 = [_dot(a_rk[g], x_v[g]) for g in G]
    eye_side = collapse(jnp.where(eye, 1.0, 0.0))
    p_inv = [eye_side - n_side[g] for g in G]
    for _ in range(5):
        nb = [_bf(n_pow[g]) for g in G]
        n_pow = [_dot(nb[g], nb[g]) for g in G]
        p_inv = [p_inv[g] + _dot(_bf(p_inv[g]), _bf(n_pow[g])) for g in G]
    w_nat = [_dot(_bf(p_inv[g]), jnp.concatenate([_bf(x_kap[g]), _bf(g0[g])], axis=1)) for g in G]
    w12 = [_bf(jnp.concatenate([stacked(w_nat[g][:, :GROUP_W]), stacked(w_nat[g][:, GROUP_W:])], axis=1))
           for g in G]
    aw = [_dot(a_rb[g], w12[g]) for g in G]
    bw = [_dot_tn(_bf(stacked(nat(_BH, g))), w12[g]) for g in G]
    kv = [_dot_tn(_bf(stacked(nat(_KH, g))), x_v[g]) for g in G]
    h_old = [_split(H_scr[g // N_GROUPS, g % N_GROUPS]) for g in G]
    ys = []
    h_new = []
    for g in G:
        h_hi, h_lo = h_old[g]
        qb = _bf(nat(_RT, g) - aw[g][:, :GROUP_W])
        ys.append(_dot(qb, h_hi) + _dot(qb, h_lo) + y0[g] - aw[g][:, GROUP_W:])
        m_mat = jnp.where(eye, pc_scr[g // N_GROUPS, :, lanes(g)], 0.0) - bw[g][:, :GROUP_W]
        m_hi, m_lo = _split(m_mat)
        h_new.append(_dot(m_hi, h_hi) + _dot(m_hi, h_lo) + _dot(m_lo, h_hi) + kv[g] - bw[g][:, GROUP_W:])
    for rr in range(R):
        y_ref[0, rr] = jnp.concatenate(ys[rr * N_GROUPS:(rr + 1) * N_GROUPS], axis=1)
    for g in G:
        H_scr[g // N_GROUPS, g % N_GROUPS] = h_new[g]

    @pl.when(c == n_chunks - 1)
    def _():
        hT_ref[0] = H_scr[...]


def _rwkv(z, pr, h0):
    B, L, N = z.shape
    C = CHUNK
    nC = L // C
    D = D_MODEL
    nb8 = L // 8

    def cidx(d, c):
        return c + d * (nC - 1 - 2 * c)

    zmap = lambda d, b, c: (b, cidx(d, c), 0)
    pmap = lambda d, b, c: (b, jnp.maximum(cidx(d, c) * (C // 8) - 1, 0), 0)
    nmap = lambda d, b, c: (b, jnp.minimum((cidx(d, c) + 1) * (C // 8), nb8 - 1), 0)
    const2 = lambda d, b, c: (0, 0)
    dir3 = lambda d, b, c: (d, 0, 0)
    omap = lambda d, b, c: (d, b, cidx(d, c), 0)
    smap = lambda d, b, c: (d, b, 0, 0, 0)
    R = RWKV_ROWS_PER_STEP if B % RWKV_ROWS_PER_STEP == 0 else 1
    kern = functools.partial(_rwkv_kernel, n_chunks=nC)
    return pl.pallas_call(
        kern,
        grid=(2, B // R, nC),
        in_specs=[pl.BlockSpec((R, C, N), zmap),
                  pl.BlockSpec((R, 8, N), pmap),
                  pl.BlockSpec((R, 8, N), nmap),
                  pl.BlockSpec((2, N), const2),
                  pl.BlockSpec((1, 1, D), dir3),
                  pl.BlockSpec((1, 2 * LORA_W, D), dir3),
                  pl.BlockSpec((1, 2 * LORA_W, D), dir3),
                  pl.BlockSpec((1, 1, D), dir3),
                  pl.BlockSpec((1, 2 * LORA_A, D), dir3),
                  pl.BlockSpec((1, 2 * LORA_A, D), dir3),
                  pl.BlockSpec((1, D), const2),
                  pl.BlockSpec((1, D), const2),
                  pl.BlockSpec((1, D), const2),
                  pl.BlockSpec((D, 128), const2),
                  pl.BlockSpec((128, D), const2),
                  pl.BlockSpec((1, C, C), dir3),
                  pl.BlockSpec((1, GROUP_W, GROUP_W), dir3),
                  pl.BlockSpec((1, GROUP_W, GROUP_W), dir3),
                  pl.BlockSpec((1, R, N_GROUPS, GROUP_W, GROUP_W), smap)],
        out_specs=[pl.BlockSpec((1, R, C, D), omap),
                   pl.BlockSpec((1, R, C, D), omap),
                   pl.BlockSpec((1, R, C, LORA_G_PAD), omap),
                   pl.BlockSpec((1, R, N_GROUPS, GROUP_W, GROUP_W), smap)],
        out_shape=[jax.ShapeDtypeStruct((2, B, L, D), F32),
                   jax.ShapeDtypeStruct((2, B, L, D), F32),
                   jax.ShapeDtypeStruct((2, B, L, LORA_G_PAD), F32),
                   jax.ShapeDtypeStruct((2, B, N_GROUPS, GROUP_W, GROUP_W), F32)],
        scratch_shapes=[pltpu.VMEM((R, N_GROUPS, GROUP_W, GROUP_W), F32),
                        pltpu.VMEM((R, 7, C, D), F32),
                        pltpu.VMEM((R, 1, D), F32)],
        compiler_params=_cp("arbitrary", "arbitrary", "arbitrary"),
        name="rwkv",
    )(z, z, z, pr["mu"], pr["w0"], *pr["w2p"], pr["a0"], *pr["a2p"], pr["k_k"], pr["k_a"], pr["r_k"],
      pr["hs"], pr["hb"], pr["tri"], pr["msl"], pr["minc"], h0)


def _conv_kernel(za_ref, zb_ref, w_ref, o_ref, upad, *, L, stride, rows_per_step):
    pad = CONV_HALF * stride
    TC = za_ref.shape[-1]
    upad[0:pad, :] = jnp.zeros((pad, TC), F32)
    upad[pad + L:pad + L + pad, :] = jnp.zeros((pad, TC), F32)
    upad[pad:pad + L, :] = za_ref[0] * _sigmoid(zb_ref[0])
    RB = rows_per_step

    def block(r0):
        acc = jnp.zeros((RB, TC), F32)
        for j in range(CONV_K):
            acc = acc + w_ref[j:j + 1, :] * upad[pl.ds(r0 + j * stride, RB), :]
        o_ref[0, pl.ds(r0, RB), :] = acc

    if stride % 8 == 0:
        def body(i, carry):
            block(pl.multiple_of(i * RB, RB))
            return carry
        lax.fori_loop(0, L // RB, body, 0)
    else:
        for i in range(L // RB):
            block(i * RB)


def _conv(zcg, conv_w, stride):
    B, L, _ = zcg.shape
    D = D_MODEL
    TC = 128
    nct = D // TC
    wpad = jnp.pad(conv_w, ((0, 32 - CONV_K), (0, 0)))
    RB = min(L, 128)
    kern = functools.partial(_conv_kernel, L=L, stride=stride, rows_per_step=RB)
    return pl.pallas_call(
        kern,
        grid=(B, nct),
        in_specs=[pl.BlockSpec((1, L, TC), lambda b, j: (b, 0, j)),
                  pl.BlockSpec((1, L, TC), lambda b, j: (b, 0, j + nct)),
                  pl.BlockSpec((32, TC), lambda b, j: (0, j))],
        out_specs=pl.BlockSpec((1, L, TC), lambda b, j: (b, 0, j)),
        out_shape=jax.ShapeDtypeStruct((B, L, D), F32),
        scratch_shapes=[pltpu.VMEM((L + 2 * CONV_HALF * stride, TC), F32)],
        compiler_params=_cp("parallel", "parallel"),
        name="conv",
    )(zcg, zcg, wpad)


def _post_kernel(x_ref, yf_ref, yb_ref, bf_ref, bb_ref, gs_ref, cv_ref, zg_ref,
                 lng_ref, lnb_ref, g2_ref, woa_ref, cng_ref, cnb_ref, wob_ref, gb_ref, wout_ref,
                 gt_ref, n2g_ref, sh2_ref, sc2_ref, hs_ref, hb_ref, xo_ref, hn_ref):
    D = D_MODEL
    hs = hs_ref[...]
    hb = hb_ref[...]

    def head_mean(t):
        return _head_sum(t, hs, hb) * (1.0 / HEAD)

    o = yf_ref[0, 0] + yb_ref[0, 0]
    oc = o - head_mean(o)
    on = oc * lax.rsqrt(head_mean(oc * oc) + GN_EPS) * lng_ref[...] + lnb_ref[...]
    on = on + bf_ref[0, 0] + bb_ref[0, 0]
    gate = _dot(_bf(gs_ref[0, 0]), g2_ref[...])
    y_a = _dot(_bf(on * gate), woa_ref[...])

    cv = cv_ref[0]
    cm = jnp.mean(cv, axis=-1, keepdims=True)
    cc = cv - cm
    cn = cc * lax.rsqrt(jnp.mean(cc * cc, axis=-1, keepdims=True) + LN_EPS) * cng_ref[...] + cnb_ref[...]
    y_b = _dot(_bf(cn * _sigmoid(cn)), wob_ref[...])

    gates = _sigmoid(zg_ref[0] + gb_ref[...])
    m = gates[:, :D] * y_a + gates[:, D:] * y_b
    xn = x_ref[0] + gt_ref[0] * _dot(_bf(m), wout_ref[...])
    xo_ref[0] = xn
    hn_ref[0] = _bf(_rms(xn, n2g_ref[...]) * (1.0 + sc2_ref[0]) + sh2_ref[0])


def _post(x, y, bv, gs, cv, zcg, pr, gt1, sh2, sc2):
    B, L, D = x.shape
    TM = min(L, 256)
    tok = lambda b, i: (b, i, 0)
    fwd = lambda b, i: (0, b, i, 0)
    bwd = lambda b, i: (1, b, i, 0)
    cst = lambda b, i: (0, 0)
    per_b = lambda b, i: (b, 0, 0)
    row = pl.BlockSpec((1, D), cst)
    mat = pl.BlockSpec((D, D), cst)
    return pl.pallas_call(
        _post_kernel,
        grid=(B, L // TM),
        in_specs=[pl.BlockSpec((1, TM, D), tok),
                  pl.BlockSpec((1, 1, TM, D), fwd), pl.BlockSpec((1, 1, TM, D), bwd),
                  pl.BlockSpec((1, 1, TM, D), fwd), pl.BlockSpec((1, 1, TM, D), bwd),
                  pl.BlockSpec((1, 1, TM, LORA_G_PAD), fwd),
                  pl.BlockSpec((1, TM, D), tok),
                  pl.BlockSpec((1, TM, 2 * D), lambda b, i: (b, i, 1)),
                  row, row, pl.BlockSpec((LORA_G_PAD, D), cst), mat,
                  row, row, mat, pl.BlockSpec((1, 2 * D), cst), mat,
                  pl.BlockSpec((1, 1, D), per_b), row,
                  pl.BlockSpec((1, 1, D), per_b), pl.BlockSpec((1, 1, D), per_b),
                  pl.BlockSpec((D, 128), cst), pl.BlockSpec((128, D), cst)],
        out_specs=[pl.BlockSpec((1, TM, D), tok), pl.BlockSpec((1, TM, D), tok)],
        out_shape=[jax.ShapeDtypeStruct((B, L, D), F32), jax.ShapeDtypeStruct((B, L, D), BF16)],
        compiler_params=_cp("parallel", "parallel"),
        name="post",
    )(x, y, y, bv, bv, gs, cv, zcg,
      pr["lnx_g"], pr["lnx_b"], pr["g2p"], pr["w_oA"], pr["cnorm_g"], pr["cnorm_b"], pr["w_oB"],
      pr["gate_b"], pr["w_out"], gt1, pr["norm2_g"], sh2, sc2, pr["hs"], pr["hb"])


def _top16(s, rowid):
    rank = jnp.full(s.shape, NOT_SELECTED, F32)
    cur = s
    vals = []
    for r in range(PEER_TOPK):
        m = jnp.max(cur, axis=0, keepdims=True)
        idx = jnp.min(jnp.where(cur == m, rowid, 1e9), axis=0, keepdims=True)
        sel = rowid == idx
        rank = jnp.where(sel, float(r), rank)
        cur = jnp.where(sel, NEG_INF, cur)
        vals.append(m)
    return rank, vals


def _top16_untied(arrays):
    n = range(len(arrays))
    rank = [jnp.full(s.shape, NOT_SELECTED, F32) for s in arrays]
    cur = list(arrays)
    vals = [[] for _ in n]
    for r in range(PEER_TOPK):
        m = [jnp.max(cur[i], axis=0, keepdims=True) for i in n]
        sel = [cur[i] == m[i] for i in n]
        rank = [jnp.where(sel[i], float(r), rank[i]) for i in n]
        cur = [jnp.where(sel[i], NEG_INF, cur[i]) for i in n]
        for i in n:
            vals[i].append(m[i])
    n_ranked = [jnp.sum(jnp.where(rank[i] < float(PEER_TOPK), 1.0, 0.0), axis=0, keepdims=True) for i in n]
    return rank, vals, n_ranked


def _peer_prep_kernel(h_ref, wq_ref, keys_ref, r2_ref, na_ref, e1_ref, e2_ref, q_scr, rk_scr, vl_scr):
    TM = h_ref.shape[0]
    K = PEER_TOPK
    q_scr[...] = _bf(_dot(h_ref[...], wq_ref[...]))
    rowid = lax.broadcasted_iota(jnp.int32, (PEER_NKEYS, TM), 0).astype(F32)
    kaid = lax.broadcasted_iota(jnp.int32, (K, TM), 0).astype(F32)

    def head(h, carry):
        off = pl.multiple_of(h * 2 * PEER_HALF, 2 * PEER_HALF)
        s1 = _dot_nt(keys_ref[h, 0], q_scr[:, pl.ds(off, PEER_HALF)])
        s2 = _dot_nt(keys_ref[h, 1], q_scr[:, pl.ds(off + PEER_HALF, PEER_HALF)])
        ranks, valss, n_ranked = _top16_untied([s1, s2])
        for half in range(2):
            rk_scr[half] = ranks[half]
            vl_scr[half] = jnp.concatenate(valss[half], axis=0)
        n_max = jnp.maximum(n_ranked[0], n_ranked[1])

        @pl.when(jnp.max(n_max) > float(K))
        def _():
            for half, s in ((0, s1), (1, s2)):
                rank, vals = _top16(s, rowid)
                rk_scr[half] = rank
                vl_scr[half] = jnp.concatenate(vals, axis=0)

        rank1 = rk_scr[0]
        rank2 = rk_scr[1]
        v1 = vl_scr[0]
        vals1 = [v1[r:r + 1] for r in range(K)]
        vals2 = [vl_scr[1, r:r + 1, :] for r in range(K)]
        taken = jnp.zeros((K, TM), F32)
        front = v1 + vals2[0]
        for _ in range(K):
            m = jnp.max(front, axis=0, keepdims=True)
            idx = jnp.min(jnp.where(front == m, kaid, 1e9), axis=0, keepdims=True)
            sel = kaid == idx
            taken = taken + jnp.where(sel, 1.0, 0.0)
            nxt = jnp.full((K, TM), NEG_INF, F32)
            for kb in range(1, K):
                nxt = jnp.where(taken == float(kb), vals2[kb], nxt)
            front = jnp.where(sel, v1 + nxt, front)
        e1k = jnp.exp(v1 - vals1[0])
        pref = jnp.zeros((1, TM), F32)
        zrow = jnp.zeros((K, TM), F32)
        for kb in range(K):
            pref = pref + jnp.exp(vals2[kb] - vals2[0])
            zrow = jnp.where(taken == float(kb + 1), pref, zrow)
        z = jnp.sum(e1k * zrow, axis=0, keepdims=True)
        na = jnp.zeros((PEER_NKEYS, TM), F32)
        for ka in range(K):
            na = jnp.where(rank1 == float(ka), taken[ka:ka + 1], na)
        r2_ref[h] = _bf(rank2)
        na_ref[h] = na
        e1_ref[h] = jnp.where(rank1 < float(K), jnp.exp(s1 - vals1[0]) / z, 0.0)
        e2_ref[h] = _bf(jnp.where(rank2 < float(K), jnp.exp(s2 - vals2[0]), 0.0))
        return carry

    lax.fori_loop(0, PEER_HEADS, head, 0)


def _peer_prep(hn, wq, keys, TM):
    T, D = hn.shape
    Q = wq.shape[1]
    shp = jax.ShapeDtypeStruct((PEER_HEADS, PEER_NKEYS, T), F32)
    shp_bf = jax.ShapeDtypeStruct((PEER_HEADS, PEER_NKEYS, T), BF16)
    ospec = pl.BlockSpec((PEER_HEADS, PEER_NKEYS, TM), lambda i: (0, 0, i))
    return pl.pallas_call(
        _peer_prep_kernel,
        grid=(T // TM,),
        in_specs=[pl.BlockSpec((TM, D), lambda i: (i, 0)),
                  pl.BlockSpec((D, Q), lambda i: (0, 0)),
                  pl.BlockSpec((PEER_HEADS, 2, PEER_NKEYS, PEER_HALF), lambda i: (0, 0, 0, 0))],
        out_specs=[ospec, ospec, ospec, ospec],
        out_shape=[shp_bf, shp, shp, shp_bf],
        scratch_shapes=[pltpu.VMEM((TM, Q), BF16),
                        pltpu.VMEM((2, PEER_NKEYS, TM), F32), pltpu.VMEM((2, PEER_TOPK, TM), F32)],
        compiler_params=_cp("parallel"),
        name="peer_prep",
    )(hn, wq, keys)


def _peer_dense_kernel(h_ref, r2_ref, na_ref, e1_ref, e2_ref, u_ref, vt_ref, x_ref, gt_ref, fg_ref,
                       o_ref, acc_t, ht_scr, s0_scr, s1_scr, aw0_scr, aw1_scr, *, block, n_blocks, final_norm):
    jj = pl.program_id(1)
    TM = h_ref.shape[0]
    SLAB = PEER_NKEYS
    EB = block

    @pl.when(jj == 0)
    def _():
        acc_t[...] = jnp.zeros_like(acc_t)
        s1_scr[...] = jnp.zeros_like(s1_scr)
        aw0_scr[...] = jnp.zeros_like(aw0_scr)
        ht_scr[...] = _bf(h_ref[...].astype(F32).T)

    zero_bf = jnp.zeros((SLAB, TM), BF16)

    def row_tile(ref, h, a):
        t = _bf(jnp.broadcast_to(ref[h, pl.ds(a, 1), :], (16, TM)))
        return jnp.concatenate([t] * (SLAB // 16), axis=0)

    def vpu_stage(k, s_ref, aw_ref):
        live = jnp.logical_and(k >= 0, k < n_blocks)
        kc = jnp.clip(k, 0, n_blocks - 1)
        for p in range(EB // (2 * SLAB)):
            lo = p * 2 * SLAB
            s = s_ref[lo:lo + 2 * SLAB, :]
            act = 0.5 * s * (1.0 + lax.erf(s * SQRT_HALF))
            weights = []
            for half in range(2):
                a = kc * (EB // SLAB) + p * 2 + half
                w = jnp.zeros((SLAB, TM), BF16)
                for h in range(PEER_HEADS):
                    chosen = r2_ref[h] < row_tile(na_ref, h, a)
                    w = w + jnp.where(chosen, e2_ref[h], zero_bf) * row_tile(e1_ref, h, a)
                weights.append(w)
            aw = _bf(act) * jnp.concatenate(weights, axis=0)
            aw_ref[lo:lo + 2 * SLAB, :] = jnp.where(live, aw, jnp.zeros_like(aw))

    out0 = _dot(vt_ref[:, 0:EB], aw0_scr[...])
    vpu_stage(2 * jj - 1, s1_scr, aw1_scr)
    s0_scr[...] = _dot(u_ref[0:EB, :], ht_scr[...])
    out1 = _dot(vt_ref[:, EB:2 * EB], aw1_scr[...])
    s1_scr[...] = _dot(u_ref[EB:2 * EB, :], ht_scr[...])
    vpu_stage(2 * jj, s0_scr, aw0_scr)
    acc_t[...] += out0 + out1

    @pl.when(jj == pl.num_programs(1) - 1)
    def _():
        xn = x_ref[...] + gt_ref[0] * acc_t[...].T
        o_ref[...] = _rms(xn, fg_ref[...]) if final_norm else xn


def _peer_dense(x, hn, sel, u, vt, gt2, TM, tiles_per_batch, final_g):
    T, D = x.shape
    E = u.shape[0]
    EB = 1024
    final_norm = final_g is not None
    fg = (final_g if final_norm else jnp.ones((D,), F32)).reshape(1, D)
    sspec = pl.BlockSpec((PEER_HEADS, PEER_NKEYS, TM), lambda i, e: (0, 0, i))
    nE = E // EB
    nP = nE // 2
    kern = functools.partial(_peer_dense_kernel, block=EB, n_blocks=nE, final_norm=final_norm)
    return pl.pallas_call(
        kern,
        grid=(T // TM, nP + 1),
        in_specs=[pl.BlockSpec((TM, D), lambda i, e: (i, 0)),
                  sspec, sspec, sspec, sspec,
                  pl.BlockSpec((2 * EB, D), lambda i, e: (jnp.minimum(e, nP - 1), 0)),
                  pl.BlockSpec((D, 2 * EB), lambda i, e: (0, jnp.maximum(e - 1, 0))),
                  pl.BlockSpec((TM, D), lambda i, e: (i, 0)),
                  pl.BlockSpec((1, 1, D), lambda i, e: (i // tiles_per_batch, 0, 0)),
                  pl.BlockSpec((1, D), lambda i, e: (0, 0))],
        out_specs=pl.BlockSpec((TM, D), lambda i, e: (i, 0)),
        out_shape=jax.ShapeDtypeStruct((T, D), F32),
        scratch_shapes=[pltpu.VMEM((D, TM), F32), pltpu.VMEM((D, TM), BF16),
                        pltpu.VMEM((EB, TM), F32), pltpu.VMEM((EB, TM), F32),
                        pltpu.VMEM((EB, TM), BF16), pltpu.VMEM((EB, TM), BF16)],
        compiler_params=_cp("parallel", "arbitrary"),
        name="peer_dense",
    )(hn, *sel, u, vt, x, gt2, fg)


def _direction_masks():
    t = jnp.arange(CHUNK)
    le = (t[None, :] <= t[:, None]).astype(F32)
    tri = jnp.stack([le, le.T])
    i = jnp.arange(GROUP_W)
    same = (i[:, None] // CHUNK) == (i[None, :] // CHUNK)
    ti, tj = i[:, None] % CHUNK, i[None, :] % CHUNK
    msl = jnp.stack([same & (tj < ti), same & (tj > ti)]).astype(F32)
    minc = jnp.stack([same & (tj <= ti), same & (tj >= ti)]).astype(F32)
    return tri, msl, minc


def _layer_params(l, w_in, shift_mu, w0, w2, a0, a2, g2, k_k, k_a, r_k, lnx_g, lnx_b, w_oA, conv_w,
                  cnorm_g, cnorm_b, w_oB, gate_b, w_out, norm2_g):
    D = D_MODEL
    row = lambda t: t.reshape(1, -1)
    zeros = jnp.zeros((LORA_W, D), F32)
    tri, msl, minc = _direction_masks()
    hs = (jnp.arange(D)[:, None] // HEAD == jnp.arange(128)[None, :]).astype(F32)
    pad_cols = P_RWKV_PAD - P_RWKV
    return dict(
        w_rk=_bf(jnp.pad(w_in[l][:, :P_RWKV], ((0, 0), (0, pad_cols)))),
        w_cg=_bf(w_in[l][:, P_RWKV:]),
        mu=jnp.pad(shift_mu[l], ((0, 0), (0, pad_cols))),
        w0=w0[l].reshape(2, 1, D), a0=a0[l].reshape(2, 1, D),
        w2p=_split(jnp.stack([jnp.concatenate([w2[l, 0], zeros]), jnp.concatenate([zeros, w2[l, 1]])])),
        a2p=_split(jnp.stack([jnp.concatenate([a2[l, 0], zeros]), jnp.concatenate([zeros, a2[l, 1]])])),
        g2p=_bf(jnp.pad(g2[l], ((0, LORA_G_PAD - LORA_G), (0, 0)))),
        k_k=row(k_k[l]), k_a=row(k_a[l]), r_k=row(r_k[l]),
        lnx_g=row(lnx_g[l]), lnx_b=row(lnx_b[l]), w_oA=_bf(w_oA[l]),
        conv_w=conv_w[l], cnorm_g=row(cnorm_g[l]), cnorm_b=row(cnorm_b[l]), w_oB=_bf(w_oB[l]),
        gate_b=row(gate_b[l]), w_out=_bf(w_out[l]), norm2_g=row(norm2_g[l]),
        hs=_bf(hs), hb=_bf(hs.T), tri=_bf(tri), msl=msl, minc=minc)


def _mixer(x, mod, pr, norm1_g, stride, h0, emit):
    B, L, D = x.shape
    sh1, sc1, gt1, sh2, sc2 = (mod[:, i:i + 1, :] for i in range(5))
    z_rk = _proj(x, sh1, sc1, norm1_g, pr["w_rk"])
    y, bv, gs, h_t = _rwkv(z_rk, pr, h0)
    if not emit:
        return None, None, h_t
    z_cg = _proj(x, sh1, sc1, norm1_g, pr["w_cg"])
    cv = _conv(z_cg, pr["conv_w"], stride)
    xn, hn = _post(x, y, bv, gs, cv, z_cg, pr, gt1, sh2, sc2)
    return xn, hn, h_t


def _peer(x, hn, wq, keys, u, v, gt2, final_g=None):
    B, L, D = x.shape
    T = B * L
    TM = min(L, 256)
    sel = _peer_prep(hn.reshape(T, D), wq, keys, TM)
    out = _peer_dense(x.reshape(T, D), hn.reshape(T, D), sel, u, v, gt2, TM, L // TM, final_g)
    return out.reshape(B, L, D)


def kernel(x, c, ctx, c_ctx, ada_w, ada_b, norm1_g, norm2_g, w_in, shift_mu, w0, w2, a0, a2, g2, k_k, k_a, r_k, lnx_g, lnx_b, w_oA, conv_w, cnorm_g, cnorm_b, w_oB, gate_b, w_out, w_q, sub_keys, peer_u, peer_v, final_g):
    B, L, D = x.shape
    depth = ada_w.shape[0]
    xc = ctx
    n_rows = -(-(B + 1) // 8) * 8
    c_rows = jnp.pad(jnp.concatenate([c, c_ctx[None, :]], axis=0), ((0, n_rows - B - 1), (0, 0)))
    zero_state = jnp.zeros((2, B, N_GROUPS, GROUP_W, GROUP_W), F32)
    for l in range(depth):
        last = l == depth - 1
        pr = _layer_params(l, w_in, shift_mu, w0, w2, a0, a2, g2, k_k, k_a, r_k, lnx_g, lnx_b, w_oA,
                           conv_w, cnorm_g, cnorm_b, w_oB, gate_b, w_out, norm2_g)
        mod_all = _modulation(c_rows, ada_w[l], ada_b[l])
        mod = mod_all[:B].reshape(B, 6, D)
        modc = jnp.broadcast_to(mod_all[B].reshape(1, 6, D), (B, 6, D))
        wq = _bf(w_q[l])
        keys = _bf(sub_keys[l])
        u = _bf(peer_u[l])
        v = _bf(peer_v[l]).T

        xc_new, hnc, ctx_states = _mixer(xc, modc, pr, norm1_g[l], 1, zero_state, emit=not last)
        xn, hn, _ = _mixer(x, mod, pr, norm1_g[l], GRID_W, ctx_states, emit=True)
        x = _peer(xn, hn, wq, keys, u, v, mod[:, 5:6, :], final_g if last else None)
        if not last:
            xc = _peer(xc_new, hnc, wq, keys, u, v, modc[:, 5:6, :])
    return x
```

```python
import functools

import jax
import jax.numpy as jnp
from jax import lax
from jax.experimental import pallas as pl
from jax.experimental.pallas import tpu as pltpu

F32 = jnp.float32
BF16 = jnp.bfloat16
HI = lax.Precision.HIGHEST

D_MODEL = 1024
HEAD = 64
HEADS = D_MODEL // HEAD
GROUP_HEADS = 4
GROUP_W = GROUP_HEADS * HEAD
N_GROUPS = HEADS // GROUP_HEADS
CHUNK = 64
RWKV_ROWS_PER_STEP = 2
LORA_W = 64
LORA_A = 64
LORA_G = 160
LORA_G_PAD = 256
P_RWKV = 3 * D_MODEL + 2 * LORA_W + 2 * LORA_A + LORA_G
P_RWKV_PAD = 3 * D_MODEL + 2 * LORA_W + 2 * LORA_A + LORA_G_PAD
COL_W1 = 3 * D_MODEL
COL_A1 = COL_W1 + 2 * LORA_W
COL_G1 = COL_A1 + 2 * LORA_A
CONV_K = 31
CONV_HALF = CONV_K // 2
GRID_W = 64
PEER_HEADS = 8
PEER_NKEYS = 128
PEER_HALF = 128
PEER_TOPK = 16
NORM_EPS = 1e-6
LN_EPS = 1e-5
GN_EPS = HEAD * 1e-5
VMEM_LIMIT = 56 * 1024 * 1024
NOT_SELECTED = 99.0
NEG_INF = float("-inf")
SQRT_HALF = 0.7071067811865476


def _cp(*sem):
    return pltpu.CompilerParams(dimension_semantics=sem, vmem_limit_bytes=VMEM_LIMIT)


def _dot(a, b):
    return jnp.dot(a, b, preferred_element_type=F32)


def _dot_hi(a, b):
    return jnp.dot(a, b, precision=HI, preferred_element_type=F32)


def _dot_nt(a, b):
    return lax.dot_general(a, b, (((1,), (1,)), ((), ())), preferred_element_type=F32)


def _dot_tn(a, b):
    return lax.dot_general(a, b, (((0,), (0,)), ((), ())), preferred_element_type=F32)


def _bf(a):
    return a.astype(BF16)


def _split(a):
    hi = a.astype(BF16)
    return hi, (a - hi.astype(F32)).astype(BF16)


def _dot_split_lhs(a, b_bf):
    hi, lo = _split(a)
    return _dot(hi, b_bf) + _dot(lo, b_bf)


def _dot_split(a, b_hi, b_lo):
    hi, lo = _split(a)
    return _dot(hi, b_hi) + _dot(hi, b_lo) + _dot(lo, b_hi)


def _head_sum(t, hs_bf, hb_bf):
    return _dot_split_lhs(_dot_split_lhs(t, hs_bf), hb_bf)


def _sigmoid(x):
    return 1.0 / (1.0 + jnp.exp(-x))


def _softplus(x):
    return jnp.maximum(x, 0.0) + jnp.log(1.0 + jnp.exp(-jnp.abs(x)))


def _rms(x, g):
    return x * lax.rsqrt(jnp.mean(x * x, axis=-1, keepdims=True) + NORM_EPS) * g


def _mod_kernel(c_ref, w_ref, b_ref, o_ref):
    c = c_ref[...]
    o_ref[...] = _dot_hi(c * _sigmoid(c), w_ref[...]) + b_ref[...]


def _modulation(c_rows, ada_w, ada_b):
    R, D = c_rows.shape
    N = ada_w.shape[1]
    TN = 512
    return pl.pallas_call(
        _mod_kernel,
        grid=(N // TN,),
        in_specs=[pl.BlockSpec((R, D), lambda j: (0, 0)),
                  pl.BlockSpec((D, TN), lambda j: (0, j)),
                  pl.BlockSpec((1, TN), lambda j: (0, j))],
        out_specs=pl.BlockSpec((R, TN), lambda j: (0, j)),
        out_shape=jax.ShapeDtypeStruct((R, N), F32),
        compiler_params=_cp("arbitrary"),
        name="modulation",
    )(c_rows, ada_w, ada_b.reshape(1, N))


def _proj_kernel(x_ref, sh_ref, sc_ref, g_ref, w_ref, o_ref):
    h = _rms(x_ref[0], g_ref[...]) * (1.0 + sc_ref[0]) + sh_ref[0]
    o_ref[0] = _dot(_bf(h), w_ref[...])


def _proj(x, sh, sc, g, w):
    B, L, D = x.shape
    N = w.shape[1]
    TM = min(L, 256)
    return pl.pallas_call(
        _proj_kernel,
        grid=(B, L // TM),
        in_specs=[pl.BlockSpec((1, TM, D), lambda b, i: (b, i, 0)),
                  pl.BlockSpec((1, 1, D), lambda b, i: (b, 0, 0)),
                  pl.BlockSpec((1, 1, D), lambda b, i: (b, 0, 0)),
                  pl.BlockSpec((1, D), lambda b, i: (0, 0)),
                  pl.BlockSpec((D, N), lambda b, i: (0, 0))],
        out_specs=pl.BlockSpec((1, TM, N), lambda b, i: (b, i, 0)),
        out_shape=jax.ShapeDtypeStruct((B, L, N), F32),
        compiler_params=_cp("parallel", "parallel"),
        name="proj",
    )(x, sh, sc, g.reshape(1, D), w)


def _proj_shift_kernel(x_ref, xp_ref, xn_ref, sh_ref, sc_ref, g_ref, w_ref, mu_ref, o_ref):
    i = pl.program_id(1)
    TM = x_ref.shape[1]
    xs = jnp.concatenate([xp_ref[0], x_ref[0], xn_ref[0]], axis=0)
    h = _rms(xs, g_ref[...]) * (1.0 + sc_ref[0]) + sh_ref[0]
    z = _dot(_bf(h), w_ref[...])
    row = lax.broadcasted_iota(jnp.int32, (TM, 1), 0)
    at_start = jnp.logical_and(row == 0, i == 0)
    at_end = jnp.logical_and(row == TM - 1, i == pl.num_programs(1) - 1)
    zp = jnp.where(at_start, 0.0, z[7:7 + TM])
    zn = jnp.where(at_end, 0.0, z[9:9 + TM])
    m0 = mu_ref[0:1, :]
    m1 = mu_ref[1:2, :]
    o_ref[0] = z[8:8 + TM] * (1.0 - m0 - m1) + m0 * zp + m1 * zn


def _proj_shift(x, sh, sc, g, w, mu):
    B, L, D = x.shape
    N = w.shape[1]
    TM = min(L, 256)
    nb8 = L // 8
    return pl.pallas_call(
        _proj_shift_kernel,
        grid=(B, L // TM),
        in_specs=[pl.BlockSpec((1, TM, D), lambda b, i: (b, i, 0)),
                  pl.BlockSpec((1, 8, D), lambda b, i: (b, jnp.maximum(i * (TM // 8) - 1, 0), 0)),
                  pl.BlockSpec((1, 8, D), lambda b, i: (b, jnp.minimum((i + 1) * (TM // 8), nb8 - 1), 0)),
                  pl.BlockSpec((1, 1, D), lambda b, i: (b, 0, 0)),
                  pl.BlockSpec((1, 1, D), lambda b, i: (b, 0, 0)),
                  pl.BlockSpec((1, D), lambda b, i: (0, 0)),
                  pl.BlockSpec((D, N), lambda b, i: (0, 0)),
                  pl.BlockSpec((2, N), lambda b, i: (0, 0))],
        out_specs=pl.BlockSpec((1, TM, N), lambda b, i: (b, i, 0)),
        out_shape=jax.ShapeDtypeStruct((B, L, N), F32),
        compiler_params=_cp("parallel", "parallel"),
        name="proj_shift",
    )(x, x, x, sh, sc, g.reshape(1, D), w, mu)


_KAP, _RT, _KT, _BT, _KH, _BH, _V = range(7)


def _rwkv_kernel(z_ref, w0_ref, w2h_ref, w2l_ref, a0_ref, a2h_ref, a2l_ref,
                 kk_ref, ka_ref, rk_ref, hs_ref, hb_ref, tri_ref, msl_ref, minc_ref, h0_ref,
                 y_ref, bv_ref, gs_ref, hT_ref, H_scr, nat_scr, pc_scr, *, n_chunks):
    d = pl.program_id(0)
    c = pl.program_id(2)
    cc = jnp.where(d == 0, c, n_chunks - 1 - c)
    C = CHUNK

    R = z_ref.shape[0]

    @pl.when(c == 0)
    def _():
        H_scr[...] = h0_ref[0]

    hs = hs_ref[...]
    hb = hb_ref[...]

    def head_sum(t):
        return _head_sum(t, hs, hb)

    def prepare(rr):
        def shifted(lo, hi):
            return z_ref[rr, :, lo:hi]

        r = shifted(0, D_MODEL)
        k = shifted(D_MODEL, 2 * D_MODEL)
        v = shifted(2 * D_MODEL, 3 * D_MODEL)
        w1 = shifted(COL_W1, COL_A1)
        a1 = shifted(COL_A1, COL_G1)
        g1 = shifted(COL_G1, P_RWKV_PAD)

        wl = w0_ref[0] + _dot_split(jnp.tanh(w1), w2h_ref[0], w2l_ref[0])
        logw = -jnp.exp(-_softplus(-wl) - 0.5)
        a = _sigmoid(_dot_split(a1, a2h_ref[0], a2l_ref[0]) + a0_ref[0])
        kkr = k * kk_ref[...]
        kk = kkr * lax.rsqrt(jnp.maximum(head_sum(kkr * kkr), 1e-24))
        kd = k * (1.0 + (a - 1.0) * ka_ref[...])
        bb = kk * a
        bv_ref[0, rr] = head_sum(r * kd * rk_ref[...]) * v
        gs_ref[0, rr] = _sigmoid(g1)

        lw_hi, lw_lo = _split(logw)
        g_in = _dot(tri_ref[0], lw_hi) + _dot(tri_ref[0], lw_lo)
        g_ex = g_in - logw
        g_c = jnp.sum(logw, axis=0, keepdims=True)
        e_inv = jnp.exp(-g_in)
        e_hat = jnp.exp(g_c - g_in)
        nat_scr[rr, _KAP] = kk * jnp.exp(g_ex)
        nat_scr[rr, _RT] = r * jnp.exp(g_in)
        nat_scr[rr, _KT] = kd * e_inv
        nat_scr[rr, _BT] = bb * e_inv
        nat_scr[rr, _KH] = kd * e_hat
        nat_scr[rr, _BH] = bb * e_hat
        nat_scr[rr, _V] = v
        pc_scr[rr] = jnp.exp(g_c)

    for rr in range(R):
        prepare(rr)

    lane_head = lax.broadcasted_iota(jnp.int32, (C, GROUP_W), 1) // HEAD
    ii = lax.broadcasted_iota(jnp.int32, (GROUP_W, GROUP_W), 0)
    jj = lax.broadcasted_iota(jnp.int32, (GROUP_W, GROUP_W), 1)
    eye = ii == jj

    def stacked(t):
        return jnp.concatenate([jnp.where(lane_head == j, t, 0.0) for j in range(GROUP_HEADS)], axis=0)

    def collapse(t):
        return t[0:C] + t[C:2 * C] + t[2 * C:3 * C] + t[3 * C:4 * C]

    G = range(R * N_GROUPS)

    def lanes(q):
        return slice((q % N_GROUPS) * GROUP_W, (q % N_GROUPS + 1) * GROUP_W)

    def nat(i, q):
        return nat_scr[q // N_GROUPS, i, :, lanes(q)]

    x_kap = [stacked(nat(_KAP, g)) for g in G]
    x_v = [_bf(stacked(nat(_V, g))) for g in G]
    x_bk = [jnp.concatenate([_bf(stacked(nat(_BT, g))), _bf(stacked(nat(_KT, g)))], axis=0) for g in G]
    kr = [_bf(jnp.concatenate([nat(_KAP, g), nat(_RT, g)], axis=0)) for g in G]
    akr = [_dot_nt(kr[g], x_bk[g]) for g in G]

    def tiled(t):
        return jnp.concatenate([t] * GROUP_HEADS, axis=0)

    same_head = (ii // C) == (jj // C)

    def block_diag(side):
        t = tiled(_bf(side))
        return jnp.where(same_head, t, jnp.zeros_like(t))

    msl_c = collapse(msl_ref[0]) > 0.5
    minc_c = collapse(minc_ref[0]) > 0.5
    n_side = [jnp.where(msl_c, akr[g][:C, :GROUP_W], 0.0) for g in G]
    a_kk = [_bf(jnp.where(msl_c, akr[g][:C, GROUP_W:], 0.0)) for g in G]
    a_rb = [_bf(jnp.where(minc_c, akr[g][C:, :GROUP_W], 0.0)) for g in G]
    a_rk = [_bf(jnp.where(minc_c, akr[g][C:, GROUP_W:], 0.0)) for g in G]
    g0 = [stacked(_dot(a_kk[g], x_v[g])) for g in G]
    y0 = [_dot(a_rk[g], x_v[g]) for g in G]
    eye_side = collapse(jnp.where(eye, 1.0, 0.0))
    p_inv = [eye_side - n_side[g] for g in G]
    m_pow = [_dot(_bf(n_side[g]), block_diag(n_side[g])) for g in G]
    for _ in range(4):
        both = [_dot(_bf(jnp.concatenate([m_pow[g], p_inv[g]], axis=0)), block_diag(m_pow[g])) for g in G]
        m_pow = [both[g][:C] for g in G]
        p_inv = [p_inv[g] + both[g][C:] for g in G]
    p_inv = [p_inv[g] + _dot(_bf(p_inv[g]), block_diag(m_pow[g])) for g in G]
    w_nat = [_dot(_bf(p_inv[g]), jnp.concatenate([_bf(x_kap[g]), _bf(g0[g])], axis=1)) for g in G]
    w12 = [_bf(jnp.concatenate([stacked(w_nat[g][:, :GROUP_W]), stacked(w_nat[g][:, GROUP_W:])], axis=1))
           for g in G]
    aw = [_dot(a_rb[g], w12[g]) for g in G]
    bw = [_dot_tn(_bf(stacked(nat(_BH, g))), w12[g]) for g in G]
    kv = [_dot_tn(_bf(stacked(nat(_KH, g))), x_v[g]) for g in G]
    h_old = [_split(H_scr[g // N_GROUPS, g % N_GROUPS]) for g in G]
    ys = []
    h_new = []
    for g in G:
        h_hi, h_lo = h_old[g]
        qb = _bf(nat(_RT, g) - aw[g][:, :GROUP_W])
        ys.append(_dot(qb, h_hi) + _dot(qb, h_lo) + y0[g] - aw[g][:, GROUP_W:])
        m_mat = jnp.where(eye, pc_scr[g // N_GROUPS, :, lanes(g)], 0.0) - bw[g][:, :GROUP_W]
        m_hi, m_lo = _split(m_mat)
        h_new.append(_dot(m_hi, h_hi) + _dot(m_hi, h_lo) + _dot(m_lo, h_hi) + kv[g] - bw[g][:, GROUP_W:])
    for rr in range(R):
        y_ref[0, rr] = jnp.concatenate(ys[rr * N_GROUPS:(rr + 1) * N_GROUPS], axis=1)
    for g in G:
        H_scr[g // N_GROUPS, g % N_GROUPS] = h_new[g]

    @pl.when(c == n_chunks - 1)
    def _():
        hT_ref[0] = H_scr[...]


def _rwkv(z, pr, h0):
    B, L, N = z.shape
    C = CHUNK
    nC = L // C
    D = D_MODEL

    def cidx(d, c):
        return c + d * (nC - 1 - 2 * c)

    zmap = lambda d, b, c: (b, cidx(d, c), 0)
    const2 = lambda d, b, c: (0, 0)
    dir3 = lambda d, b, c: (d, 0, 0)
    omap = lambda d, b, c: (d, b, cidx(d, c), 0)
    smap = lambda d, b, c: (d, b, 0, 0, 0)
    R = RWKV_ROWS_PER_STEP if B % RWKV_ROWS_PER_STEP == 0 else 1
    kern = functools.partial(_rwkv_kernel, n_chunks=nC)
    return pl.pallas_call(
        kern,
        grid=(2, B // R, nC),
        in_specs=[pl.BlockSpec((R, C, N), zmap),
                  pl.BlockSpec((1, 1, D), dir3),
                  pl.BlockSpec((1, 2 * LORA_W, D), dir3),
                  pl.BlockSpec((1, 2 * LORA_W, D), dir3),
                  pl.BlockSpec((1, 1, D), dir3),
                  pl.BlockSpec((1, 2 * LORA_A, D), dir3),
                  pl.BlockSpec((1, 2 * LORA_A, D), dir3),
                  pl.BlockSpec((1, D), const2),
                  pl.BlockSpec((1, D), const2),
                  pl.BlockSpec((1, D), const2),
                  pl.BlockSpec((D, 128), const2),
                  pl.BlockSpec((128, D), const2),
                  pl.BlockSpec((1, C, C), dir3),
                  pl.BlockSpec((1, GROUP_W, GROUP_W), dir3),
                  pl.BlockSpec((1, GROUP_W, GROUP_W), dir3),
                  pl.BlockSpec((1, R, N_GROUPS, GROUP_W, GROUP_W), smap)],
        out_specs=[pl.BlockSpec((1, R, C, D), omap),
                   pl.BlockSpec((1, R, C, D), omap),
                   pl.BlockSpec((1, R, C, LORA_G_PAD), omap),
                   pl.BlockSpec((1, R, N_GROUPS, GROUP_W, GROUP_W), smap)],
        out_shape=[jax.ShapeDtypeStruct((2, B, L, D), F32),
                   jax.ShapeDtypeStruct((2, B, L, D), F32),
                   jax.ShapeDtypeStruct((2, B, L, LORA_G_PAD), F32),
                   jax.ShapeDtypeStruct((2, B, N_GROUPS, GROUP_W, GROUP_W), F32)],
        scratch_shapes=[pltpu.VMEM((R, N_GROUPS, GROUP_W, GROUP_W), F32),
                        pltpu.VMEM((R, 7, C, D), F32),
                        pltpu.VMEM((R, 1, D), F32)],
        compiler_params=_cp("arbitrary", "arbitrary", "arbitrary"),
        name="rwkv",
    )(z, pr["w0"], *pr["w2p"], pr["a0"], *pr["a2p"], pr["k_k"], pr["k_a"], pr["r_k"],
      pr["hs"], pr["hb"], pr["tri"], pr["msl"], pr["minc"], h0)


def _conv_kernel(za_ref, zb_ref, w_ref, o_ref, upad, *, L, stride, rows_per_step):
    pad = CONV_HALF * stride
    TC = za_ref.shape[-1]
    upad[0:pad, :] = jnp.zeros((pad, TC), F32)
    upad[pad + L:pad + L + pad, :] = jnp.zeros((pad, TC), F32)
    upad[pad:pad + L, :] = za_ref[0] * _sigmoid(zb_ref[0])
    RB = rows_per_step

    def block(r0):
        acc = jnp.zeros((RB, TC), F32)
        for j in range(CONV_K):
            acc = acc + w_ref[j:j + 1, :] * upad[pl.ds(r0 + j * stride, RB), :]
        o_ref[0, pl.ds(r0, RB), :] = acc

    if stride % 8 == 0:
        def body(i, carry):
            block(pl.multiple_of(i * RB, RB))
            return carry
        lax.fori_loop(0, L // RB, body, 0)
    else:
        for i in range(L // RB):
            block(i * RB)


def _conv(zcg, conv_w, stride):
    B, L, _ = zcg.shape
    D = D_MODEL
    TC = 128
    nct = D // TC
    wpad = jnp.pad(conv_w, ((0, 32 - CONV_K), (0, 0)))
    RB = min(L, 128)
    kern = functools.partial(_conv_kernel, L=L, stride=stride, rows_per_step=RB)
    return pl.pallas_call(
        kern,
        grid=(B, nct),
        in_specs=[pl.BlockSpec((1, L, TC), lambda b, j: (b, 0, j)),
                  pl.BlockSpec((1, L, TC), lambda b, j: (b, 0, j + nct)),
                  pl.BlockSpec((32, TC), lambda b, j: (0, j))],
        out_specs=pl.BlockSpec((1, L, TC), lambda b, j: (b, 0, j)),
        out_shape=jax.ShapeDtypeStruct((B, L, D), F32),
        scratch_shapes=[pltpu.VMEM((L + 2 * CONV_HALF * stride, TC), F32)],
        compiler_params=_cp("parallel", "parallel"),
        name="conv",
    )(zcg, zcg, wpad)


def _post_kernel(x_ref, yf_ref, yb_ref, bf_ref, bb_ref, gs_ref, cv_ref, zg_ref,
                 lng_ref, lnb_ref, g2_ref, woa_ref, cng_ref, cnb_ref, wob_ref, gb_ref, wout_ref,
                 gt_ref, n2g_ref, sh2_ref, sc2_ref, hs_ref, hb_ref, xo_ref, hn_ref):
    D = D_MODEL
    hs = hs_ref[...]
    hb = hb_ref[...]

    def head_mean(t):
        return _head_sum(t, hs, hb) * (1.0 / HEAD)

    o = yf_ref[0, 0] + yb_ref[0, 0]
    oc = o - head_mean(o)
    on = oc * lax.rsqrt(head_mean(oc * oc) + GN_EPS) * lng_ref[...] + lnb_ref[...]
    on = on + bf_ref[0, 0] + bb_ref[0, 0]
    gate = _dot(_bf(gs_ref[0, 0]), g2_ref[...])
    y_a = _dot(_bf(on * gate), woa_ref[...])

    cv = cv_ref[0]
    cm = jnp.mean(cv, axis=-1, keepdims=True)
    cc = cv - cm
    cn = cc * lax.rsqrt(jnp.mean(cc * cc, axis=-1, keepdims=True) + LN_EPS) * cng_ref[...] + cnb_ref[...]
    y_b = _dot(_bf(cn * _sigmoid(cn)), wob_ref[...])

    gates = _sigmoid(zg_ref[0] + gb_ref[...])
    m = gates[:, :D] * y_a + gates[:, D:] * y_b
    xn = x_ref[0] + gt_ref[0] * _dot(_bf(m), wout_ref[...])
    xo_ref[0] = xn
    hn_ref[0] = _bf(_rms(xn, n2g_ref[...]) * (1.0 + sc2_ref[0]) + sh2_ref[0])


def _post(x, y, bv, gs, cv, zcg, pr, gt1, sh2, sc2):
    B, L, D = x.shape
    TM = min(L, 256)
    tok = lambda b, i: (b, i, 0)
    fwd = lambda b, i: (0, b, i, 0)
    bwd = lambda b, i: (1, b, i, 0)
    cst = lambda b, i: (0, 0)
    per_b = lambda b, i: (b, 0, 0)
    row = pl.BlockSpec((1, D), cst)
    mat = pl.BlockSpec((D, D), cst)
    return pl.pallas_call(
        _post_kernel,
        grid=(B, L // TM),
        in_specs=[pl.BlockSpec((1, TM, D), tok),
                  pl.BlockSpec((1, 1, TM, D), fwd), pl.BlockSpec((1, 1, TM, D), bwd),
                  pl.BlockSpec((1, 1, TM, D), fwd), pl.BlockSpec((1, 1, TM, D), bwd),
                  pl.BlockSpec((1, 1, TM, LORA_G_PAD), fwd),
                  pl.BlockSpec((1, TM, D), tok),
                  pl.BlockSpec((1, TM, 2 * D), lambda b, i: (b, i, 1)),
                  row, row, pl.BlockSpec((LORA_G_PAD, D), cst), mat,
                  row, row, mat, pl.BlockSpec((1, 2 * D), cst), mat,
                  pl.BlockSpec((1, 1, D), per_b), row,
                  pl.BlockSpec((1, 1, D), per_b), pl.BlockSpec((1, 1, D), per_b),
                  pl.BlockSpec((D, 128), cst), pl.BlockSpec((128, D), cst)],
        out_specs=[pl.BlockSpec((1, TM, D), tok), pl.BlockSpec((1, TM, D), tok)],
        out_shape=[jax.ShapeDtypeStruct((B, L, D), F32), jax.ShapeDtypeStruct((B, L, D), BF16)],
        compiler_params=_cp("parallel", "parallel"),
        name="post",
    )(x, y, y, bv, bv, gs, cv, zcg,
      pr["lnx_g"], pr["lnx_b"], pr["g2p"], pr["w_oA"], pr["cnorm_g"], pr["cnorm_b"], pr["w_oB"],
      pr["gate_b"], pr["w_out"], gt1, pr["norm2_g"], sh2, sc2, pr["hs"], pr["hb"])


def _top16(s, rowid):
    rank = jnp.full(s.shape, NOT_SELECTED, F32)
    cur = s
    vals = []
    for r in range(PEER_TOPK):
        m = jnp.max(cur, axis=0, keepdims=True)
        idx = jnp.min(jnp.where(cur == m, rowid, 1e9), axis=0, keepdims=True)
        sel = rowid == idx
        rank = jnp.where(sel, float(r), rank)
        cur = jnp.where(sel, NEG_INF, cur)
        vals.append(m)
    return rank, vals


def _top16_untied(arrays):
    n = range(len(arrays))
    rank = [jnp.full(s.shape, NOT_SELECTED, F32) for s in arrays]
    cur = list(arrays)
    vals = [[] for _ in n]
    for r in range(PEER_TOPK):
        m = [jnp.max(cur[i], axis=0, keepdims=True) for i in n]
        sel = [cur[i] == m[i] for i in n]
        rank = [jnp.where(sel[i], float(r), rank[i]) for i in n]
        cur = [jnp.where(sel[i], NEG_INF, cur[i]) for i in n]
        for i in n:
            vals[i].append(m[i])
    n_ranked = [jnp.sum(jnp.where(rank[i] < float(PEER_TOPK), 1.0, 0.0), axis=0, keepdims=True) for i in n]
    return rank, vals, n_ranked


def _peer_prep_kernel(h_ref, wq_ref, keys_ref, r2_ref, na_ref, e1_ref, e2_ref, q_scr, rk_scr, vl_scr):
    TM = h_ref.shape[0]
    K = PEER_TOPK
    q_scr[...] = _bf(_dot(h_ref[...], wq_ref[...]))
    rowid = lax.broadcasted_iota(jnp.int32, (PEER_NKEYS, TM), 0).astype(F32)
    kaid = lax.broadcasted_iota(jnp.int32, (K, TM), 0).astype(F32)

    def head(h, carry):
        off = pl.multiple_of(h * 2 * PEER_HALF, 2 * PEER_HALF)
        s1 = _dot_nt(keys_ref[h, 0], q_scr[:, pl.ds(off, PEER_HALF)])
        s2 = _dot_nt(keys_ref[h, 1], q_scr[:, pl.ds(off + PEER_HALF, PEER_HALF)])
        ranks, valss, n_ranked = _top16_untied([s1, s2])
        for half in range(2):
            rk_scr[half] = ranks[half]
            vl_scr[half] = jnp.concatenate(valss[half], axis=0)
        n_max = jnp.maximum(n_ranked[0], n_ranked[1])

        @pl.when(jnp.max(n_max) > float(K))
        def _():
            for half, s in ((0, s1), (1, s2)):
                rank, vals = _top16(s, rowid)
                rk_scr[half] = rank
                vl_scr[half] = jnp.concatenate(vals, axis=0)

        rank1 = rk_scr[0]
        rank2 = rk_scr[1]
        v1 = vl_scr[0]
        vals1 = [v1[r:r + 1] for r in range(K)]
        vals2 = [vl_scr[1, r:r + 1, :] for r in range(K)]
        taken = jnp.zeros((K, TM), F32)
        front = v1 + vals2[0]
        for _ in range(K):
            m = jnp.max(front, axis=0, keepdims=True)
            idx = jnp.min(jnp.where(front == m, kaid, 1e9), axis=0, keepdims=True)
            sel = kaid == idx
            taken = taken + jnp.where(sel, 1.0, 0.0)
            nxt = jnp.full((K, TM), NEG_INF, F32)
            for kb in range(1, K):
                nxt = jnp.where(taken == float(kb), vals2[kb], nxt)
            front = jnp.where(sel, v1 + nxt, front)
        e1k = jnp.exp(v1 - vals1[0])
        pref = jnp.zeros((1, TM), F32)
        zrow = jnp.zeros((K, TM), F32)
        for kb in range(K):
            pref = pref + jnp.exp(vals2[kb] - vals2[0])
            zrow = jnp.where(taken == float(kb + 1), pref, zrow)
        z = jnp.sum(e1k * zrow, axis=0, keepdims=True)
        na = jnp.zeros((PEER_NKEYS, TM), F32)
        for ka in range(K):
            na = jnp.where(rank1 == float(ka), taken[ka:ka + 1], na)
        r2_ref[h] = _bf(rank2)
        na_ref[h] = na
        e1_ref[h] = jnp.where(rank1 < float(K), jnp.exp(s1 - vals1[0]) / z, 0.0)
        e2_ref[h] = _bf(jnp.where(rank2 < float(K), jnp.exp(s2 - vals2[0]), 0.0))
        return carry

    lax.fori_loop(0, PEER_HEADS, head, 0)


def _peer_prep(hn, wq, keys, TM):
    T, D = hn.shape
    Q = wq.shape[1]
    shp = jax.ShapeDtypeStruct((PEER_HEADS, PEER_NKEYS, T), F32)
    shp_bf = jax.ShapeDtypeStruct((PEER_HEADS, PEER_NKEYS, T), BF16)
    ospec = pl.BlockSpec((PEER_HEADS, PEER_NKEYS, TM), lambda i: (0, 0, i))
    return pl.pallas_call(
        _peer_prep_kernel,
        grid=(T // TM,),
        in_specs=[pl.BlockSpec((TM, D), lambda i: (i, 0)),
                  pl.BlockSpec((D, Q), lambda i: (0, 0)),
                  pl.BlockSpec((PEER_HEADS, 2, PEER_NKEYS, PEER_HALF), lambda i: (0, 0, 0, 0))],
        out_specs=[ospec, ospec, ospec, ospec],
        out_shape=[shp_bf, shp, shp, shp_bf],
        scratch_shapes=[pltpu.VMEM((TM, Q), BF16),
                        pltpu.VMEM((2, PEER_NKEYS, TM), F32), pltpu.VMEM((2, PEER_TOPK, TM), F32)],
        compiler_params=_cp("parallel"),
        name="peer_prep",
    )(hn, wq, keys)


def _peer_dense_kernel(h_ref, r2_ref, na_ref, e1_ref, e2_ref, u_ref, vt_ref, x_ref, gt_ref, fg_ref,
                       o_ref, acc_t, ht_scr, s0_scr, s1_scr, aw0_scr, aw1_scr, *, block, n_blocks, final_norm):
    jj = pl.program_id(1)
    TM = h_ref.shape[0]
    SLAB = PEER_NKEYS
    EB = block

    @pl.when(jj == 0)
    def _():
        acc_t[...] = jnp.zeros_like(acc_t)
        s1_scr[...] = jnp.zeros_like(s1_scr)
        aw0_scr[...] = jnp.zeros_like(aw0_scr)
        ht_scr[...] = _bf(h_ref[...].astype(F32).T)

    zero_bf = jnp.zeros((SLAB, TM), BF16)

    def row_tile(ref, h, a):
        t = _bf(jnp.broadcast_to(ref[h, pl.ds(a, 1), :], (16, TM)))
        return jnp.concatenate([t] * (SLAB // 16), axis=0)

    def vpu_stage(k, s_ref, aw_ref):
        live = jnp.logical_and(k >= 0, k < n_blocks)
        kc = jnp.clip(k, 0, n_blocks - 1)
        for p in range(EB // (2 * SLAB)):
            lo = p * 2 * SLAB
            s = s_ref[lo:lo + 2 * SLAB, :]
            act = 0.5 * s * (1.0 + lax.erf(s * SQRT_HALF))
            weights = []
            for half in range(2):
                a = kc * (EB // SLAB) + p * 2 + half
                w = jnp.zeros((SLAB, TM), BF16)
                for h in range(PEER_HEADS):
                    chosen = r2_ref[h] < row_tile(na_ref, h, a)
                    w = w + jnp.where(chosen, e2_ref[h], zero_bf) * row_tile(e1_ref, h, a)
                weights.append(w)
            aw = _bf(act) * jnp.concatenate(weights, axis=0)
            aw_ref[lo:lo + 2 * SLAB, :] = jnp.where(live, aw, jnp.zeros_like(aw))

    out0 = _dot(vt_ref[:, 0:EB], aw0_scr[...])
    vpu_stage(2 * jj - 1, s1_scr, aw1_scr)
    s0_scr[...] = _dot(u_ref[0:EB, :], ht_scr[...])
    out1 = _dot(vt_ref[:, EB:2 * EB], aw1_scr[...])
    s1_scr[...] = _dot(u_ref[EB:2 * EB, :], ht_scr[...])
    vpu_stage(2 * jj, s0_scr, aw0_scr)
    acc_t[...] += out0 + out1

    @pl.when(jj == pl.num_programs(1) - 1)
    def _():
        xn = x_ref[...] + gt_ref[0] * acc_t[...].T
        o_ref[...] = _rms(xn, fg_ref[...]) if final_norm else xn


def _peer_dense(x, hn, sel, u, vt, gt2, TM, tiles_per_batch, final_g):
    T, D = x.shape
    E = u.shape[0]
    EB = 1024
    final_norm = final_g is not None
    fg = (final_g if final_norm else jnp.ones((D,), F32)).reshape(1, D)
    sspec = pl.BlockSpec((PEER_HEADS, PEER_NKEYS, TM), lambda i, e: (0, 0, i))
    nE = E // EB
    nP = nE // 2
    kern = functools.partial(_peer_dense_kernel, block=EB, n_blocks=nE, final_norm=final_norm)
    return pl.pallas_call(
        kern,
        grid=(T // TM, nP + 1),
        in_specs=[pl.BlockSpec((TM, D), lambda i, e: (i, 0)),
                  sspec, sspec, sspec, sspec,
                  pl.BlockSpec((2 * EB, D), lambda i, e: (jnp.minimum(e, nP - 1), 0)),
                  pl.BlockSpec((D, 2 * EB), lambda i, e: (0, jnp.maximum(e - 1, 0))),
                  pl.BlockSpec((TM, D), lambda i, e: (i, 0)),
                  pl.BlockSpec((1, 1, D), lambda i, e: (i // tiles_per_batch, 0, 0)),
                  pl.BlockSpec((1, D), lambda i, e: (0, 0))],
        out_specs=pl.BlockSpec((TM, D), lambda i, e: (i, 0)),
        out_shape=jax.ShapeDtypeStruct((T, D), F32),
        scratch_shapes=[pltpu.VMEM((D, TM), F32), pltpu.VMEM((D, TM), BF16),
                        pltpu.VMEM((EB, TM), F32), pltpu.VMEM((EB, TM), F32),
                        pltpu.VMEM((EB, TM), BF16), pltpu.VMEM((EB, TM), BF16)],
        compiler_params=_cp("parallel", "arbitrary"),
        name="peer_dense",
    )(hn, *sel, u, vt, x, gt2, fg)


def _direction_masks():
    t = jnp.arange(CHUNK)
    le = (t[None, :] <= t[:, None]).astype(F32)
    tri = jnp.stack([le, le.T])
    i = jnp.arange(GROUP_W)
    same = (i[:, None] // CHUNK) == (i[None, :] // CHUNK)
    ti, tj = i[:, None] % CHUNK, i[None, :] % CHUNK
    msl = jnp.stack([same & (tj < ti), same & (tj > ti)]).astype(F32)
    minc = jnp.stack([same & (tj <= ti), same & (tj >= ti)]).astype(F32)
    return tri, msl, minc


def _layer_params(l, w_in, shift_mu, w0, w2, a0, a2, g2, k_k, k_a, r_k, lnx_g, lnx_b, w_oA, conv_w,
                  cnorm_g, cnorm_b, w_oB, gate_b, w_out, norm2_g):
    D = D_MODEL
    row = lambda t: t.reshape(1, -1)
    zeros = jnp.zeros((LORA_W, D), F32)
    tri, msl, minc = _direction_masks()
    hs = (jnp.arange(D)[:, None] // HEAD == jnp.arange(128)[None, :]).astype(F32)
    pad_cols = P_RWKV_PAD - P_RWKV
    return dict(
        w_rk=_bf(jnp.pad(w_in[l][:, :P_RWKV], ((0, 0), (0, pad_cols)))),
        w_cg=_bf(w_in[l][:, P_RWKV:]),
        mu=jnp.pad(shift_mu[l], ((0, 0), (0, pad_cols))),
        w0=w0[l].reshape(2, 1, D), a0=a0[l].reshape(2, 1, D),
        w2p=_split(jnp.stack([jnp.concatenate([w2[l, 0], zeros]), jnp.concatenate([zeros, w2[l, 1]])])),
        a2p=_split(jnp.stack([jnp.concatenate([a2[l, 0], zeros]), jnp.concatenate([zeros, a2[l, 1]])])),
        g2p=_bf(jnp.pad(g2[l], ((0, LORA_G_PAD - LORA_G), (0, 0)))),
        k_k=row(k_k[l]), k_a=row(k_a[l]), r_k=row(r_k[l]),
        lnx_g=row(lnx_g[l]), lnx_b=row(lnx_b[l]), w_oA=_bf(w_oA[l]),
        conv_w=conv_w[l], cnorm_g=row(cnorm_g[l]), cnorm_b=row(cnorm_b[l]), w_oB=_bf(w_oB[l]),
        gate_b=row(gate_b[l]), w_out=_bf(w_out[l]), norm2_g=row(norm2_g[l]),
        hs=_bf(hs), hb=_bf(hs.T), tri=_bf(tri), msl=msl, minc=minc)


def _mixer(x, mod, pr, norm1_g, stride, h0, emit):
    B, L, D = x.shape
    sh1, sc1, gt1, sh2, sc2 = (mod[:, i:i + 1, :] for i in range(5))
    z_rk = _proj_shift(x, sh1, sc1, norm1_g, pr["w_rk"], pr["mu"])
    y, bv, gs, h_t = _rwkv(z_rk, pr, h0)
    if not emit:
        return None, None, h_t
    z_cg = _proj(x, sh1, sc1, norm1_g, pr["w_cg"])
    cv = _conv(z_cg, pr["conv_w"], stride)
    xn, hn = _post(x, y, bv, gs, cv, z_cg, pr, gt1, sh2, sc2)
    return xn, hn, h_t


def _peer(x, hn, wq, keys, u, v, gt2, final_g=None):
    B, L, D = x.shape
    T = B * L
    TM = min(L, 256)
    sel = _peer_prep(hn.reshape(T, D), wq, keys, TM)
    out = _peer_dense(x.reshape(T, D), hn.reshape(T, D), sel, u, v, gt2, TM, L // TM, final_g)
    return out.reshape(B, L, D)


def kernel(x, c, ctx, c_ctx, ada_w, ada_b, norm1_g, norm2_g, w_in, shift_mu, w0, w2, a0, a2, g2, k_k, k_a, r_k, lnx_g, lnx_b, w_oA, conv_w, cnorm_g, cnorm_b, w_oB, gate_b, w_out, w_q, sub_keys, peer_u, peer_v, final_g):
    B, L, D = x.shape
    depth = ada_w.shape[0]
    xc = ctx
    n_rows = -(-(B + 1) // 8) * 8
    c_rows = jnp.pad(jnp.concatenate([c, c_ctx[None, :]], axis=0), ((0, n_rows - B - 1), (0, 0)))
    zero_state = jnp.zeros((2, B, N_GROUPS, GROUP_W, GROUP_W), F32)
    for l in range(depth):
        last = l == depth - 1
        pr = _layer_params(l, w_in, shift_mu, w0, w2, a0, a2, g2, k_k, k_a, r_k, lnx_g, lnx_b, w_oA,
                           conv_w, cnorm_g, cnorm_b, w_oB, gate_b, w_out, norm2_g)
        mod_all = _modulation(c_rows, ada_w[l], ada_b[l])
        mod = mod_all[:B].reshape(B, 6, D)
        modc = jnp.broadcast_to(mod_all[B].reshape(1, 6, D), (B, 6, D))
        wq = _bf(w_q[l])
        keys = _bf(sub_keys[l])
        u = _bf(peer_u[l])
        v = _bf(peer_v[l]).T

        xc_new, hnc, ctx_states = _mixer(xc, modc, pr, norm1_g[l], 1, zero_state, emit=not last)
        xn, hn, _ = _mixer(x, mod, pr, norm1_g[l], GRID_W, ctx_states, emit=True)
        x = _peer(xn, hn, wq, keys, u, v, mod[:, 5:6, :], final_g if last else None)
        if not last:
            xc = _peer(xc_new, hnc, wq, keys, u, v, modc[:, 5:6, :])
    return x
```

```python
import functools

import jax
import jax.numpy as jnp
from jax import lax
from jax.experimental import pallas as pl
from jax.experimental.pallas import tpu as pltpu

F32 = jnp.float32
BF16 = jnp.bfloat16
HI = lax.Precision.HIGHEST

D_MODEL = 1024
HEAD = 64
HEADS = D_MODEL // HEAD
GROUP_HEADS = 4
GROUP_W = GROUP_HEADS * HEAD
N_GROUPS = HEADS // GROUP_HEADS
CHUNK = 64
assert CHUNK == HEAD
RWKV_ROWS_PER_STEP = 2
LORA_W = 64
LORA_A = 64
LORA_G = 160
LORA_G_PAD = 256
P_RWKV = 3 * D_MODEL + 2 * LORA_W + 2 * LORA_A + LORA_G
P_RWKV_PAD = 3 * D_MODEL + 2 * LORA_W + 2 * LORA_A + LORA_G_PAD
COL_W1 = 3 * D_MODEL
COL_A1 = COL_W1 + 2 * LORA_W
COL_G1 = COL_A1 + 2 * LORA_A
CONV_K = 31
CONV_HALF = CONV_K // 2
GRID_W = 64
PEER_HEADS = 8
PEER_NKEYS = 128
PEER_HALF = 128
PEER_TOPK = 16
NORM_EPS = 1e-6
LN_EPS = 1e-5
GN_EPS = HEAD * 1e-5
VMEM_LIMIT = 56 * 1024 * 1024
NOT_SELECTED = 99.0
NEG_INF = float("-inf")
SQRT_HALF = 0.7071067811865476


def _cp(*sem):
    return pltpu.CompilerParams(dimension_semantics=sem, vmem_limit_bytes=VMEM_LIMIT)


def _dot(a, b):
    return jnp.dot(a, b, preferred_element_type=F32)


def _dot_hi(a, b):
    return jnp.dot(a, b, precision=HI, preferred_element_type=F32)


def _dot_nt(a, b):
    return lax.dot_general(a, b, (((1,), (1,)), ((), ())), preferred_element_type=F32)


def _dot_tn(a, b):
    return lax.dot_general(a, b, (((0,), (0,)), ((), ())), preferred_element_type=F32)


def _bf(a):
    return a.astype(BF16)


def _split(a):
    hi = a.astype(BF16)
    return hi, (a - hi.astype(F32)).astype(BF16)


def _dot_split_lhs(a, b_bf):
    hi, lo = _split(a)
    return _dot(hi, b_bf) + _dot(lo, b_bf)


def _dot_split(a, b_hi, b_lo):
    hi, lo = _split(a)
    return _dot(hi, b_hi) + _dot(hi, b_lo) + _dot(lo, b_hi)


def _head_sum(t, hs_bf, hb_bf):
    return _dot_split_lhs(_dot_split_lhs(t, hs_bf), hb_bf)


def _sigmoid(x):
    return 1.0 / (1.0 + jnp.exp(-x))


def _softplus(x):
    return jnp.maximum(x, 0.0) + jnp.log(1.0 + jnp.exp(-jnp.abs(x)))


def _rms(x, g):
    return x * lax.rsqrt(jnp.mean(x * x, axis=-1, keepdims=True) + NORM_EPS) * g


def _mod_kernel(c_ref, w_ref, b_ref, o_ref):
    c = c_ref[...]
    o_ref[...] = _dot_hi(c * _sigmoid(c), w_ref[...]) + b_ref[...]


def _modulation(c_rows, ada_w, ada_b):
    R, D = c_rows.shape
    N = ada_w.shape[1]
    TN = 512
    return pl.pallas_call(
        _mod_kernel,
        grid=(N // TN,),
        in_specs=[pl.BlockSpec((R, D), lambda j: (0, 0)),
                  pl.BlockSpec((D, TN), lambda j: (0, j)),
                  pl.BlockSpec((1, TN), lambda j: (0, j))],
        out_specs=pl.BlockSpec((R, TN), lambda j: (0, j)),
        out_shape=jax.ShapeDtypeStruct((R, N), F32),
        compiler_params=_cp("arbitrary"),
        name="modulation",
    )(c_rows, ada_w, ada_b.reshape(1, N))


def _proj_kernel(x_ref, sh_ref, sc_ref, g_ref, w_ref, o_ref):
    h = _rms(x_ref[0], g_ref[...]) * (1.0 + sc_ref[0]) + sh_ref[0]
    o_ref[0] = _dot(_bf(h), w_ref[...])


def _proj(x, sh, sc, g, w):
    B, L, D = x.shape
    N = w.shape[1]
    TM = min(L, 256)
    return pl.pallas_call(
        _proj_kernel,
        grid=(B, L // TM),
        in_specs=[pl.BlockSpec((1, TM, D), lambda b, i: (b, i, 0)),
                  pl.BlockSpec((1, 1, D), lambda b, i: (b, 0, 0)),
                  pl.BlockSpec((1, 1, D), lambda b, i: (b, 0, 0)),
                  pl.BlockSpec((1, D), lambda b, i: (0, 0)),
                  pl.BlockSpec((D, N), lambda b, i: (0, 0))],
        out_specs=pl.BlockSpec((1, TM, N), lambda b, i: (b, i, 0)),
        out_shape=jax.ShapeDtypeStruct((B, L, N), F32),
        compiler_params=_cp("parallel", "parallel"),
        name="proj",
    )(x, sh, sc, g.reshape(1, D), w)


def _proj_shift_kernel(x_ref, xp_ref, xn_ref, sh_ref, sc_ref, g_ref, w_ref, mu_ref, o_ref):
    i = pl.program_id(1)
    TM = x_ref.shape[1]
    xs = jnp.concatenate([xp_ref[0], x_ref[0], xn_ref[0]], axis=0)
    h = _rms(xs, g_ref[...]) * (1.0 + sc_ref[0]) + sh_ref[0]
    z = _dot(_bf(h), w_ref[...])
    row = lax.broadcasted_iota(jnp.int32, (TM, 1), 0)
    at_start = jnp.logical_and(row == 0, i == 0)
    at_end = jnp.logical_and(row == TM - 1, i == pl.num_programs(1) - 1)
    zp = jnp.where(at_start, 0.0, z[7:7 + TM])
    zn = jnp.where(at_end, 0.0, z[9:9 + TM])
    m0 = mu_ref[0:1, :]
    m1 = mu_ref[1:2, :]
    o_ref[0] = z[8:8 + TM] * (1.0 - m0 - m1) + m0 * zp + m1 * zn


def _proj_shift(x, sh, sc, g, w, mu):
    B, L, D = x.shape
    N = w.shape[1]
    TM = min(L, 256)
    nb8 = L // 8
    return pl.pallas_call(
        _proj_shift_kernel,
        grid=(B, L // TM),
        in_specs=[pl.BlockSpec((1, TM, D), lambda b, i: (b, i, 0)),
                  pl.BlockSpec((1, 8, D), lambda b, i: (b, jnp.maximum(i * (TM // 8) - 1, 0), 0)),
                  pl.BlockSpec((1, 8, D), lambda b, i: (b, jnp.minimum((i + 1) * (TM // 8), nb8 - 1), 0)),
                  pl.BlockSpec((1, 1, D), lambda b, i: (b, 0, 0)),
                  pl.BlockSpec((1, 1, D), lambda b, i: (b, 0, 0)),
                  pl.BlockSpec((1, D), lambda b, i: (0, 0)),
                  pl.BlockSpec((D, N), lambda b, i: (0, 0)),
                  pl.BlockSpec((2, N), lambda b, i: (0, 0))],
        out_specs=pl.BlockSpec((1, TM, N), lambda b, i: (b, i, 0)),
        out_shape=jax.ShapeDtypeStruct((B, L, N), F32),
        compiler_params=_cp("parallel", "parallel"),
        name="proj_shift",
    )(x, x, x, sh, sc, g.reshape(1, D), w, mu)


_KAP, _RT, _KT, _BT, _KH, _BH, _V = range(7)


def _rwkv_kernel(z_ref, w0_ref, w2h_ref, w2l_ref, a0_ref, a2h_ref, a2l_ref,
                 kk_ref, ka_ref, rk_ref, hs_ref, hb_ref, tri_ref, msl_ref, minc_ref, h0_ref,
                 y_ref, bv_ref, gs_ref, hT_ref, H_scr, nat_scr, pc_scr, *, n_chunks):
    d = pl.program_id(0)
    c = pl.program_id(2)
    cc = jnp.where(d == 0, c, n_chunks - 1 - c)
    C = CHUNK

    R = z_ref.shape[0]

    @pl.when(c == 0)
    def _():
        H_scr[...] = h0_ref[0]

    hs = hs_ref[...]
    hb = hb_ref[...]

    def head_sum(t):
        return _head_sum(t, hs, hb)

    def prepare(rr):
        def shifted(lo, hi):
            return z_ref[rr, :, lo:hi]

        r = shifted(0, D_MODEL)
        k = shifted(D_MODEL, 2 * D_MODEL)
        v = shifted(2 * D_MODEL, 3 * D_MODEL)
        w1 = shifted(COL_W1, COL_A1)
        a1 = shifted(COL_A1, COL_G1)
        g1 = shifted(COL_G1, P_RWKV_PAD)

        wl = w0_ref[0] + _dot_split(jnp.tanh(w1), w2h_ref[0], w2l_ref[0])
        logw = -jnp.exp(-_softplus(-wl) - 0.5)
        a = _sigmoid(_dot_split(a1, a2h_ref[0], a2l_ref[0]) + a0_ref[0])
        kkr = k * kk_ref[...]
        kk = kkr * lax.rsqrt(jnp.maximum(head_sum(kkr * kkr), 1e-24))
        kd = k * (1.0 + (a - 1.0) * ka_ref[...])
        bb = kk * a
        bv_ref[0, rr] = head_sum(r * kd * rk_ref[...]) * v
        gs_ref[0, rr] = _sigmoid(g1)

        lw_hi, lw_lo = _split(logw)
        g_in = _dot(tri_ref[0], lw_hi) + _dot(tri_ref[0], lw_lo)
        g_ex = g_in - logw
        g_c = jnp.sum(logw, axis=0, keepdims=True)
        e_inv = jnp.exp(-g_in)
        e_hat = jnp.exp(g_c - g_in)
        nat_scr[rr, _KAP] = kk * jnp.exp(g_ex)
        nat_scr[rr, _RT] = r * jnp.exp(g_in)
        nat_scr[rr, _KT] = kd * e_inv
        nat_scr[rr, _BT] = bb * e_inv
        nat_scr[rr, _KH] = kd * e_hat
        nat_scr[rr, _BH] = bb * e_hat
        nat_scr[rr, _V] = v
        pc_scr[rr] = jnp.exp(g_c)

    for rr in range(R):
        prepare(rr)

    lane_head = lax.broadcasted_iota(jnp.int32, (C, GROUP_W), 1) // HEAD
    ii = lax.broadcasted_iota(jnp.int32, (GROUP_W, GROUP_W), 0)
    jj = lax.broadcasted_iota(jnp.int32, (GROUP_W, GROUP_W), 1)
    eye = ii == jj

    def stacked(t):
        return jnp.concatenate([jnp.where(lane_head == j, t, 0.0) for j in range(GROUP_HEADS)], axis=0)

    def collapse(t):
        return t[0:C] + t[C:2 * C] + t[2 * C:3 * C] + t[3 * C:4 * C]

    G = range(R * N_GROUPS)

    def lanes(q):
        return slice((q % N_GROUPS) * GROUP_W, (q % N_GROUPS + 1) * GROUP_W)

    def nat(i, q):
        return nat_scr[q // N_GROUPS, i, :, lanes(q)]

    x_kap = [stacked(nat(_KAP, g)) for g in G]
    x_v = [_bf(stacked(nat(_V, g))) for g in G]
    x_bk = [jnp.concatenate([_bf(stacked(nat(_BT, g))), _bf(stacked(nat(_KT, g)))], axis=0) for g in G]
    kr = [_bf(jnp.concatenate([nat(_KAP, g), nat(_RT, g)], axis=0)) for g in G]
    akr = [_dot_nt(kr[g], x_bk[g]) for g in G]

    def tiled(t):
        return jnp.concatenate([t] * GROUP_HEADS, axis=0)

    same_head = (ii // C) == (jj // C)

    def block_diag(side):
        t = tiled(_bf(side))
        return jnp.where(same_head, t, jnp.zeros_like(t))

    msl_c = collapse(msl_ref[0]) > 0.5
    minc_c = collapse(minc_ref[0]) > 0.5
    n_side = [jnp.where(msl_c, akr[g][:C, :GROUP_W], 0.0) for g in G]
    a_kk = [_bf(jnp.where(msl_c, akr[g][:C, GROUP_W:], 0.0)) for g in G]
    a_rb = [_bf(jnp.where(minc_c, akr[g][C:, :GROUP_W], 0.0)) for g in G]
    a_rk = [_bf(jnp.where(minc_c, akr[g][C:, GROUP_W:], 0.0)) for g in G]
    g0 = [stacked(_dot(a_kk[g], x_v[g])) for g in G]
    y0 = [_dot(a_rk[g], x_v[g]) for g in G]
    eye_side = collapse(jnp.where(eye, 1.0, 0.0))
    p_inv = [eye_side - n_side[g] for g in G]
    m_pow = [_dot(_bf(n_side[g]), block_diag(n_side[g])) for g in G]
    for _ in range(4):
        both = [_dot(_bf(jnp.concatenate([m_pow[g], p_inv[g]], axis=0)), block_diag(m_pow[g])) for g in G]
        m_pow = [both[g][:C] for g in G]
        p_inv = [p_inv[g] + both[g][C:] for g in G]
    p_inv = [p_inv[g] + _dot(_bf(p_inv[g]), block_diag(m_pow[g])) for g in G]
    w_nat = [_dot(_bf(p_inv[g]), jnp.concatenate([_bf(x_kap[g]), _bf(g0[g])], axis=1)) for g in G]
    w12 = [_bf(jnp.concatenate([stacked(w_nat[g][:, :GROUP_W]), stacked(w_nat[g][:, GROUP_W:])], axis=1))
           for g in G]
    aw = [_dot(a_rb[g], w12[g]) for g in G]
    def heads_transposed(t):
        tt = t.T
        return jnp.concatenate([tt[j * HEAD:(j + 1) * HEAD] for j in range(GROUP_HEADS)], axis=1)

    bw = [_dot(_bf(heads_transposed(nat(_BH, g))), w12[g]) for g in G]
    kv = [_dot(_bf(heads_transposed(nat(_KH, g))), x_v[g]) for g in G]
    h_prev = [H_scr[g // N_GROUPS, g % N_GROUPS] for g in G]
    ys = []
    h_new = []
    for g in G:
        h_hi = block_diag(h_prev[g])
        h_lo = block_diag(h_prev[g] - _bf(h_prev[g]).astype(F32))
        qb = _bf(nat(_RT, g) - aw[g][:, :GROUP_W])
        m_side = eye_side * pc_scr[g // N_GROUPS, :, lanes(g)] - bw[g][:, :GROUP_W]
        m_hi, m_lo = _split(m_side)
        by_hi = _dot(jnp.concatenate([m_hi, m_lo, qb], axis=0), h_hi)
        by_lo = _dot(jnp.concatenate([m_hi, qb], axis=0), h_lo)
        ys.append(by_hi[2 * C:] + by_lo[C:] + y0[g] - aw[g][:, GROUP_W:])
        h_new.append(by_hi[:C] + by_hi[C:2 * C] + by_lo[:C] + kv[g] - bw[g][:, GROUP_W:])
    for rr in range(R):
        y_ref[0, rr] = jnp.concatenate(ys[rr * N_GROUPS:(rr + 1) * N_GROUPS], axis=1)
    for g in G:
        H_scr[g // N_GROUPS, g % N_GROUPS] = h_new[g]

    @pl.when(c == n_chunks - 1)
    def _():
        hT_ref[0] = H_scr[...]


def _rwkv(z, pr, h0):
    B, L, N = z.shape
    C = CHUNK
    nC = L // C
    D = D_MODEL

    def cidx(d, c):
        return c + d * (nC - 1 - 2 * c)

    zmap = lambda d, b, c: (b, cidx(d, c), 0)
    const2 = lambda d, b, c: (0, 0)
    dir3 = lambda d, b, c: (d, 0, 0)
    omap = lambda d, b, c: (d, b, cidx(d, c), 0)
    smap = lambda d, b, c: (d, b, 0, 0, 0)
    R = RWKV_ROWS_PER_STEP if B % RWKV_ROWS_PER_STEP == 0 else 1
    kern = functools.partial(_rwkv_kernel, n_chunks=nC)
    return pl.pallas_call(
        kern,
        grid=(2, B // R, nC),
        in_specs=[pl.BlockSpec((R, C, N), zmap),
                  pl.BlockSpec((1, 1, D), dir3),
                  pl.BlockSpec((1, 2 * LORA_W, D), dir3),
                  pl.BlockSpec((1, 2 * LORA_W, D), dir3),
                  pl.BlockSpec((1, 1, D), dir3),
                  pl.BlockSpec((1, 2 * LORA_A, D), dir3),
                  pl.BlockSpec((1, 2 * LORA_A, D), dir3),
                  pl.BlockSpec((1, D), const2),
                  pl.BlockSpec((1, D), const2),
                  pl.BlockSpec((1, D), const2),
                  pl.BlockSpec((D, 128), const2),
                  pl.BlockSpec((128, D), const2),
                  pl.BlockSpec((1, C, C), dir3),
                  pl.BlockSpec((1, GROUP_W, GROUP_W), dir3),
                  pl.BlockSpec((1, GROUP_W, GROUP_W), dir3),
                  pl.BlockSpec((1, R, N_GROUPS, HEAD, GROUP_W), smap)],
        out_specs=[pl.BlockSpec((1, R, C, D), omap),
                   pl.BlockSpec((1, R, C, D), omap),
                   pl.BlockSpec((1, R, C, LORA_G_PAD), omap),
                   pl.BlockSpec((1, R, N_GROUPS, HEAD, GROUP_W), smap)],
        out_shape=[jax.ShapeDtypeStruct((2, B, L, D), F32),
                   jax.ShapeDtypeStruct((2, B, L, D), F32),
                   jax.ShapeDtypeStruct((2, B, L, LORA_G_PAD), F32),
                   jax.ShapeDtypeStruct((2, B, N_GROUPS, HEAD, GROUP_W), F32)],
        scratch_shapes=[pltpu.VMEM((R, N_GROUPS, HEAD, GROUP_W), F32),
                        pltpu.VMEM((R, 7, C, D), F32),
                        pltpu.VMEM((R, 1, D), F32)],
        compiler_params=_cp("arbitrary", "arbitrary", "arbitrary"),
        name="rwkv",
    )(z, pr["w0"], *pr["w2p"], pr["a0"], *pr["a2p"], pr["k_k"], pr["k_a"], pr["r_k"],
      pr["hs"], pr["hb"], pr["tri"], pr["msl"], pr["minc"], h0)


def _conv_kernel(za_ref, zb_ref, w_ref, o_ref, upad, *, L, stride, rows_per_step):
    pad = CONV_HALF * stride
    TC = za_ref.shape[-1]
    upad[0:pad, :] = jnp.zeros((pad, TC), F32)
    upad[pad + L:pad + L + pad, :] = jnp.zeros((pad, TC), F32)
    upad[pad:pad + L, :] = za_ref[0] * _sigmoid(zb_ref[0])
    RB = rows_per_step

    def block(r0):
        acc = jnp.zeros((RB, TC), F32)
        for j in range(CONV_K):
            acc = acc + w_ref[j:j + 1, :] * upad[pl.ds(r0 + j * stride, RB), :]
        o_ref[0, pl.ds(r0, RB), :] = acc

    if stride % 8 == 0:
        def body(i, carry):
            block(pl.multiple_of(i * RB, RB))
            return carry
        lax.fori_loop(0, L // RB, body, 0)
    else:
        for i in range(L // RB):
            block(i * RB)


def _conv(zcg, conv_w, stride):
    B, L, _ = zcg.shape
    D = D_MODEL
    TC = 128
    nct = D // TC
    wpad = jnp.pad(conv_w, ((0, 32 - CONV_K), (0, 0)))
    RB = min(L, 128)
    kern = functools.partial(_conv_kernel, L=L, stride=stride, rows_per_step=RB)
    return pl.pallas_call(
        kern,
        grid=(B, nct),
        in_specs=[pl.BlockSpec((1, L, TC), lambda b, j: (b, 0, j)),
                  pl.BlockSpec((1, L, TC), lambda b, j: (b, 0, j + nct)),
                  pl.BlockSpec((32, TC), lambda b, j: (0, j))],
        out_specs=pl.BlockSpec((1, L, TC), lambda b, j: (b, 0, j)),
        out_shape=jax.ShapeDtypeStruct((B, L, D), F32),
        scratch_shapes=[pltpu.VMEM((L + 2 * CONV_HALF * stride, TC), F32)],
        compiler_params=_cp("parallel", "parallel"),
        name="conv",
    )(zcg, zcg, wpad)


def _post_kernel(x_ref, yf_ref, yb_ref, bf_ref, bb_ref, gs_ref, cv_ref, zg_ref,
                 lng_ref, lnb_ref, g2_ref, woa_ref, cng_ref, cnb_ref, wob_ref, gb_ref, wout_ref,
                 gt_ref, n2g_ref, sh2_ref, sc2_ref, hs_ref, hb_ref, xo_ref, hn_ref):
    D = D_MODEL
    hs = hs_ref[...]
    hb = hb_ref[...]

    def head_mean(t):
        return _head_sum(t, hs, hb) * (1.0 / HEAD)

    o = yf_ref[0, 0] + yb_ref[0, 0]
    oc = o - head_mean(o)
    on = oc * lax.rsqrt(head_mean(oc * oc) + GN_EPS) * lng_ref[...] + lnb_ref[...]
    on = on + bf_ref[0, 0] + bb_ref[0, 0]
    gate = _dot(_bf(gs_ref[0, 0]), g2_ref[...])
    y_a = _dot(_bf(on * gate), woa_ref[...])

    cv = cv_ref[0]
    cm = jnp.mean(cv, axis=-1, keepdims=True)
    cc = cv - cm
    cn = cc * lax.rsqrt(jnp.mean(cc * cc, axis=-1, keepdims=True) + LN_EPS) * cng_ref[...] + cnb_ref[...]
    y_b = _dot(_bf(cn * _sigmoid(cn)), wob_ref[...])

    gates = _sigmoid(zg_ref[0] + gb_ref[...])
    m = gates[:, :D] * y_a + gates[:, D:] * y_b
    xn = x_ref[0] + gt_ref[0] * _dot(_bf(m), wout_ref[...])
    xo_ref[0] = xn
    hn_ref[0] = _bf(_rms(xn, n2g_ref[...]) * (1.0 + sc2_ref[0]) + sh2_ref[0])


def _post(x, y, bv, gs, cv, zcg, pr, gt1, sh2, sc2):
    B, L, D = x.shape
    TM = min(L, 256)
    tok = lambda b, i: (b, i, 0)
    fwd = lambda b, i: (0, b, i, 0)
    bwd = lambda b, i: (1, b, i, 0)
    cst = lambda b, i: (0, 0)
    per_b = lambda b, i: (b, 0, 0)
    row = pl.BlockSpec((1, D), cst)
    mat = pl.BlockSpec((D, D), cst)
    return pl.pallas_call(
        _post_kernel,
        grid=(B, L // TM),
        in_specs=[pl.BlockSpec((1, TM, D), tok),
                  pl.BlockSpec((1, 1, TM, D), fwd), pl.BlockSpec((1, 1, TM, D), bwd),
                  pl.BlockSpec((1, 1, TM, D), fwd), pl.BlockSpec((1, 1, TM, D), bwd),
                  pl.BlockSpec((1, 1, TM, LORA_G_PAD), fwd),
                  pl.BlockSpec((1, TM, D), tok),
                  pl.BlockSpec((1, TM, 2 * D), lambda b, i: (b, i, 1)),
                  row, row, pl.BlockSpec((LORA_G_PAD, D), cst), mat,
                  row, row, mat, pl.BlockSpec((1, 2 * D), cst), mat,
                  pl.BlockSpec((1, 1, D), per_b), row,
                  pl.BlockSpec((1, 1, D), per_b), pl.BlockSpec((1, 1, D), per_b),
                  pl.BlockSpec((D, 128), cst), pl.BlockSpec((128, D), cst)],
        out_specs=[pl.BlockSpec((1, TM, D), tok), pl.BlockSpec((1, TM, D), tok)],
        out_shape=[jax.ShapeDtypeStruct((B, L, D), F32), jax.ShapeDtypeStruct((B, L, D), BF16)],
        compiler_params=_cp("parallel", "parallel"),
        name="post",
    )(x, y, y, bv, bv, gs, cv, zcg,
      pr["lnx_g"], pr["lnx_b"], pr["g2p"], pr["w_oA"], pr["cnorm_g"], pr["cnorm_b"], pr["w_oB"],
      pr["gate_b"], pr["w_out"], gt1, pr["norm2_g"], sh2, sc2, pr["hs"], pr["hb"])


def _top16(s, rowid):
    rank = jnp.full(s.shape, NOT_SELECTED, F32)
    cur = s
    vals = []
    for r in range(PEER_TOPK):
        m = jnp.max(cur, axis=0, keepdims=True)
        idx = jnp.min(jnp.where(cur == m, rowid, 1e9), axis=0, keepdims=True)
        sel = rowid == idx
        rank = jnp.where(sel, float(r), rank)
        cur = jnp.where(sel, NEG_INF, cur)
        vals.append(m)
    return rank, vals


def _top16_untied(arrays):
    n = range(len(arrays))
    rank = [jnp.full(s.shape, NOT_SELECTED, F32) for s in arrays]
    cur = list(arrays)
    vals = [[] for _ in n]
    for r in range(PEER_TOPK):
        m = [jnp.max(cur[i], axis=0, keepdims=True) for i in n]
        sel = [cur[i] == m[i] for i in n]
        rank = [jnp.where(sel[i], float(r), rank[i]) for i in n]
        cur = [jnp.where(sel[i], NEG_INF, cur[i]) for i in n]
        for i in n:
            vals[i].append(m[i])
    n_ranked = [jnp.sum(jnp.where(rank[i] < float(PEER_TOPK), 1.0, 0.0), axis=0, keepdims=True) for i in n]
    return rank, vals, n_ranked


def _peer_prep_kernel(h_ref, wq_ref, keys_ref, r2_ref, na_ref, e1_ref, e2_ref, q_scr, rk_scr, vl_scr):
    TM = h_ref.shape[0]
    K = PEER_TOPK
    q_scr[...] = _bf(_dot(h_ref[...], wq_ref[...]))
    rowid = lax.broadcasted_iota(jnp.int32, (PEER_NKEYS, TM), 0).astype(F32)
    kaid = lax.broadcasted_iota(jnp.int32, (K, TM), 0).astype(F32)

    def head(h, carry):
        off = pl.multiple_of(h * 2 * PEER_HALF, 2 * PEER_HALF)
        s1 = _dot_nt(keys_ref[h, 0], q_scr[:, pl.ds(off, PEER_HALF)])
        s2 = _dot_nt(keys_ref[h, 1], q_scr[:, pl.ds(off + PEER_HALF, PEER_HALF)])
        ranks, valss, n_ranked = _top16_untied([s1, s2])
        for half in range(2):
            rk_scr[half] = ranks[half]
            vl_scr[half] = jnp.concatenate(valss[half], axis=0)
        n_max = jnp.maximum(n_ranked[0], n_ranked[1])

        @pl.when(jnp.max(n_max) > float(K))
        def _():
            for half, s in ((0, s1), (1, s2)):
                rank, vals = _top16(s, rowid)
                rk_scr[half] = rank
                vl_scr[half] = jnp.concatenate(vals, axis=0)

        rank1 = rk_scr[0]
        rank2 = rk_scr[1]
        v1 = vl_scr[0]
        vals1 = [v1[r:r + 1] for r in range(K)]
        vals2 = [vl_scr[1, r:r + 1, :] for r in range(K)]
        taken = jnp.zeros((K, TM), F32)
        front = v1 + vals2[0]
        for _ in range(K):
            m = jnp.max(front, axis=0, keepdims=True)
            idx = jnp.min(jnp.where(front == m, kaid, 1e9), axis=0, keepdims=True)
            sel = kaid == idx
            taken = taken + jnp.where(sel, 1.0, 0.0)
            nxt = jnp.full((K, TM), NEG_INF, F32)
            for kb in range(1, K):
                nxt = jnp.where(taken == float(kb), vals2[kb], nxt)
            front = jnp.where(sel, v1 + nxt, front)
        e1k = jnp.exp(v1 - vals1[0])
        pref = jnp.zeros((1, TM), F32)
        zrow = jnp.zeros((K, TM), F32)
        for kb in range(K):
            pref = pref + jnp.exp(vals2[kb] - vals2[0])
            zrow = jnp.where(taken == float(kb + 1), pref, zrow)
        z = jnp.sum(e1k * zrow, axis=0, keepdims=True)
        na = jnp.zeros((PEER_NKEYS, TM), F32)
        for ka in range(K):
            na = jnp.where(rank1 == float(ka), taken[ka:ka + 1], na)
        r2_ref[h] = _bf(rank2)
        na_ref[h] = na
        e1_ref[h] = jnp.where(rank1 < float(K), jnp.exp(s1 - vals1[0]) / z, 0.0)
        e2_ref[h] = _bf(jnp.where(rank2 < float(K), jnp.exp(s2 - vals2[0]), 0.0))
        return carry

    lax.fori_loop(0, PEER_HEADS, head, 0)


def _peer_prep(hn, wq, keys, TM):
    T, D = hn.shape
    Q = wq.shape[1]
    shp = jax.ShapeDtypeStruct((PEER_HEADS, PEER_NKEYS, T), F32)
    shp_bf = jax.ShapeDtypeStruct((PEER_HEADS, PEER_NKEYS, T), BF16)
    ospec = pl.BlockSpec((PEER_HEADS, PEER_NKEYS, TM), lambda i: (0, 0, i))
    return pl.pallas_call(
        _peer_prep_kernel,
        grid=(T // TM,),
        in_specs=[pl.BlockSpec((TM, D), lambda i: (i, 0)),
                  pl.BlockSpec((D, Q), lambda i: (0, 0)),
                  pl.BlockSpec((PEER_HEADS, 2, PEER_NKEYS, PEER_HALF), lambda i: (0, 0, 0, 0))],
        out_specs=[ospec, ospec, ospec, ospec],
        out_shape=[shp_bf, shp, shp, shp_bf],
        scratch_shapes=[pltpu.VMEM((TM, Q), BF16),
                        pltpu.VMEM((2, PEER_NKEYS, TM), F32), pltpu.VMEM((2, PEER_TOPK, TM), F32)],
        compiler_params=_cp("parallel"),
        name="peer_prep",
    )(hn, wq, keys)


def _peer_dense_kernel(h_ref, r2_ref, na_ref, e1_ref, e2_ref, u_ref, vt_ref, x_ref, gt_ref, fg_ref,
                       o_ref, acc_t, ht_scr, s0_scr, s1_scr, aw0_scr, aw1_scr, *, block, n_blocks, final_norm):
    jj = pl.program_id(1)
    TM = h_ref.shape[0]
    SLAB = PEER_NKEYS
    EB = block

    @pl.when(jj == 0)
    def _():
        acc_t[...] = jnp.zeros_like(acc_t)
        s1_scr[...] = jnp.zeros_like(s1_scr)
        aw0_scr[...] = jnp.zeros_like(aw0_scr)
        ht_scr[...] = _bf(h_ref[...].astype(F32).T)

    zero_bf = jnp.zeros((SLAB, TM), BF16)

    def row_tile(ref, h, a):
        t = _bf(jnp.broadcast_to(ref[h, pl.ds(a, 1), :], (16, TM)))
        return jnp.concatenate([t] * (SLAB // 16), axis=0)

    def vpu_stage(k, s_ref, aw_ref):
        live = jnp.logical_and(k >= 0, k < n_blocks)
        kc = jnp.clip(k, 0, n_blocks - 1)
        for p in range(EB // (2 * SLAB)):
            lo = p * 2 * SLAB
            s = s_ref[lo:lo + 2 * SLAB, :]
            act = 0.5 * s * (1.0 + lax.erf(s * SQRT_HALF))
            weights = []
            for half in range(2):
                a = kc * (EB // SLAB) + p * 2 + half
                w = jnp.zeros((SLAB, TM), BF16)
                for h in range(PEER_HEADS):
                    chosen = r2_ref[h] < row_tile(na_ref, h, a)
                    w = w + jnp.where(chosen, e2_ref[h], zero_bf) * row_tile(e1_ref, h, a)
                weights.append(w)
            aw = act * jnp.concatenate(weights, axis=0)
            aw_ref[lo:lo + 2 * SLAB, :] = jnp.where(live, aw, jnp.zeros_like(aw))

    out0 = _dot(vt_ref[:, 0:EB], aw0_scr[...])
    vpu_stage(2 * jj - 1, s1_scr, aw1_scr)
    s0_scr[...] = _bf(_dot(u_ref[0:EB, :], ht_scr[...]))
    out1 = _dot(vt_ref[:, EB:2 * EB], aw1_scr[...])
    s1_scr[...] = _bf(_dot(u_ref[EB:2 * EB, :], ht_scr[...]))
    vpu_stage(2 * jj, s0_scr, aw0_scr)
    acc_t[...] += out0 + out1

    @pl.when(jj == pl.num_programs(1) - 1)
    def _():
        xn = x_ref[...] + gt_ref[0] * acc_t[...].T
        o_ref[...] = _rms(xn, fg_ref[...]) if final_norm else xn


def _peer_dense(x, hn, sel, u, vt, gt2, TM, tiles_per_batch, final_g):
    T, D = x.shape
    E = u.shape[0]
    EB = 1024
    final_norm = final_g is not None
    fg = (final_g if final_norm else jnp.ones((D,), F32)).reshape(1, D)
    sspec = pl.BlockSpec((PEER_HEADS, PEER_NKEYS, TM), lambda i, e: (0, 0, i))
    nE = E // EB
    nP = nE // 2
    kern = functools.partial(_peer_dense_kernel, block=EB, n_blocks=nE, final_norm=final_norm)
    return pl.pallas_call(
        kern,
        grid=(T // TM, nP + 1),
        in_specs=[pl.BlockSpec((TM, D), lambda i, e: (i, 0)),
                  sspec, sspec, sspec, sspec,
                  pl.BlockSpec((2 * EB, D), lambda i, e: (jnp.minimum(e, nP - 1), 0)),
                  pl.BlockSpec((D, 2 * EB), lambda i, e: (0, jnp.maximum(e - 1, 0))),
                  pl.BlockSpec((TM, D), lambda i, e: (i, 0)),
                  pl.BlockSpec((1, 1, D), lambda i, e: (i // tiles_per_batch, 0, 0)),
                  pl.BlockSpec((1, D), lambda i, e: (0, 0))],
        out_specs=pl.BlockSpec((TM, D), lambda i, e: (i, 0)),
        out_shape=jax.ShapeDtypeStruct((T, D), F32),
        scratch_shapes=[pltpu.VMEM((D, TM), F32), pltpu.VMEM((D, TM), BF16),
                        pltpu.VMEM((EB, TM), BF16), pltpu.VMEM((EB, TM), BF16),
                        pltpu.VMEM((EB, TM), BF16), pltpu.VMEM((EB, TM), BF16)],
        compiler_params=_cp("parallel", "arbitrary"),
        name="peer_dense",
    )(hn, *sel, u, vt, x, gt2, fg)


def _direction_masks():
    t = jnp.arange(CHUNK)
    le = (t[None, :] <= t[:, None]).astype(F32)
    tri = jnp.stack([le, le.T])
    i = jnp.arange(GROUP_W)
    same = (i[:, None] // CHUNK) == (i[None, :] // CHUNK)
    ti, tj = i[:, None] % CHUNK, i[None, :] % CHUNK
    msl = jnp.stack([same & (tj < ti), same & (tj > ti)]).astype(F32)
    minc = jnp.stack([same & (tj <= ti), same & (tj >= ti)]).astype(F32)
    return tri, msl, minc


def _layer_params(l, w_in, shift_mu, w0, w2, a0, a2, g2, k_k, k_a, r_k, lnx_g, lnx_b, w_oA, conv_w,
                  cnorm_g, cnorm_b, w_oB, gate_b, w_out, norm2_g):
    D = D_MODEL
    row = lambda t: t.reshape(1, -1)
    zeros = jnp.zeros((LORA_W, D), F32)
    tri, msl, minc = _direction_masks()
    hs = (jnp.arange(D)[:, None] // HEAD == jnp.arange(128)[None, :]).astype(F32)
    pad_cols = P_RWKV_PAD - P_RWKV
    return dict(
        w_rk=_bf(jnp.pad(w_in[l][:, :P_RWKV], ((0, 0), (0, pad_cols)))),
        w_cg=_bf(w_in[l][:, P_RWKV:]),
        mu=jnp.pad(shift_mu[l], ((0, 0), (0, pad_cols))),
        w0=w0[l].reshape(2, 1, D), a0=a0[l].reshape(2, 1, D),
        w2p=_split(jnp.stack([jnp.concatenate([w2[l, 0], zeros]), jnp.concatenate([zeros, w2[l, 1]])])),
        a2p=_split(jnp.stack([jnp.concatenate([a2[l, 0], zeros]), jnp.concatenate([zeros, a2[l, 1]])])),
        g2p=_bf(jnp.pad(g2[l], ((0, LORA_G_PAD - LORA_G), (0, 0)))),
        k_k=row(k_k[l]), k_a=row(k_a[l]), r_k=row(r_k[l]),
        lnx_g=row(lnx_g[l]), lnx_b=row(lnx_b[l]), w_oA=_bf(w_oA[l]),
        conv_w=conv_w[l], cnorm_g=row(cnorm_g[l]), cnorm_b=row(cnorm_b[l]), w_oB=_bf(w_oB[l]),
        gate_b=row(gate_b[l]), w_out=_bf(w_out[l]), norm2_g=row(norm2_g[l]),
        hs=_bf(hs), hb=_bf(hs.T), tri=_bf(tri), msl=msl, minc=minc)


def _mixer(x, mod, pr, norm1_g, stride, h0, emit):
    B, L, D = x.shape
    sh1, sc1, gt1, sh2, sc2 = (mod[:, i:i + 1, :] for i in range(5))
    z_rk = _proj_shift(x, sh1, sc1, norm1_g, pr["w_rk"], pr["mu"])
    y, bv, gs, h_t = _rwkv(z_rk, pr, h0)
    if not emit:
        return None, None, h_t
    z_cg = _proj(x, sh1, sc1, norm1_g, pr["w_cg"])
    cv = _conv(z_cg, pr["conv_w"], stride)
    xn, hn = _post(x, y, bv, gs, cv, z_cg, pr, gt1, sh2, sc2)
    return xn, hn, h_t


def _peer(x, hn, wq, keys, u, v, gt2, final_g=None):
    B, L, D = x.shape
    T = B * L
    TM = min(L, 256)
    sel = _peer_prep(hn.reshape(T, D), wq, keys, TM)
    out = _peer_dense(x.reshape(T, D), hn.reshape(T, D), sel, u, v, gt2, TM, L // TM, final_g)
    return out.reshape(B, L, D)


def kernel(x, c, ctx, c_ctx, ada_w, ada_b, norm1_g, norm2_g, w_in, shift_mu, w0, w2, a0, a2, g2, k_k, k_a, r_k, lnx_g, lnx_b, w_oA, conv_w, cnorm_g, cnorm_b, w_oB, gate_b, w_out, w_q, sub_keys, peer_u, peer_v, final_g):
    B, L, D = x.shape
    depth = ada_w.shape[0]
    xc = ctx
    n_rows = -(-(B + 1) // 8) * 8
    c_rows = jnp.pad(jnp.concatenate([c, c_ctx[None, :]], axis=0), ((0, n_rows - B - 1), (0, 0)))
    zero_state = jnp.zeros((2, B, N_GROUPS, HEAD, GROUP_W), F32)
    for l in range(depth):
        last = l == depth - 1
        pr = _layer_params(l, w_in, shift_mu, w0, w2, a0, a2, g2, k_k, k_a, r_k, lnx_g, lnx_b, w_oA,
                           conv_w, cnorm_g, cnorm_b, w_oB, gate_b, w_out, norm2_g)
        mod_all = _modulation(c_rows, ada_w[l], ada_b[l])
        mod = mod_all[:B].reshape(B, 6, D)
        modc = jnp.broadcast_to(mod_all[B].reshape(1, 6, D), (B, 6, D))
        wq = _bf(w_q[l])
        keys = _bf(sub_keys[l])
        u = _bf(peer_u[l])
        v = _bf(peer_v[l]).T

        xc_new, hnc, ctx_states = _mixer(xc, modc, pr, norm1_g[l], 1, zero_state, emit=not last)
        xn, hn, _ = _mixer(x, mod, pr, norm1_g[l], GRID_W, ctx_states, emit=True)
        x = _peer(xn, hn, wq, keys, u, v, mod[:, 5:6, :], final_g if last else None)
        if not last:
            xc = _peer(xc_new, hnc, wq, keys, u, v, modc[:, 5:6, :])
    return x
```

```python
import functools

import jax
import jax.numpy as jnp
from jax import lax
from jax.experimental import pallas as pl
from jax.experimental.pallas import tpu as pltpu

F32 = jnp.float32
BF16 = jnp.bfloat16
HI = lax.Precision.HIGHEST

D_MODEL = 1024
HEAD = 64
HEADS = D_MODEL // HEAD
GROUP_HEADS = 4
GROUP_W = GROUP_HEADS * HEAD
N_GROUPS = HEADS // GROUP_HEADS
CHUNK = 64
assert CHUNK == HEAD
RWKV_ROWS_PER_STEP = 4
LORA_W = 64
LORA_A = 64
LORA_G = 160
LORA_G_PAD = 256
P_RWKV = 3 * D_MODEL + 2 * LORA_W + 2 * LORA_A + LORA_G
P_RWKV_PAD = 3 * D_MODEL + 2 * LORA_W + 2 * LORA_A + LORA_G_PAD
COL_W1 = 3 * D_MODEL
COL_A1 = COL_W1 + 2 * LORA_W
COL_G1 = COL_A1 + 2 * LORA_A
CONV_K = 31
CONV_HALF = CONV_K // 2
GRID_W = 64
PEER_HEADS = 8
PEER_NKEYS = 128
PEER_HALF = 128
PEER_TOPK = 16
NORM_EPS = 1e-6
LN_EPS = 1e-5
GN_EPS = HEAD * 1e-5
VMEM_LIMIT = 56 * 1024 * 1024
NOT_SELECTED = 99.0
NEG_INF = float("-inf")
SQRT_HALF = 0.7071067811865476
EXP_MINUS_HALF = 0.6065306597126334


def _cp(*sem):
    return pltpu.CompilerParams(dimension_semantics=sem, vmem_limit_bytes=VMEM_LIMIT)


def _dot(a, b):
    return jnp.dot(a, b, preferred_element_type=F32)


def _dot_hi(a, b):
    return jnp.dot(a, b, precision=HI, preferred_element_type=F32)


def _dot_nt(a, b):
    return lax.dot_general(a, b, (((1,), (1,)), ((), ())), preferred_element_type=F32)


def _dot_tn(a, b):
    return lax.dot_general(a, b, (((0,), (0,)), ((), ())), preferred_element_type=F32)


def _bf(a):
    return a.astype(BF16)


def _split(a):
    hi = a.astype(BF16)
    return hi, (a - hi.astype(F32)).astype(BF16)


def _dot_split_lhs(a, b_bf):
    hi, lo = _split(a)
    return _dot(hi, b_bf) + _dot(lo, b_bf)


def _dot_split(a, b_hi, b_lo):
    hi, lo = _split(a)
    return _dot(hi, b_hi) + _dot(hi, b_lo) + _dot(lo, b_hi)


def _head_sum(t, hs_bf, hb_bf):
    return _dot_split_lhs(_dot_split_lhs(t, hs_bf), hb_bf)


def _sigmoid(x):
    return 1.0 / (1.0 + jnp.exp(-x))


def _rms(x, g):
    return x * lax.rsqrt(jnp.mean(x * x, axis=-1, keepdims=True) + NORM_EPS) * g


def _mod_kernel(c_ref, w_ref, b_ref, o_ref):
    c = c_ref[...]
    o_ref[...] = _dot_hi(c * _sigmoid(c), w_ref[...]) + b_ref[...]


def _modulation(c_rows, ada_w, ada_b):
    R, D = c_rows.shape
    N = ada_w.shape[1]
    TN = 512
    return pl.pallas_call(
        _mod_kernel,
        grid=(N // TN,),
        in_specs=[pl.BlockSpec((R, D), lambda j: (0, 0)),
                  pl.BlockSpec((D, TN), lambda j: (0, j)),
                  pl.BlockSpec((1, TN), lambda j: (0, j))],
        out_specs=pl.BlockSpec((R, TN), lambda j: (0, j)),
        out_shape=jax.ShapeDtypeStruct((R, N), F32),
        compiler_params=_cp("arbitrary"),
        name="modulation",
    )(c_rows, ada_w, ada_b.reshape(1, N))


def _proj_kernel(x_ref, sh_ref, sc_ref, g_ref, w_ref, o_ref):
    h = _rms(x_ref[0], g_ref[...]) * (1.0 + sc_ref[0]) + sh_ref[0]
    o_ref[0] = _dot(_bf(h), w_ref[...])


def _proj(x, sh, sc, g, w):
    B, L, D = x.shape
    N = w.shape[1]
    TM = min(L, 256)
    return pl.pallas_call(
        _proj_kernel,
        grid=(B, L // TM),
        in_specs=[pl.BlockSpec((1, TM, D), lambda b, i: (b, i, 0)),
                  pl.BlockSpec((1, 1, D), lambda b, i: (b, 0, 0)),
                  pl.BlockSpec((1, 1, D), lambda b, i: (b, 0, 0)),
                  pl.BlockSpec((1, D), lambda b, i: (0, 0)),
                  pl.BlockSpec((D, N), lambda b, i: (0, 0))],
        out_specs=pl.BlockSpec((1, TM, N), lambda b, i: (b, i, 0)),
        out_shape=jax.ShapeDtypeStruct((B, L, N), F32),
        compiler_params=_cp("parallel", "parallel"),
        name="proj",
    )(x, sh, sc, g.reshape(1, D), w)


def _proj_shift_kernel(x_ref, xp_ref, xn_ref, sh_ref, sc_ref, g_ref, w_ref, mu_ref, o_ref):
    i = pl.program_id(1)
    TM = x_ref.shape[1]
    xs = jnp.concatenate([xp_ref[0], x_ref[0], xn_ref[0]], axis=0)
    h = _rms(xs, g_ref[...]) * (1.0 + sc_ref[0]) + sh_ref[0]
    z = _dot(_bf(h), w_ref[...])
    row = lax.broadcasted_iota(jnp.int32, (TM, 1), 0)
    at_start = jnp.logical_and(row == 0, i == 0)
    at_end = jnp.logical_and(row == TM - 1, i == pl.num_programs(1) - 1)
    zp = jnp.where(at_start, 0.0, z[7:7 + TM])
    zn = jnp.where(at_end, 0.0, z[9:9 + TM])
    m0 = mu_ref[0:1, :]
    m1 = mu_ref[1:2, :]
    o_ref[0] = z[8:8 + TM] * (1.0 - m0 - m1) + m0 * zp + m1 * zn


def _proj_shift(x, sh, sc, g, w, mu):
    B, L, D = x.shape
    N = w.shape[1]
    TM = min(L, 256)
    nb8 = L // 8
    return pl.pallas_call(
        _proj_shift_kernel,
        grid=(B, L // TM),
        in_specs=[pl.BlockSpec((1, TM, D), lambda b, i: (b, i, 0)),
                  pl.BlockSpec((1, 8, D), lambda b, i: (b, jnp.maximum(i * (TM // 8) - 1, 0), 0)),
                  pl.BlockSpec((1, 8, D), lambda b, i: (b, jnp.minimum((i + 1) * (TM // 8), nb8 - 1), 0)),
                  pl.BlockSpec((1, 1, D), lambda b, i: (b, 0, 0)),
                  pl.BlockSpec((1, 1, D), lambda b, i: (b, 0, 0)),
                  pl.BlockSpec((1, D), lambda b, i: (0, 0)),
                  pl.BlockSpec((D, N), lambda b, i: (0, 0)),
                  pl.BlockSpec((2, N), lambda b, i: (0, 0))],
        out_specs=pl.BlockSpec((1, TM, N), lambda b, i: (b, i, 0)),
        out_shape=jax.ShapeDtypeStruct((B, L, N), F32),
        compiler_params=_cp("parallel", "parallel"),
        name="proj_shift",
    )(x, x, x, sh, sc, g.reshape(1, D), w, mu)


_KAP, _RT, _KT, _BT, _KH, _BH, _V = range(7)


def _rwkv_kernel(z_ref, w0_ref, w2h_ref, w2l_ref, a0_ref, a2h_ref, a2l_ref,
                 kk_ref, ka_ref, rk_ref, hs_ref, hb_ref, tri_ref, msl_ref, minc_ref, h0_ref,
                 y_ref, bv_ref, gs_ref, hT_ref, H_scr, nat_scr, pc_scr, *, n_chunks):
    d = pl.program_id(0)
    c = pl.program_id(2)
    cc = jnp.where(d == 0, c, n_chunks - 1 - c)
    C = CHUNK

    R = z_ref.shape[0]

    @pl.when(c == 0)
    def _():
        H_scr[...] = h0_ref[0]

    hs = hs_ref[...]
    hb = hb_ref[...]

    def head_sum(t):
        return _head_sum(t, hs, hb)

    def prepare(rr):
        def shifted(lo, hi):
            return z_ref[rr, :, lo:hi]

        r = shifted(0, D_MODEL)
        k = shifted(D_MODEL, 2 * D_MODEL)
        v = shifted(2 * D_MODEL, 3 * D_MODEL)
        w1 = shifted(COL_W1, COL_A1)
        a1 = shifted(COL_A1, COL_G1)
        g1 = shifted(COL_G1, P_RWKV_PAD)

        wl = w0_ref[0] + _dot_split(jnp.tanh(w1), w2h_ref[0], w2l_ref[0])
        logw = -EXP_MINUS_HALF * _sigmoid(wl)
        a = _sigmoid(_dot_split(a1, a2h_ref[0], a2l_ref[0]) + a0_ref[0])
        kkr = k * kk_ref[...]
        inv_norm = lax.rsqrt(jnp.maximum(_dot_split_lhs(kkr * kkr, hs), 1e-24))
        kk = kkr * _dot_split_lhs(inv_norm, hb)
        kd = k * (1.0 + (a - 1.0) * ka_ref[...])
        bb = kk * a
        bv_ref[0, rr] = head_sum(r * kd * rk_ref[...]) * v
        gs_ref[0, rr] = _sigmoid(g1)

        lw_hi, lw_lo = _split(logw)
        g_in = _dot(tri_ref[0], lw_hi) + _dot(tri_ref[0], lw_lo)
        g_ex = g_in - logw
        g_c = jnp.sum(logw, axis=0, keepdims=True)
        e_inv = jnp.exp(-g_in)
        e_hat = jnp.exp(g_c - g_in)
        nat_scr[rr, _KAP] = kk * jnp.exp(g_ex)
        nat_scr[rr, _RT] = r * jnp.exp(g_in)
        nat_scr[rr, _KT] = kd * e_inv
        nat_scr[rr, _BT] = bb * e_inv
        nat_scr[rr, _KH] = kd * e_hat
        nat_scr[rr, _BH] = bb * e_hat
        nat_scr[rr, _V] = v
        pc_scr[rr] = jnp.exp(g_c)

    for rr in range(R):
        prepare(rr)

    lane_head = lax.broadcasted_iota(jnp.int32, (C, GROUP_W), 1) // HEAD
    ii = lax.broadcasted_iota(jnp.int32, (GROUP_W, GROUP_W), 0)
    jj = lax.broadcasted_iota(jnp.int32, (GROUP_W, GROUP_W), 1)
    eye = ii == jj

    def stacked(t):
        return jnp.concatenate([jnp.where(lane_head == j, t, 0.0) for j in range(GROUP_HEADS)], axis=0)

    def collapse(t):
        return t[0:C] + t[C:2 * C] + t[2 * C:3 * C] + t[3 * C:4 * C]

    G = range(R * N_GROUPS)

    def lanes(q):
        return slice((q % N_GROUPS) * GROUP_W, (q % N_GROUPS + 1) * GROUP_W)

    def nat(i, q):
        return nat_scr[q // N_GROUPS, i, :, lanes(q)]

    x_kap = [stacked(nat(_KAP, g)) for g in G]
    x_v = [_bf(stacked(nat(_V, g))) for g in G]
    x_bk = [jnp.concatenate([_bf(stacked(nat(_BT, g))), _bf(stacked(nat(_KT, g)))], axis=0) for g in G]
    kr = [_bf(jnp.concatenate([nat(_KAP, g), nat(_RT, g)], axis=0)) for g in G]
    akr = [_dot_nt(kr[g], x_bk[g]) for g in G]

    def tiled(t):
        return jnp.concatenate([t] * GROUP_HEADS, axis=0)

    same_head = (ii // C) == (jj // C)

    def block_diag(side):
        t = tiled(_bf(side))
        return jnp.where(same_head, t, jnp.zeros_like(t))

    msl_c = collapse(msl_ref[0]) > 0.5
    minc_c = collapse(minc_ref[0]) > 0.5
    n_side = [jnp.where(msl_c, akr[g][:C, :GROUP_W], 0.0) for g in G]
    a_kk = [_bf(jnp.where(msl_c, akr[g][:C, GROUP_W:], 0.0)) for g in G]
    a_rb = [_bf(jnp.where(minc_c, akr[g][C:, :GROUP_W], 0.0)) for g in G]
    a_rk = [_bf(jnp.where(minc_c, akr[g][C:, GROUP_W:], 0.0)) for g in G]
    g0 = [stacked(_dot(a_kk[g], x_v[g])) for g in G]
    y0 = [_dot(a_rk[g], x_v[g]) for g in G]
    eye_side = collapse(jnp.where(eye, 1.0, 0.0))
    p_inv = [eye_side - n_side[g] for g in G]
    m_pow = [_dot(_bf(n_side[g]), block_diag(n_side[g])) for g in G]
    for _ in range(4):
        both = [_dot(_bf(jnp.concatenate([m_pow[g], p_inv[g]], axis=0)), block_diag(m_pow[g])) for g in G]
        m_pow = [both[g][:C] for g in G]
        p_inv = [p_inv[g] + both[g][C:] for g in G]
    p_inv = [p_inv[g] + _dot(_bf(p_inv[g]), block_diag(m_pow[g])) for g in G]
    w_nat = [_dot(_bf(p_inv[g]), jnp.concatenate([_bf(x_kap[g]), _bf(g0[g])], axis=1)) for g in G]
    w12 = [_bf(jnp.concatenate([stacked(w_nat[g][:, :GROUP_W]), stacked(w_nat[g][:, GROUP_W:])], axis=1))
           for g in G]
    aw = [_dot(a_rb[g], w12[g]) for g in G]
    def heads_transposed(t):
        tt = t.T
        return jnp.concatenate([tt[j * HEAD:(j + 1) * HEAD] for j in range(GROUP_HEADS)], axis=1)

    bw = [_dot(_bf(heads_transposed(nat(_BH, g))), w12[g]) for g in G]
    kv = [_dot(_bf(heads_transposed(nat(_KH, g))), x_v[g]) for g in G]
    h_prev = [H_scr[g // N_GROUPS, g % N_GROUPS] for g in G]
    ys = []
    h_new = []
    for g in G:
        h_hi = block_diag(h_prev[g])
        h_lo = block_diag(h_prev[g] - _bf(h_prev[g]).astype(F32))
        qb = _bf(nat(_RT, g) - aw[g][:, :GROUP_W])
        m_side = eye_side * pc_scr[g // N_GROUPS, :, lanes(g)] - bw[g][:, :GROUP_W]
        m_hi, m_lo = _split(m_side)
        by_hi = _dot(jnp.concatenate([m_hi, m_lo, qb], axis=0), h_hi)
        by_lo = _dot(jnp.concatenate([m_hi, qb], axis=0), h_lo)
        ys.append(by_hi[2 * C:] + by_lo[C:] + y0[g] - aw[g][:, GROUP_W:])
        h_new.append(by_hi[:C] + by_hi[C:2 * C] + by_lo[:C] + kv[g] - bw[g][:, GROUP_W:])
    for rr in range(R):
        y_ref[0, rr] = jnp.concatenate(ys[rr * N_GROUPS:(rr + 1) * N_GROUPS], axis=1)
    for g in G:
        H_scr[g // N_GROUPS, g % N_GROUPS] = h_new[g]

    @pl.when(c == n_chunks - 1)
    def _():
        hT_ref[0] = H_scr[...]


def _rwkv(z, pr, h0):
    B, L, N = z.shape
    C = CHUNK
    nC = L // C
    D = D_MODEL

    def cidx(d, c):
        return c + d * (nC - 1 - 2 * c)

    zmap = lambda d, b, c: (b, cidx(d, c), 0)
    const2 = lambda d, b, c: (0, 0)
    dir3 = lambda d, b, c: (d, 0, 0)
    omap = lambda d, b, c: (d, b, cidx(d, c), 0)
    smap = lambda d, b, c: (d, b, 0, 0, 0)
    R = RWKV_ROWS_PER_STEP if B % RWKV_ROWS_PER_STEP == 0 else 1
    kern = functools.partial(_rwkv_kernel, n_chunks=nC)
    return pl.pallas_call(
        kern,
        grid=(2, B // R, nC),
        in_specs=[pl.BlockSpec((R, C, N), zmap),
                  pl.BlockSpec((1, 1, D), dir3),
                  pl.BlockSpec((1, 2 * LORA_W, D), dir3),
                  pl.BlockSpec((1, 2 * LORA_W, D), dir3),
                  pl.BlockSpec((1, 1, D), dir3),
                  pl.BlockSpec((1, 2 * LORA_A, D), dir3),
                  pl.BlockSpec((1, 2 * LORA_A, D), dir3),
                  pl.BlockSpec((1, D), const2),
                  pl.BlockSpec((1, D), const2),
                  pl.BlockSpec((1, D), const2),
                  pl.BlockSpec((D, 128), const2),
                  pl.BlockSpec((128, D), const2),
                  pl.BlockSpec((1, C, C), dir3),
                  pl.BlockSpec((1, GROUP_W, GROUP_W), dir3),
                  pl.BlockSpec((1, GROUP_W, GROUP_W), dir3),
                  pl.BlockSpec((1, R, N_GROUPS, HEAD, GROUP_W), smap)],
        out_specs=[pl.BlockSpec((1, R, C, D), omap),
                   pl.BlockSpec((1, R, C, D), omap),
                   pl.BlockSpec((1, R, C, LORA_G_PAD), omap),
                   pl.BlockSpec((1, R, N_GROUPS, HEAD, GROUP_W), smap)],
        out_shape=[jax.ShapeDtypeStruct((2, B, L, D), F32),
                   jax.ShapeDtypeStruct((2, B, L, D), F32),
                   jax.ShapeDtypeStruct((2, B, L, LORA_G_PAD), F32),
                   jax.ShapeDtypeStruct((2, B, N_GROUPS, HEAD, GROUP_W), F32)],
        scratch_shapes=[pltpu.VMEM((R, N_GROUPS, HEAD, GROUP_W), F32),
                        pltpu.VMEM((R, 7, C, D), F32),
                        pltpu.VMEM((R, 1, D), F32)],
        compiler_params=_cp("arbitrary", "arbitrary", "arbitrary"),
        name="rwkv",
    )(z, pr["w0"], *pr["w2p"], pr["a0"], *pr["a2p"], pr["k_k"], pr["k_a"], pr["r_k"],
      pr["hs"], pr["hb"], pr["tri"], pr["msl"], pr["minc"], h0)


def _conv_kernel(za_ref, zb_ref, w_ref, o_ref, upad, *, L, stride, rows_per_step):
    pad = CONV_HALF * stride
    TC = za_ref.shape[-1]
    upad[0:pad, :] = jnp.zeros((pad, TC), F32)
    upad[pad + L:pad + L + pad, :] = jnp.zeros((pad, TC), F32)
    upad[pad:pad + L, :] = za_ref[0] * _sigmoid(zb_ref[0])
    RB = rows_per_step

    def block(r0):
        acc = jnp.zeros((RB, TC), F32)
        for j in range(CONV_K):
            acc = acc + w_ref[j:j + 1, :] * upad[pl.ds(r0 + j * stride, RB), :]
        o_ref[0, pl.ds(r0, RB), :] = acc

    if stride % 8 == 0:
        def body(i, carry):
            block(pl.multiple_of(i * RB, RB))
            return carry
        lax.fori_loop(0, L // RB, body, 0)
    else:
        for i in range(L // RB):
            block(i * RB)


def _conv(zcg, conv_w, stride):
    B, L, _ = zcg.shape
    D = D_MODEL
    TC = 128
    nct = D // TC
    wpad = jnp.pad(conv_w, ((0, 32 - CONV_K), (0, 0)))
    RB = min(L, 128)
    kern = functools.partial(_conv_kernel, L=L, stride=stride, rows_per_step=RB)
    return pl.pallas_call(
        kern,
        grid=(B, nct),
        in_specs=[pl.BlockSpec((1, L, TC), lambda b, j: (b, 0, j)),
                  pl.BlockSpec((1, L, TC), lambda b, j: (b, 0, j + nct)),
                  pl.BlockSpec((32, TC), lambda b, j: (0, j))],
        out_specs=pl.BlockSpec((1, L, TC), lambda b, j: (b, 0, j)),
        out_shape=jax.ShapeDtypeStruct((B, L, D), F32),
        scratch_shapes=[pltpu.VMEM((L + 2 * CONV_HALF * stride, TC), F32)],
        compiler_params=_cp("parallel", "parallel"),
        name="conv",
    )(zcg, zcg, wpad)


def _post_kernel(x_ref, yf_ref, yb_ref, bf_ref, bb_ref, gs_ref, cv_ref, zg_ref,
                 lng_ref, lnb_ref, g2_ref, woa_ref, cng_ref, cnb_ref, wob_ref, gb_ref, wout_ref,
                 gt_ref, n2g_ref, sh2_ref, sc2_ref, hs_ref, hb_ref, xo_ref, hn_ref):
    D = D_MODEL
    hs = hs_ref[...]
    hb = hb_ref[...]

    def head_mean(t):
        return _head_sum(t, hs, hb) * (1.0 / HEAD)

    o = yf_ref[0, 0] + yb_ref[0, 0]
    oc = o - head_mean(o)
    on = oc * lax.rsqrt(head_mean(oc * oc) + GN_EPS) * lng_ref[...] + lnb_ref[...]
    on = on + bf_ref[0, 0] + bb_ref[0, 0]
    gate = _dot(_bf(gs_ref[0, 0]), g2_ref[...])
    y_a = _dot(_bf(on * gate), woa_ref[...])

    cv = cv_ref[0]
    cm = jnp.mean(cv, axis=-1, keepdims=True)
    cc = cv - cm
    cn = cc * lax.rsqrt(jnp.mean(cc * cc, axis=-1, keepdims=True) + LN_EPS) * cng_ref[...] + cnb_ref[...]
    y_b = _dot(_bf(cn * _sigmoid(cn)), wob_ref[...])

    gates = _sigmoid(zg_ref[0] + gb_ref[...])
    m = gates[:, :D] * y_a + gates[:, D:] * y_b
    xn = x_ref[0] + gt_ref[0] * _dot(_bf(m), wout_ref[...])
    xo_ref[0] = xn
    hn_ref[0] = _bf(_rms(xn, n2g_ref[...]) * (1.0 + sc2_ref[0]) + sh2_ref[0])


def _post(x, y, bv, gs, cv, zcg, pr, gt1, sh2, sc2):
    B, L, D = x.shape
    TM = min(L, 256)
    tok = lambda b, i: (b, i, 0)
    fwd = lambda b, i: (0, b, i, 0)
    bwd = lambda b, i: (1, b, i, 0)
    cst = lambda b, i: (0, 0)
    per_b = lambda b, i: (b, 0, 0)
    row = pl.BlockSpec((1, D), cst)
    mat = pl.BlockSpec((D, D), cst)
    return pl.pallas_call(
        _post_kernel,
        grid=(B, L // TM),
        in_specs=[pl.BlockSpec((1, TM, D), tok),
                  pl.BlockSpec((1, 1, TM, D), fwd), pl.BlockSpec((1, 1, TM, D), bwd),
                  pl.BlockSpec((1, 1, TM, D), fwd), pl.BlockSpec((1, 1, TM, D), bwd),
                  pl.BlockSpec((1, 1, TM, LORA_G_PAD), fwd),
                  pl.BlockSpec((1, TM, D), tok),
                  pl.BlockSpec((1, TM, 2 * D), lambda b, i: (b, i, 1)),
                  row, row, pl.BlockSpec((LORA_G_PAD, D), cst), mat,
                  row, row, mat, pl.BlockSpec((1, 2 * D), cst), mat,
                  pl.BlockSpec((1, 1, D), per_b), row,
                  pl.BlockSpec((1, 1, D), per_b), pl.BlockSpec((1, 1, D), per_b),
                  pl.BlockSpec((D, 128), cst), pl.BlockSpec((128, D), cst)],
        out_specs=[pl.BlockSpec((1, TM, D), tok), pl.BlockSpec((1, TM, D), tok)],
        out_shape=[jax.ShapeDtypeStruct((B, L, D), F32), jax.ShapeDtypeStruct((B, L, D), BF16)],
        compiler_params=_cp("parallel", "parallel"),
        name="post",
    )(x, y, y, bv, bv, gs, cv, zcg,
      pr["lnx_g"], pr["lnx_b"], pr["g2p"], pr["w_oA"], pr["cnorm_g"], pr["cnorm_b"], pr["w_oB"],
      pr["gate_b"], pr["w_out"], gt1, pr["norm2_g"], sh2, sc2, pr["hs"], pr["hb"])


def _top16(s, rowid):
    rank = jnp.full(s.shape, NOT_SELECTED, F32)
    cur = s
    vals = []
    for r in range(PEER_TOPK):
        m = jnp.max(cur, axis=0, keepdims=True)
        idx = jnp.min(jnp.where(cur == m, rowid, 1e9), axis=0, keepdims=True)
        sel = rowid == idx
        rank = jnp.where(sel, float(r), rank)
        cur = jnp.where(sel, NEG_INF, cur)
        vals.append(m)
    return rank, vals


def _top16_untied(arrays):
    n = range(len(arrays))
    rank = [jnp.full(s.shape, NOT_SELECTED, F32) for s in arrays]
    cur = list(arrays)
    vals = [[] for _ in n]
    for r in range(PEER_TOPK):
        m = [jnp.max(cur[i], axis=0, keepdims=True) for i in n]
        sel = [cur[i] == m[i] for i in n]
        rank = [jnp.where(sel[i], float(r), rank[i]) for i in n]
        cur = [jnp.where(sel[i], NEG_INF, cur[i]) for i in n]
        for i in n:
            vals[i].append(m[i])
    n_ranked = [jnp.sum(jnp.where(rank[i] < float(PEER_TOPK), 1.0, 0.0), axis=0, keepdims=True) for i in n]
    return rank, vals, n_ranked


def _peer_prep_kernel(h_ref, wq_ref, keys_ref, r2_ref, na_ref, e1_ref, e2_ref, q_scr, rk_scr, vl_scr):
    TM = h_ref.shape[0]
    K = PEER_TOPK
    q_scr[...] = _bf(_dot(h_ref[...], wq_ref[...]))
    rowid = lax.broadcasted_iota(jnp.int32, (PEER_NKEYS, TM), 0).astype(F32)
    kaid = lax.broadcasted_iota(jnp.int32, (K, TM), 0).astype(F32)

    def head(h, carry):
        off = pl.multiple_of(h * 2 * PEER_HALF, 2 * PEER_HALF)
        s1 = _dot_nt(keys_ref[h, 0], q_scr[:, pl.ds(off, PEER_HALF)])
        s2 = _dot_nt(keys_ref[h, 1], q_scr[:, pl.ds(off + PEER_HALF, PEER_HALF)])
        ranks, valss, n_ranked = _top16_untied([s1, s2])
        for half in range(2):
            rk_scr[half] = ranks[half]
            vl_scr[half] = jnp.concatenate(valss[half], axis=0)
        n_max = jnp.maximum(n_ranked[0], n_ranked[1])

        @pl.when(jnp.max(n_max) > float(K))
        def _():
            for half, s in ((0, s1), (1, s2)):
                rank, vals = _top16(s, rowid)
                rk_scr[half] = rank
                vl_scr[half] = jnp.concatenate(vals, axis=0)

        rank1 = rk_scr[0]
        rank2 = rk_scr[1]
        v1 = vl_scr[0]
        vals1 = [v1[r:r + 1] for r in range(K)]
        vals2 = [vl_scr[1, r:r + 1, :] for r in range(K)]
        taken = jnp.zeros((K, TM), F32)
        front = v1 + vals2[0]
        for _ in range(K):
            m = jnp.max(front, axis=0, keepdims=True)
            idx = jnp.min(jnp.where(front == m, kaid, 1e9), axis=0, keepdims=True)
            sel = kaid == idx
            taken = taken + jnp.where(sel, 1.0, 0.0)
            nxt = jnp.full((K, TM), NEG_INF, F32)
            for kb in range(1, K):
                nxt = jnp.where(taken == float(kb), vals2[kb], nxt)
            front = jnp.where(sel, v1 + nxt, front)
        e1k = jnp.exp(v1 - vals1[0])
        pref = jnp.zeros((1, TM), F32)
        zrow = jnp.zeros((K, TM), F32)
        for kb in range(K):
            pref = pref + jnp.exp(vals2[kb] - vals2[0])
            zrow = jnp.where(taken == float(kb + 1), pref, zrow)
        z = jnp.sum(e1k * zrow, axis=0, keepdims=True)
        na = jnp.zeros((PEER_NKEYS, TM), F32)
        for ka in range(K):
            na = jnp.where(rank1 == float(ka), taken[ka:ka + 1], na)
        r2_ref[h] = _bf(rank2)
        na_ref[h] = na
        e1_ref[h] = jnp.where(rank1 < float(K), jnp.exp(s1 - vals1[0]) / z, 0.0)
        e2_ref[h] = _bf(jnp.where(rank2 < float(K), jnp.exp(s2 - vals2[0]), 0.0))
        return carry

    lax.fori_loop(0, PEER_HEADS, head, 0)


def _peer_prep(hn, wq, keys, TM):
    T, D = hn.shape
    Q = wq.shape[1]
    shp = jax.ShapeDtypeStruct((PEER_HEADS, PEER_NKEYS, T), F32)
    shp_bf = jax.ShapeDtypeStruct((PEER_HEADS, PEER_NKEYS, T), BF16)
    ospec = pl.BlockSpec((PEER_HEADS, PEER_NKEYS, TM), lambda i: (0, 0, i))
    return pl.pallas_call(
        _peer_prep_kernel,
        grid=(T // TM,),
        in_specs=[pl.BlockSpec((TM, D), lambda i: (i, 0)),
                  pl.BlockSpec((D, Q), lambda i: (0, 0)),
                  pl.BlockSpec((PEER_HEADS, 2, PEER_NKEYS, PEER_HALF), lambda i: (0, 0, 0, 0))],
        out_specs=[ospec, ospec, ospec, ospec],
        out_shape=[shp_bf, shp, shp, shp_bf],
        scratch_shapes=[pltpu.VMEM((TM, Q), BF16),
                        pltpu.VMEM((2, PEER_NKEYS, TM), F32), pltpu.VMEM((2, PEER_TOPK, TM), F32)],
        compiler_params=_cp("parallel"),
        name="peer_prep",
    )(hn, wq, keys)


def _peer_dense_kernel(h_ref, r2_ref, na_ref, e1_ref, e2_ref, u_ref, vt_ref, x_ref, gt_ref, fg_ref,
                       o_ref, acc_t, ht_scr, s0_scr, s1_scr, aw0_scr, aw1_scr, *, block, n_blocks, final_norm):
    jj = pl.program_id(1)
    TM = h_ref.shape[0]
    SLAB = PEER_NKEYS
    EB = block

    @pl.when(jj == 0)
    def _():
        acc_t[...] = jnp.zeros_like(acc_t)
        s1_scr[...] = jnp.zeros_like(s1_scr)
        aw0_scr[...] = jnp.zeros_like(aw0_scr)
        ht_scr[...] = _bf(h_ref[...].astype(F32).T)

    zero_bf = jnp.zeros((SLAB, TM), BF16)

    def row_tile(ref, h, a):
        t = _bf(jnp.broadcast_to(ref[h, pl.ds(a, 1), :], (16, TM)))
        return jnp.concatenate([t] * (SLAB // 16), axis=0)

    def vpu_stage(k, s_ref, aw_ref):
        live = jnp.logical_and(k >= 0, k < n_blocks)
        kc = jnp.clip(k, 0, n_blocks - 1)
        for p in range(EB // (2 * SLAB)):
            lo = p * 2 * SLAB
            s = s_ref[lo:lo + 2 * SLAB, :]
            act = 0.5 * s * (1.0 + lax.erf(s * SQRT_HALF))
            weights = []
            for half in range(2):
                a = kc * (EB // SLAB) + p * 2 + half
                w = jnp.zeros((SLAB, TM), BF16)
                for h in range(PEER_HEADS):
                    chosen = r2_ref[h] < row_tile(na_ref, h, a)
                    w = w + jnp.where(chosen, e2_ref[h], zero_bf) * row_tile(e1_ref, h, a)
                weights.append(w)
            aw = act * jnp.concatenate(weights, axis=0)
            aw_ref[lo:lo + 2 * SLAB, :] = jnp.where(live, aw, jnp.zeros_like(aw))

    out0 = _dot(vt_ref[:, 0:EB], aw0_scr[...])
    vpu_stage(2 * jj - 1, s1_scr, aw1_scr)
    s0_scr[...] = _bf(_dot(u_ref[0:EB, :], ht_scr[...]))
    out1 = _dot(vt_ref[:, EB:2 * EB], aw1_scr[...])
    s1_scr[...] = _bf(_dot(u_ref[EB:2 * EB, :], ht_scr[...]))
    vpu_stage(2 * jj, s0_scr, aw0_scr)
    acc_t[...] += out0 + out1

    @pl.when(jj == pl.num_programs(1) - 1)
    def _():
        xn = x_ref[...] + gt_ref[0] * acc_t[...].T
        o_ref[...] = _rms(xn, fg_ref[...]) if final_norm else xn


def _peer_dense(x, hn, sel, u, vt, gt2, TM, tiles_per_batch, final_g):
    T, D = x.shape
    E = u.shape[0]
    EB = 1024
    final_norm = final_g is not None
    fg = (final_g if final_norm else jnp.ones((D,), F32)).reshape(1, D)
    sspec = pl.BlockSpec((PEER_HEADS, PEER_NKEYS, TM), lambda i, e: (0, 0, i))
    nE = E // EB
    nP = nE // 2
    kern = functools.partial(_peer_dense_kernel, block=EB, n_blocks=nE, final_norm=final_norm)
    return pl.pallas_call(
        kern,
        grid=(T // TM, nP + 1),
        in_specs=[pl.BlockSpec((TM, D), lambda i, e: (i, 0)),
                  sspec, sspec, sspec, sspec,
                  pl.BlockSpec((2 * EB, D), lambda i, e: (jnp.minimum(e, nP - 1), 0)),
                  pl.BlockSpec((D, 2 * EB), lambda i, e: (0, jnp.maximum(e - 1, 0))),
                  pl.BlockSpec((TM, D), lambda i, e: (i, 0)),
                  pl.BlockSpec((1, 1, D), lambda i, e: (i // tiles_per_batch, 0, 0)),
                  pl.BlockSpec((1, D), lambda i, e: (0, 0))],
        out_specs=pl.BlockSpec((TM, D), lambda i, e: (i, 0)),
        out_shape=jax.ShapeDtypeStruct((T, D), F32),
        scratch_shapes=[pltpu.VMEM((D, TM), F32), pltpu.VMEM((D, TM), BF16),
                        pltpu.VMEM((EB, TM), BF16), pltpu.VMEM((EB, TM), BF16),
                        pltpu.VMEM((EB, TM), BF16), pltpu.VMEM((EB, TM), BF16)],
        compiler_params=_cp("parallel", "arbitrary"),
        name="peer_dense",
    )(hn, *sel, u, vt, x, gt2, fg)


def _direction_masks():
    t = jnp.arange(CHUNK)
    le = (t[None, :] <= t[:, None]).astype(F32)
    tri = jnp.stack([le, le.T])
    i = jnp.arange(GROUP_W)
    same = (i[:, None] // CHUNK) == (i[None, :] // CHUNK)
    ti, tj = i[:, None] % CHUNK, i[None, :] % CHUNK
    msl = jnp.stack([same & (tj < ti), same & (tj > ti)]).astype(F32)
    minc = jnp.stack([same & (tj <= ti), same & (tj >= ti)]).astype(F32)
    return tri, msl, minc


def _layer_params(l, w_in, shift_mu, w0, w2, a0, a2, g2, k_k, k_a, r_k, lnx_g, lnx_b, w_oA, conv_w,
                  cnorm_g, cnorm_b, w_oB, gate_b, w_out, norm2_g):
    D = D_MODEL
    row = lambda t: t.reshape(1, -1)
    zeros = jnp.zeros((LORA_W, D), F32)
    tri, msl, minc = _direction_masks()
    hs = (jnp.arange(D)[:, None] // HEAD == jnp.arange(128)[None, :]).astype(F32)
    pad_cols = P_RWKV_PAD - P_RWKV
    return dict(
        w_rk=_bf(jnp.pad(w_in[l][:, :P_RWKV], ((0, 0), (0, pad_cols)))),
        w_cg=_bf(w_in[l][:, P_RWKV:]),
        mu=jnp.pad(shift_mu[l], ((0, 0), (0, pad_cols))),
        w0=w0[l].reshape(2, 1, D), a0=a0[l].reshape(2, 1, D),
        w2p=_split(jnp.stack([jnp.concatenate([w2[l, 0], zeros]), jnp.concatenate([zeros, w2[l, 1]])])),
        a2p=_split(jnp.stack([jnp.concatenate([a2[l, 0], zeros]), jnp.concatenate([zeros, a2[l, 1]])])),
        g2p=_bf(jnp.pad(g2[l], ((0, LORA_G_PAD - LORA_G), (0, 0)))),
        k_k=row(k_k[l]), k_a=row(k_a[l]), r_k=row(r_k[l]),
        lnx_g=row(lnx_g[l]), lnx_b=row(lnx_b[l]), w_oA=_bf(w_oA[l]),
        conv_w=conv_w[l], cnorm_g=row(cnorm_g[l]), cnorm_b=row(cnorm_b[l]), w_oB=_bf(w_oB[l]),
        gate_b=row(gate_b[l]), w_out=_bf(w_out[l]), norm2_g=row(norm2_g[l]),
        hs=_bf(hs), hb=_bf(hs.T), tri=_bf(tri), msl=msl, minc=minc)


def _mixer(x, mod, pr, norm1_g, stride, h0, emit):
    B, L, D = x.shape
    sh1, sc1, gt1, sh2, sc2 = (mod[:, i:i + 1, :] for i in range(5))
    z_rk = _proj_shift(x, sh1, sc1, norm1_g, pr["w_rk"], pr["mu"])
    y, bv, gs, h_t = _rwkv(z_rk, pr, h0)
    if not emit:
        return None, None, h_t
    z_cg = _proj(x, sh1, sc1, norm1_g, pr["w_cg"])
    cv = _conv(z_cg, pr["conv_w"], stride)
    xn, hn = _post(x, y, bv, gs, cv, z_cg, pr, gt1, sh2, sc2)
    return xn, hn, h_t


def _peer(x, hn, wq, keys, u, v, gt2, final_g=None):
    B, L, D = x.shape
    T = B * L
    TM = min(L, 256)
    sel = _peer_prep(hn.reshape(T, D), wq, keys, TM)
    out = _peer_dense(x.reshape(T, D), hn.reshape(T, D), sel, u, v, gt2, TM, L // TM, final_g)
    return out.reshape(B, L, D)


def kernel(x, c, ctx, c_ctx, ada_w, ada_b, norm1_g, norm2_g, w_in, shift_mu, w0, w2, a0, a2, g2, k_k, k_a, r_k, lnx_g, lnx_b, w_oA, conv_w, cnorm_g, cnorm_b, w_oB, gate_b, w_out, w_q, sub_keys, peer_u, peer_v, final_g):
    B, L, D = x.shape
    depth = ada_w.shape[0]
    xc = ctx
    n_rows = -(-(B + 1) // 8) * 8
    c_rows = jnp.pad(jnp.concatenate([c, c_ctx[None, :]], axis=0), ((0, n_rows - B - 1), (0, 0)))
    zero_state = jnp.zeros((2, B, N_GROUPS, HEAD, GROUP_W), F32)
    for l in range(depth):
        last = l == depth - 1
        pr = _layer_params(l, w_in, shift_mu, w0, w2, a0, a2, g2, k_k, k_a, r_k, lnx_g, lnx_b, w_oA,
                           conv_w, cnorm_g, cnorm_b, w_oB, gate_b, w_out, norm2_g)
        mod_all = _modulation(c_rows, ada_w[l], ada_b[l])
        mod = mod_all[:B].reshape(B, 6, D)
        modc = jnp.broadcast_to(mod_all[B].reshape(1, 6, D), (B, 6, D))
        wq = _bf(w_q[l])
        keys = _bf(sub_keys[l])
        u = _bf(peer_u[l])
        v = _bf(peer_v[l]).T

        xc_new, hnc, ctx_states = _mixer(xc, modc, pr, norm1_g[l], 1, zero_state, emit=not last)
        xn, hn, _ = _mixer(x, mod, pr, norm1_g[l], GRID_W, ctx_states, emit=True)
        x = _peer(xn, hn, wq, keys, u, v, mod[:, 5:6, :], final_g if last else None)
        if not last:
            xc = _peer(xc_new, hnc, wq, keys, u, v, modc[:, 5:6, :])
    return x
```

```python
import functools

import jax
import jax.numpy as jnp
from jax import lax
from jax.experimental import pallas as pl
from jax.experimental.pallas import tpu as pltpu

F32 = jnp.float32
BF16 = jnp.bfloat16
HI = lax.Precision.HIGHEST

D_MODEL = 1024
HEAD = 64
HEADS = D_MODEL // HEAD
GROUP_HEADS = 4
GROUP_W = GROUP_HEADS * HEAD
N_GROUPS = HEADS // GROUP_HEADS
CHUNK = 64
assert CHUNK == HEAD
RWKV_ROWS_PER_STEP = 4
LORA_W = 64
LORA_A = 64
LORA_G = 160
LORA_G_PAD = 256
P_RWKV = 3 * D_MODEL + 2 * LORA_W + 2 * LORA_A + LORA_G
P_RWKV_PAD = 3 * D_MODEL + 2 * LORA_W + 2 * LORA_A + LORA_G_PAD
COL_W1 = 3 * D_MODEL
COL_A1 = COL_W1 + 2 * LORA_W
COL_G1 = COL_A1 + 2 * LORA_A
CONV_K = 31
CONV_HALF = CONV_K // 2
GRID_W = 64
PEER_HEADS = 8
PEER_NKEYS = 128
PEER_HALF = 128
PEER_TOPK = 16
NORM_EPS = 1e-6
LN_EPS = 1e-5
GN_EPS = HEAD * 1e-5
VMEM_LIMIT = 56 * 1024 * 1024
NOT_SELECTED = 99.0
NEG_INF = float("-inf")
SQRT_HALF = 0.7071067811865476
EXP_MINUS_HALF = 0.6065306597126334


def _cp(*sem):
    return pltpu.CompilerParams(dimension_semantics=sem, vmem_limit_bytes=VMEM_LIMIT)


def _dot(a, b):
    return jnp.dot(a, b, preferred_element_type=F32)


def _dot_hi(a, b):
    return jnp.dot(a, b, precision=HI, preferred_element_type=F32)


def _dot_nt(a, b):
    return lax.dot_general(a, b, (((1,), (1,)), ((), ())), preferred_element_type=F32)


def _dot_tn(a, b):
    return lax.dot_general(a, b, (((0,), (0,)), ((), ())), preferred_element_type=F32)


def _bf(a):
    return a.astype(BF16)


def _split(a):
    hi = a.astype(BF16)
    return hi, (a - hi.astype(F32)).astype(BF16)


def _dot_split_lhs(a, b_bf):
    hi, lo = _split(a)
    return _dot(hi, b_bf) + _dot(lo, b_bf)


def _dot_split(a, b_hi, b_lo):
    hi, lo = _split(a)
    return _dot(hi, b_hi) + _dot(hi, b_lo) + _dot(lo, b_hi)


def _head_sum(t, hs_bf, hb_bf):
    return _dot_split_lhs(_dot_split_lhs(t, hs_bf), hb_bf)


def _sigmoid(x):
    return 1.0 / (1.0 + jnp.exp(-x))


def _rms(x, g):
    return x * lax.rsqrt(jnp.mean(x * x, axis=-1, keepdims=True) + NORM_EPS) * g


def _mod_kernel(c_ref, w_ref, b_ref, o_ref):
    c = c_ref[...]
    o_ref[...] = _dot_hi(c * _sigmoid(c), w_ref[...]) + b_ref[...]


def _modulation(c_rows, ada_w, ada_b):
    R, D = c_rows.shape
    N = ada_w.shape[1]
    TN = 512
    return pl.pallas_call(
        _mod_kernel,
        grid=(N // TN,),
        in_specs=[pl.BlockSpec((R, D), lambda j: (0, 0)),
                  pl.BlockSpec((D, TN), lambda j: (0, j)),
                  pl.BlockSpec((1, TN), lambda j: (0, j))],
        out_specs=pl.BlockSpec((R, TN), lambda j: (0, j)),
        out_shape=jax.ShapeDtypeStruct((R, N), F32),
        compiler_params=_cp("arbitrary"),
        name="modulation",
    )(c_rows, ada_w, ada_b.reshape(1, N))


def _proj_kernel(x_ref, sh_ref, sc_ref, g_ref, w_ref, o_ref):
    h = _rms(x_ref[0], g_ref[...]) * (1.0 + sc_ref[0]) + sh_ref[0]
    o_ref[0] = _dot(_bf(h), w_ref[...])


def _proj(x, sh, sc, g, w):
    B, L, D = x.shape
    N = w.shape[1]
    TM = min(L, 256)
    return pl.pallas_call(
        _proj_kernel,
        grid=(B, L // TM),
        in_specs=[pl.BlockSpec((1, TM, D), lambda b, i: (b, i, 0)),
                  pl.BlockSpec((1, 1, D), lambda b, i: (b, 0, 0)),
                  pl.BlockSpec((1, 1, D), lambda b, i: (b, 0, 0)),
                  pl.BlockSpec((1, D), lambda b, i: (0, 0)),
                  pl.BlockSpec((D, N), lambda b, i: (0, 0))],
        out_specs=pl.BlockSpec((1, TM, N), lambda b, i: (b, i, 0)),
        out_shape=jax.ShapeDtypeStruct((B, L, N), F32),
        compiler_params=_cp("parallel", "parallel"),
        name="proj",
    )(x, sh, sc, g.reshape(1, D), w)


def _proj_shift_kernel(x_ref, xp_ref, xn_ref, sh_ref, sc_ref, g_ref, w_ref, mu_ref, o_ref):
    i = pl.program_id(1)
    TM = x_ref.shape[1]
    xs = jnp.concatenate([xp_ref[0], x_ref[0], xn_ref[0]], axis=0)
    h = _rms(xs, g_ref[...]) * (1.0 + sc_ref[0]) + sh_ref[0]
    z = _dot(_bf(h), w_ref[...])
    row = lax.broadcasted_iota(jnp.int32, (TM, 1), 0)
    at_start = jnp.logical_and(row == 0, i == 0)
    at_end = jnp.logical_and(row == TM - 1, i == pl.num_programs(1) - 1)
    zp = jnp.where(at_start, 0.0, z[7:7 + TM])
    zn = jnp.where(at_end, 0.0, z[9:9 + TM])
    m0 = mu_ref[0:1, :]
    m1 = mu_ref[1:2, :]
    o_ref[0] = z[8:8 + TM] * (1.0 - m0 - m1) + m0 * zp + m1 * zn


def _proj_shift(x, sh, sc, g, w, mu):
    B, L, D = x.shape
    N = w.shape[1]
    TM = min(L, 256)
    nb8 = L // 8
    return pl.pallas_call(
        _proj_shift_kernel,
        grid=(B, L // TM),
        in_specs=[pl.BlockSpec((1, TM, D), lambda b, i: (b, i, 0)),
                  pl.BlockSpec((1, 8, D), lambda b, i: (b, jnp.maximum(i * (TM // 8) - 1, 0), 0)),
                  pl.BlockSpec((1, 8, D), lambda b, i: (b, jnp.minimum((i + 1) * (TM // 8), nb8 - 1), 0)),
                  pl.BlockSpec((1, 1, D), lambda b, i: (b, 0, 0)),
                  pl.BlockSpec((1, 1, D), lambda b, i: (b, 0, 0)),
                  pl.BlockSpec((1, D), lambda b, i: (0, 0)),
                  pl.BlockSpec((D, N), lambda b, i: (0, 0)),
                  pl.BlockSpec((2, N), lambda b, i: (0, 0))],
        out_specs=pl.BlockSpec((1, TM, N), lambda b, i: (b, i, 0)),
        out_shape=jax.ShapeDtypeStruct((B, L, N), F32),
        compiler_params=_cp("parallel", "parallel"),
        name="proj_shift",
    )(x, x, x, sh, sc, g.reshape(1, D), w, mu)


_KAP, _RT, _KT, _BT, _KH, _BH, _V = range(7)


def _rwkv_kernel(z_ref, w0_ref, w2h_ref, w2l_ref, a0_ref, a2h_ref, a2l_ref,
                 kk_ref, ka_ref, rk_ref, hs_ref, hb_ref, tri_ref, msl_ref, minc_ref, h0_ref,
                 y_ref, bv_ref, gs_ref, hT_ref, H_scr, nat_scr, pc_scr, *, n_chunks):
    d = pl.program_id(0)
    c = pl.program_id(2)
    cc = jnp.where(d == 0, c, n_chunks - 1 - c)
    C = CHUNK

    R = z_ref.shape[0]

    @pl.when(c == 0)
    def _():
        H_scr[...] = h0_ref[0]

    hs = hs_ref[...]
    hb = hb_ref[...]

    def head_sum(t):
        return _head_sum(t, hs, hb)

    def prepare(rr):
        def shifted(lo, hi):
            return z_ref[rr, :, lo:hi]

        r = shifted(0, D_MODEL)
        k = shifted(D_MODEL, 2 * D_MODEL)
        v = shifted(2 * D_MODEL, 3 * D_MODEL)
        w1 = shifted(COL_W1, COL_A1)
        a1 = shifted(COL_A1, COL_G1)
        g1 = shifted(COL_G1, P_RWKV_PAD)

        wl = w0_ref[0] + _dot_split(jnp.tanh(w1), w2h_ref[0], w2l_ref[0])
        logw = -EXP_MINUS_HALF * _sigmoid(wl)
        a = _sigmoid(_dot_split(a1, a2h_ref[0], a2l_ref[0]) + a0_ref[0])
        kkr = k * kk_ref[...]
        inv_norm = lax.rsqrt(jnp.maximum(_dot_split_lhs(kkr * kkr, hs), 1e-24))
        kk = kkr * _dot_split_lhs(inv_norm, hb)
        kd = k * (1.0 + (a - 1.0) * ka_ref[...])
        bb = kk * a
        bv_ref[0, rr] = head_sum(r * kd * rk_ref[...]) * v
        gs_ref[0, rr] = _sigmoid(g1)

        lw_hi, lw_lo = _split(logw)
        g_in = _dot(tri_ref[0], lw_hi) + _dot(tri_ref[0], lw_lo)
        g_ex = g_in - logw
        g_c = jnp.sum(logw, axis=0, keepdims=True)
        e_inv = jnp.exp(-g_in)
        e_hat = jnp.exp(g_c - g_in)
        nat_scr[rr, _KAP] = kk * jnp.exp(g_ex)
        nat_scr[rr, _RT] = r * jnp.exp(g_in)
        nat_scr[rr, _KT] = kd * e_inv
        nat_scr[rr, _BT] = bb * e_inv
        nat_scr[rr, _KH] = kd * e_hat
        nat_scr[rr, _BH] = bb * e_hat
        nat_scr[rr, _V] = v
        pc_scr[rr] = jnp.exp(g_c)

    for rr in range(R):
        prepare(rr)

    lane_head = lax.broadcasted_iota(jnp.int32, (C, GROUP_W), 1) // HEAD
    ii = lax.broadcasted_iota(jnp.int32, (GROUP_W, GROUP_W), 0)
    jj = lax.broadcasted_iota(jnp.int32, (GROUP_W, GROUP_W), 1)
    eye = ii == jj

    def stacked(t):
        return jnp.concatenate([jnp.where(lane_head == j, t, 0.0) for j in range(GROUP_HEADS)], axis=0)

    def collapse(t):
        return t[0:C] + t[C:2 * C] + t[2 * C:3 * C] + t[3 * C:4 * C]

    G = range(R * N_GROUPS)

    def lanes(q):
        return slice((q % N_GROUPS) * GROUP_W, (q % N_GROUPS + 1) * GROUP_W)

    def nat(i, q):
        return nat_scr[q // N_GROUPS, i, :, lanes(q)]

    x_kap = [stacked(nat(_KAP, g)) for g in G]
    x_v = [_bf(stacked(nat(_V, g))) for g in G]
    x_bk = [jnp.concatenate([_bf(stacked(nat(_BT, g))), _bf(stacked(nat(_KT, g)))], axis=0) for g in G]
    kr = [_bf(jnp.concatenate([nat(_KAP, g), nat(_RT, g)], axis=0)) for g in G]
    akr = [_dot_nt(kr[g], x_bk[g]) for g in G]

    def tiled(t):
        return jnp.concatenate([t] * GROUP_HEADS, axis=0)

    same_head = (ii // C) == (jj // C)

    def block_diag(side):
        t = tiled(_bf(side))
        return jnp.where(same_head, t, jnp.zeros_like(t))

    msl_c = collapse(msl_ref[0]) > 0.5
    minc_c = collapse(minc_ref[0]) > 0.5
    n_side = [jnp.where(msl_c, akr[g][:C, :GROUP_W], 0.0) for g in G]
    a_kk = [_bf(jnp.where(msl_c, akr[g][:C, GROUP_W:], 0.0)) for g in G]
    a_rb = [_bf(jnp.where(minc_c, akr[g][C:, :GROUP_W], 0.0)) for g in G]
    a_rk = [_bf(jnp.where(minc_c, akr[g][C:, GROUP_W:], 0.0)) for g in G]
    g0 = [stacked(_dot(a_kk[g], x_v[g])) for g in G]
    y0 = [_dot(a_rk[g], x_v[g]) for g in G]
    eye_side = collapse(jnp.where(eye, 1.0, 0.0))
    p_inv = [eye_side - n_side[g] for g in G]
    m_pow = [_dot(_bf(n_side[g]), block_diag(n_side[g])) for g in G]
    for _ in range(4):
        both = [_dot(_bf(jnp.concatenate([m_pow[g], p_inv[g]], axis=0)), block_diag(m_pow[g])) for g in G]
        m_pow = [both[g][:C] for g in G]
        p_inv = [p_inv[g] + both[g][C:] for g in G]
    p_inv = [p_inv[g] + _dot(_bf(p_inv[g]), block_diag(m_pow[g])) for g in G]
    w_nat = [_dot(_bf(p_inv[g]), jnp.concatenate([_bf(x_kap[g]), _bf(g0[g])], axis=1)) for g in G]
    w12 = [_bf(jnp.concatenate([stacked(w_nat[g][:, :GROUP_W]), stacked(w_nat[g][:, GROUP_W:])], axis=1))
           for g in G]
    aw = [_dot(a_rb[g], w12[g]) for g in G]
    def heads_transposed(t):
        tt = t.T
        return jnp.concatenate([tt[j * HEAD:(j + 1) * HEAD] for j in range(GROUP_HEADS)], axis=1)

    bw = [_dot(_bf(heads_transposed(nat(_BH, g))), w12[g]) for g in G]
    kv = [_dot(_bf(heads_transposed(nat(_KH, g))), x_v[g]) for g in G]
    h_prev = [H_scr[g // N_GROUPS, g % N_GROUPS] for g in G]
    ys = []
    h_new = []
    for g in G:
        h_hi = block_diag(h_prev[g])
        h_lo = block_diag(h_prev[g] - _bf(h_prev[g]).astype(F32))
        qb = _bf(nat(_RT, g) - aw[g][:, :GROUP_W])
        m_side = eye_side * pc_scr[g // N_GROUPS, :, lanes(g)] - bw[g][:, :GROUP_W]
        m_hi, m_lo = _split(m_side)
        by_hi = _dot(jnp.concatenate([m_hi, m_lo, qb], axis=0), h_hi)
        by_lo = _dot(jnp.concatenate([m_hi, qb], axis=0), h_lo)
        ys.append(by_hi[2 * C:] + by_lo[C:] + y0[g] - aw[g][:, GROUP_W:])
        h_new.append(by_hi[:C] + by_hi[C:2 * C] + by_lo[:C] + kv[g] - bw[g][:, GROUP_W:])
    for rr in range(R):
        y_ref[0, rr] = jnp.concatenate(ys[rr * N_GROUPS:(rr + 1) * N_GROUPS], axis=1)
    for g in G:
        H_scr[g // N_GROUPS, g % N_GROUPS] = h_new[g]

    @pl.when(c == n_chunks - 1)
    def _():
        hT_ref[0] = H_scr[...]


def _rwkv(z, pr, h0):
    B, L, N = z.shape
    C = CHUNK
    nC = L // C
    D = D_MODEL

    def cidx(d, c):
        return c + d * (nC - 1 - 2 * c)

    zmap = lambda d, b, c: (b, cidx(d, c), 0)
    const2 = lambda d, b, c: (0, 0)
    dir3 = lambda d, b, c: (d, 0, 0)
    omap = lambda d, b, c: (d, b, cidx(d, c), 0)
    smap = lambda d, b, c: (d, b, 0, 0, 0)
    R = RWKV_ROWS_PER_STEP if B % RWKV_ROWS_PER_STEP == 0 else 1
    kern = functools.partial(_rwkv_kernel, n_chunks=nC)
    return pl.pallas_call(
        kern,
        grid=(2, B // R, nC),
        in_specs=[pl.BlockSpec((R, C, N), zmap),
                  pl.BlockSpec((1, 1, D), dir3),
                  pl.BlockSpec((1, 2 * LORA_W, D), dir3),
                  pl.BlockSpec((1, 2 * LORA_W, D), dir3),
                  pl.BlockSpec((1, 1, D), dir3),
                  pl.BlockSpec((1, 2 * LORA_A, D), dir3),
                  pl.BlockSpec((1, 2 * LORA_A, D), dir3),
                  pl.BlockSpec((1, D), const2),
                  pl.BlockSpec((1, D), const2),
                  pl.BlockSpec((1, D), const2),
                  pl.BlockSpec((D, 128), const2),
                  pl.BlockSpec((128, D), const2),
                  pl.BlockSpec((1, C, C), dir3),
                  pl.BlockSpec((1, GROUP_W, GROUP_W), dir3),
                  pl.BlockSpec((1, GROUP_W, GROUP_W), dir3),
                  pl.BlockSpec((1, R, N_GROUPS, HEAD, GROUP_W), smap)],
        out_specs=[pl.BlockSpec((1, R, C, D), omap),
                   pl.BlockSpec((1, R, C, D), omap),
                   pl.BlockSpec((1, R, C, LORA_G_PAD), omap),
                   pl.BlockSpec((1, R, N_GROUPS, HEAD, GROUP_W), smap)],
        out_shape=[jax.ShapeDtypeStruct((2, B, L, D), F32),
                   jax.ShapeDtypeStruct((2, B, L, D), F32),
                   jax.ShapeDtypeStruct((2, B, L, LORA_G_PAD), F32),
                   jax.ShapeDtypeStruct((2, B, N_GROUPS, HEAD, GROUP_W), F32)],
        scratch_shapes=[pltpu.VMEM((R, N_GROUPS, HEAD, GROUP_W), F32),
                        pltpu.VMEM((R, 7, C, D), F32),
                        pltpu.VMEM((R, 1, D), F32)],
        compiler_params=_cp("arbitrary", "arbitrary", "arbitrary"),
        name="rwkv",
    )(z, pr["w0"], *pr["w2p"], pr["a0"], *pr["a2p"], pr["k_k"], pr["k_a"], pr["r_k"],
      pr["hs"], pr["hb"], pr["tri"], pr["msl"], pr["minc"], h0)


def _conv_kernel(za_ref, zb_ref, w_ref, o_ref, upad, *, L, stride, rows_per_step):
    pad = CONV_HALF * stride
    TC = za_ref.shape[-1]
    upad[0:pad, :] = jnp.zeros((pad, TC), F32)
    upad[pad + L:pad + L + pad, :] = jnp.zeros((pad, TC), F32)
    upad[pad:pad + L, :] = za_ref[0] * _sigmoid(zb_ref[0])
    RB = rows_per_step

    def block(r0):
        acc = jnp.zeros((RB, TC), F32)
        for j in range(CONV_K):
            acc = acc + w_ref[j:j + 1, :] * upad[pl.ds(r0 + j * stride, RB), :]
        o_ref[0, pl.ds(r0, RB), :] = acc

    if stride % 8 == 0:
        def body(i, carry):
            block(pl.multiple_of(i * RB, RB))
            return carry
        lax.fori_loop(0, L // RB, body, 0)
    else:
        for i in range(L // RB):
            block(i * RB)


def _conv(zcg, conv_w, stride):
    B, L, _ = zcg.shape
    D = D_MODEL
    TC = 128
    nct = D // TC
    wpad = jnp.pad(conv_w, ((0, 32 - CONV_K), (0, 0)))
    RB = min(L, 128)
    kern = functools.partial(_conv_kernel, L=L, stride=stride, rows_per_step=RB)
    return pl.pallas_call(
        kern,
        grid=(B, nct),
        in_specs=[pl.BlockSpec((1, L, TC), lambda b, j: (b, 0, j)),
                  pl.BlockSpec((1, L, TC), lambda b, j: (b, 0, j + nct)),
                  pl.BlockSpec((32, TC), lambda b, j: (0, j))],
        out_specs=pl.BlockSpec((1, L, TC), lambda b, j: (b, 0, j)),
        out_shape=jax.ShapeDtypeStruct((B, L, D), F32),
        scratch_shapes=[pltpu.VMEM((L + 2 * CONV_HALF * stride, TC), F32)],
        compiler_params=_cp("parallel", "parallel"),
        name="conv",
    )(zcg, zcg, wpad)


def _post_kernel(x_ref, yf_ref, yb_ref, bf_ref, bb_ref, gs_ref, cv_ref, zg_ref,
                 lng_ref, lnb_ref, g2_ref, woa_ref, cng_ref, cnb_ref, wob_ref, gb_ref, wout_ref,
                 gt_ref, n2g_ref, sh2_ref, sc2_ref, hs_ref, hb_ref, xo_ref, hn_ref):
    D = D_MODEL
    hs = hs_ref[...]
    hb = hb_ref[...]

    def head_mean(t):
        return _head_sum(t, hs, hb) * (1.0 / HEAD)

    o = yf_ref[0, 0] + yb_ref[0, 0]
    oc = o - head_mean(o)
    on = oc * lax.rsqrt(head_mean(oc * oc) + GN_EPS) * lng_ref[...] + lnb_ref[...]
    on = on + bf_ref[0, 0] + bb_ref[0, 0]
    gate = _dot(_bf(gs_ref[0, 0]), g2_ref[...])
    y_a = _dot(_bf(on * gate), woa_ref[...])

    cv = cv_ref[0]
    cm = jnp.mean(cv, axis=-1, keepdims=True)
    cc = cv - cm
    cn = cc * lax.rsqrt(jnp.mean(cc * cc, axis=-1, keepdims=True) + LN_EPS) * cng_ref[...] + cnb_ref[...]
    y_b = _dot(_bf(cn * _sigmoid(cn)), wob_ref[...])

    gates = _sigmoid(zg_ref[0] + gb_ref[...])
    m = gates[:, :D] * y_a + gates[:, D:] * y_b
    xn = x_ref[0] + gt_ref[0] * _dot(_bf(m), wout_ref[...])
    xo_ref[0] = xn
    hn_ref[0] = _bf(_rms(xn, n2g_ref[...]) * (1.0 + sc2_ref[0]) + sh2_ref[0])


def _post(x, y, bv, gs, cv, zcg, pr, gt1, sh2, sc2):
    B, L, D = x.shape
    TM = min(L, 256)
    tok = lambda b, i: (b, i, 0)
    fwd = lambda b, i: (0, b, i, 0)
    bwd = lambda b, i: (1, b, i, 0)
    cst = lambda b, i: (0, 0)
    per_b = lambda b, i: (b, 0, 0)
    row = pl.BlockSpec((1, D), cst)
    mat = pl.BlockSpec((D, D), cst)
    return pl.pallas_call(
        _post_kernel,
        grid=(B, L // TM),
        in_specs=[pl.BlockSpec((1, TM, D), tok),
                  pl.BlockSpec((1, 1, TM, D), fwd), pl.BlockSpec((1, 1, TM, D), bwd),
                  pl.BlockSpec((1, 1, TM, D), fwd), pl.BlockSpec((1, 1, TM, D), bwd),
                  pl.BlockSpec((1, 1, TM, LORA_G_PAD), fwd),
                  pl.BlockSpec((1, TM, D), tok),
                  pl.BlockSpec((1, TM, 2 * D), lambda b, i: (b, i, 1)),
                  row, row, pl.BlockSpec((LORA_G_PAD, D), cst), mat,
                  row, row, mat, pl.BlockSpec((1, 2 * D), cst), mat,
                  pl.BlockSpec((1, 1, D), per_b), row,
                  pl.BlockSpec((1, 1, D), per_b), pl.BlockSpec((1, 1, D), per_b),
                  pl.BlockSpec((D, 128), cst), pl.BlockSpec((128, D), cst)],
        out_specs=[pl.BlockSpec((1, TM, D), tok), pl.BlockSpec((1, TM, D), tok)],
        out_shape=[jax.ShapeDtypeStruct((B, L, D), F32), jax.ShapeDtypeStruct((B, L, D), BF16)],
        compiler_params=_cp("parallel", "parallel"),
        name="post",
    )(x, y, y, bv, bv, gs, cv, zcg,
      pr["lnx_g"], pr["lnx_b"], pr["g2p"], pr["w_oA"], pr["cnorm_g"], pr["cnorm_b"], pr["w_oB"],
      pr["gate_b"], pr["w_out"], gt1, pr["norm2_g"], sh2, sc2, pr["hs"], pr["hb"])


def _top16(s, rowid):
    rank = jnp.full(s.shape, NOT_SELECTED, F32)
    cur = s
    vals = []
    for r in range(PEER_TOPK):
        m = jnp.max(cur, axis=0, keepdims=True)
        idx = jnp.min(jnp.where(cur == m, rowid, 1e9), axis=0, keepdims=True)
        sel = rowid == idx
        rank = jnp.where(sel, float(r), rank)
        cur = jnp.where(sel, NEG_INF, cur)
        vals.append(m)
    return rank, vals


def _top16_untied(arrays):
    n = range(len(arrays))
    rank = [jnp.full(s.shape, NOT_SELECTED, F32) for s in arrays]
    cur = list(arrays)
    vals = [[] for _ in n]
    for r in range(PEER_TOPK):
        m = [jnp.max(cur[i], axis=0, keepdims=True) for i in n]
        sel = [cur[i] == m[i] for i in n]
        rank = [jnp.where(sel[i], float(r), rank[i]) for i in n]
        cur = [jnp.where(sel[i], NEG_INF, cur[i]) for i in n]
        for i in n:
            vals[i].append(m[i])
    n_ranked = [jnp.sum(jnp.where(rank[i] < float(PEER_TOPK), 1.0, 0.0), axis=0, keepdims=True) for i in n]
    return rank, vals, n_ranked


def _peer_prep_kernel(h_ref, wq_ref, keys_ref, r2_ref, na_ref, e1_ref, e2_ref, q_scr, rk_scr, vl_scr):
    TM = h_ref.shape[0]
    K = PEER_TOPK
    q_scr[...] = _bf(_dot(h_ref[...], wq_ref[...]))
    rowid = lax.broadcasted_iota(jnp.int32, (PEER_NKEYS, TM), 0).astype(F32)
    kaid = lax.broadcasted_iota(jnp.int32, (K, TM), 0).astype(F32)

    def head(h, carry):
        off = pl.multiple_of(h * 2 * PEER_HALF, 2 * PEER_HALF)
        s1 = _dot_nt(keys_ref[h, 0], q_scr[:, pl.ds(off, PEER_HALF)])
        s2 = _dot_nt(keys_ref[h, 1], q_scr[:, pl.ds(off + PEER_HALF, PEER_HALF)])
        ranks, valss, n_ranked = _top16_untied([s1, s2])
        for half in range(2):
            rk_scr[half] = ranks[half]
            vl_scr[half] = jnp.concatenate(valss[half], axis=0)
        n_max = jnp.maximum(n_ranked[0], n_ranked[1])

        @pl.when(jnp.max(n_max) > float(K))
        def _():
            for half, s in ((0, s1), (1, s2)):
                rank, vals = _top16(s, rowid)
                rk_scr[half] = rank
                vl_scr[half] = jnp.concatenate(vals, axis=0)

        rank1 = rk_scr[0]
        rank2 = rk_scr[1]
        v1 = vl_scr[0]
        vals1 = [v1[r:r + 1] for r in range(K)]
        vals2 = [vl_scr[1, r:r + 1, :] for r in range(K)]
        taken = jnp.zeros((K, TM), F32)
        front = v1 + vals2[0]
        for _ in range(K):
            m = jnp.max(front, axis=0, keepdims=True)
            idx = jnp.min(jnp.where(front == m, kaid, 1e9), axis=0, keepdims=True)
            sel = kaid == idx
            taken = taken + jnp.where(sel, 1.0, 0.0)
            nxt = jnp.full((K, TM), NEG_INF, F32)
            for kb in range(1, K):
                nxt = jnp.where(taken == float(kb), vals2[kb], nxt)
            front = jnp.where(sel, v1 + nxt, front)
        e1k = jnp.exp(v1 - vals1[0])
        pref = jnp.zeros((1, TM), F32)
        zrow = jnp.zeros((K, TM), F32)
        for kb in range(K):
            pref = pref + jnp.exp(vals2[kb] - vals2[0])
            zrow = jnp.where(taken == float(kb + 1), pref, zrow)
        z = jnp.sum(e1k * zrow, axis=0, keepdims=True)
        na = jnp.zeros((PEER_NKEYS, TM), F32)
        for ka in range(K):
            na = jnp.where(rank1 == float(ka), taken[ka:ka + 1], na)
        r2_ref[h] = _bf(rank2)
        na_ref[h] = na
        e1_ref[h] = jnp.where(rank1 < float(K), jnp.exp(s1 - vals1[0]) / z, 0.0)
        e2_ref[h] = _bf(jnp.where(rank2 < float(K), jnp.exp(s2 - vals2[0]), 0.0))
        return carry

    lax.fori_loop(0, PEER_HEADS, head, 0)


def _peer_prep(hn, wq, keys, TM):
    T, D = hn.shape
    Q = wq.shape[1]
    shp = jax.ShapeDtypeStruct((PEER_HEADS, PEER_NKEYS, T), F32)
    shp_bf = jax.ShapeDtypeStruct((PEER_HEADS, PEER_NKEYS, T), BF16)
    ospec = pl.BlockSpec((PEER_HEADS, PEER_NKEYS, TM), lambda i: (0, 0, i))
    return pl.pallas_call(
        _peer_prep_kernel,
        grid=(T // TM,),
        in_specs=[pl.BlockSpec((TM, D), lambda i: (i, 0)),
                  pl.BlockSpec((D, Q), lambda i: (0, 0)),
                  pl.BlockSpec((PEER_HEADS, 2, PEER_NKEYS, PEER_HALF), lambda i: (0, 0, 0, 0))],
        out_specs=[ospec, ospec, ospec, ospec],
        out_shape=[shp_bf, shp, shp, shp_bf],
        scratch_shapes=[pltpu.VMEM((TM, Q), BF16),
                        pltpu.VMEM((2, PEER_NKEYS, TM), F32), pltpu.VMEM((2, PEER_TOPK, TM), F32)],
        compiler_params=_cp("parallel"),
        name="peer_prep",
    )(hn, wq, keys)


def _peer_dense_kernel(h_ref, r2_ref, na_ref, e1_ref, e2_ref, u_ref, vt_ref, x_ref, gt_ref, fg_ref,
                       o_ref, acc_t, ht_scr, s0_scr, s1_scr, aw0_scr, aw1_scr, *, block, n_blocks, final_norm):
    jj = pl.program_id(1)
    TM = h_ref.shape[0]
    SLAB = PEER_NKEYS
    EB = block

    last = pl.num_programs(1) - 1
    zero_bf = jnp.zeros((SLAB, TM), BF16)

    def row_tile(ref, h, a):
        t = _bf(jnp.broadcast_to(ref[h, pl.ds(a, 1), :], (16, TM)))
        return jnp.concatenate([t] * (SLAB // 16), axis=0)

    def vpu_stage(k, s_ref, aw_ref):
        for p in range(EB // (2 * SLAB)):
            lo = p * 2 * SLAB
            s = s_ref[lo:lo + 2 * SLAB, :]
            act = 0.5 * s * (1.0 + lax.erf(s * SQRT_HALF))
            weights = []
            for half in range(2):
                a = k * (EB // SLAB) + p * 2 + half
                w = jnp.zeros((SLAB, TM), BF16)
                for h in range(PEER_HEADS):
                    chosen = r2_ref[h] < row_tile(na_ref, h, a)
                    w = w + jnp.where(chosen, e2_ref[h], zero_bf) * row_tile(e1_ref, h, a)
                weights.append(w)
            aw_ref[lo:lo + 2 * SLAB, :] = act * jnp.concatenate(weights, axis=0)

    def scores(half):
        return _bf(_dot(u_ref[half * EB:(half + 1) * EB, :], ht_scr[...]))

    def outputs(half, aw_ref):
        return _dot(vt_ref[:, half * EB:(half + 1) * EB], aw_ref[...])

    @pl.when(jj == 0)
    def _():
        ht_scr[...] = _bf(h_ref[...].astype(F32).T)
        s0_scr[...] = scores(0)
        s1_scr[...] = scores(1)
        vpu_stage(0, s0_scr, aw0_scr)
        acc_t[...] = jnp.zeros_like(acc_t)

    @pl.when(jnp.logical_and(jj > 0, jj < last))
    def _():
        out0 = outputs(0, aw0_scr)
        vpu_stage(2 * jj - 1, s1_scr, aw1_scr)
        s0_scr[...] = scores(0)
        out1 = outputs(1, aw1_scr)
        s1_scr[...] = scores(1)
        vpu_stage(2 * jj, s0_scr, aw0_scr)
        acc_t[...] += out0 + out1

    @pl.when(jj == last)
    def _():
        out0 = outputs(0, aw0_scr)
        vpu_stage(n_blocks - 1, s1_scr, aw1_scr)
        out1 = outputs(1, aw1_scr)
        xn = x_ref[...] + gt_ref[0] * (acc_t[...] + out0 + out1).T
        o_ref[...] = _rms(xn, fg_ref[...]) if final_norm else xn


def _peer_dense(x, hn, sel, u, vt, gt2, TM, tiles_per_batch, final_g):
    T, D = x.shape
    E = u.shape[0]
    EB = 1024
    final_norm = final_g is not None
    fg = (final_g if final_norm else jnp.ones((D,), F32)).reshape(1, D)
    sspec = pl.BlockSpec((PEER_HEADS, PEER_NKEYS, TM), lambda i, e: (0, 0, i))
    nE = E // EB
    nP = nE // 2
    kern = functools.partial(_peer_dense_kernel, block=EB, n_blocks=nE, final_norm=final_norm)
    return pl.pallas_call(
        kern,
        grid=(T // TM, nP + 1),
        in_specs=[pl.BlockSpec((TM, D), lambda i, e: (i, 0)),
                  sspec, sspec, sspec, sspec,
                  pl.BlockSpec((2 * EB, D), lambda i, e: (jnp.minimum(e, nP - 1), 0)),
                  pl.BlockSpec((D, 2 * EB), lambda i, e: (0, jnp.maximum(e - 1, 0))),
                  pl.BlockSpec((TM, D), lambda i, e: (i, 0)),
                  pl.BlockSpec((1, 1, D), lambda i, e: (i // tiles_per_batch, 0, 0)),
                  pl.BlockSpec((1, D), lambda i, e: (0, 0))],
        out_specs=pl.BlockSpec((TM, D), lambda i, e: (i, 0)),
        out_shape=jax.ShapeDtypeStruct((T, D), F32),
        scratch_shapes=[pltpu.VMEM((D, TM), F32), pltpu.VMEM((D, TM), BF16),
                        pltpu.VMEM((EB, TM), BF16), pltpu.VMEM((EB, TM), BF16),
                        pltpu.VMEM((EB, TM), BF16), pltpu.VMEM((EB, TM), BF16)],
        compiler_params=_cp("parallel", "arbitrary"),
        name="peer_dense",
    )(hn, *sel, u, vt, x, gt2, fg)


def _direction_masks():
    t = jnp.arange(CHUNK)
    le = (t[None, :] <= t[:, None]).astype(F32)
    tri = jnp.stack([le, le.T])
    i = jnp.arange(GROUP_W)
    same = (i[:, None] // CHUNK) == (i[None, :] // CHUNK)
    ti, tj = i[:, None] % CHUNK, i[None, :] % CHUNK
    msl = jnp.stack([same & (tj < ti), same & (tj > ti)]).astype(F32)
    minc = jnp.stack([same & (tj <= ti), same & (tj >= ti)]).astype(F32)
    return tri, msl, minc


def _layer_params(l, w_in, shift_mu, w0, w2, a0, a2, g2, k_k, k_a, r_k, lnx_g, lnx_b, w_oA, conv_w,
                  cnorm_g, cnorm_b, w_oB, gate_b, w_out, norm2_g):
    D = D_MODEL
    row = lambda t: t.reshape(1, -1)
    zeros = jnp.zeros((LORA_W, D), F32)
    tri, msl, minc = _direction_masks()
    hs = (jnp.arange(D)[:, None] // HEAD == jnp.arange(128)[None, :]).astype(F32)
    pad_cols = P_RWKV_PAD - P_RWKV
    return dict(
        w_rk=_bf(jnp.pad(w_in[l][:, :P_RWKV], ((0, 0), (0, pad_cols)))),
        w_cg=_bf(w_in[l][:, P_RWKV:]),
        mu=jnp.pad(shift_mu[l], ((0, 0), (0, pad_cols))),
        w0=w0[l].reshape(2, 1, D), a0=a0[l].reshape(2, 1, D),
        w2p=_split(jnp.stack([jnp.concatenate([w2[l, 0], zeros]), jnp.concatenate([zeros, w2[l, 1]])])),
        a2p=_split(jnp.stack([jnp.concatenate([a2[l, 0], zeros]), jnp.concatenate([zeros, a2[l, 1]])])),
        g2p=_bf(jnp.pad(g2[l], ((0, LORA_G_PAD - LORA_G), (0, 0)))),
        k_k=row(k_k[l]), k_a=row(k_a[l]), r_k=row(r_k[l]),
        lnx_g=row(lnx_g[l]), lnx_b=row(lnx_b[l]), w_oA=_bf(w_oA[l]),
        conv_w=conv_w[l], cnorm_g=row(cnorm_g[l]), cnorm_b=row(cnorm_b[l]), w_oB=_bf(w_oB[l]),
        gate_b=row(gate_b[l]), w_out=_bf(w_out[l]), norm2_g=row(norm2_g[l]),
        hs=_bf(hs), hb=_bf(hs.T), tri=_bf(tri), msl=msl, minc=minc)


def _mixer(x, mod, pr, norm1_g, stride, h0, emit):
    B, L, D = x.shape
    sh1, sc1, gt1, sh2, sc2 = (mod[:, i:i + 1, :] for i in range(5))
    z_rk = _proj_shift(x, sh1, sc1, norm1_g, pr["w_rk"], pr["mu"])
    y, bv, gs, h_t = _rwkv(z_rk, pr, h0)
    if not emit:
        return None, None, h_t
    z_cg = _proj(x, sh1, sc1, norm1_g, pr["w_cg"])
    cv = _conv(z_cg, pr["conv_w"], stride)
    xn, hn = _post(x, y, bv, gs, cv, z_cg, pr, gt1, sh2, sc2)
    return xn, hn, h_t


def _peer(x, hn, wq, keys, u, v, gt2, final_g=None):
    B, L, D = x.shape
    T = B * L
    TM = min(L, 256)
    sel = _peer_prep(hn.reshape(T, D), wq, keys, TM)
    out = _peer_dense(x.reshape(T, D), hn.reshape(T, D), sel, u, v, gt2, TM, L // TM, final_g)
    return out.reshape(B, L, D)


def kernel(x, c, ctx, c_ctx, ada_w, ada_b, norm1_g, norm2_g, w_in, shift_mu, w0, w2, a0, a2, g2, k_k, k_a, r_k, lnx_g, lnx_b, w_oA, conv_w, cnorm_g, cnorm_b, w_oB, gate_b, w_out, w_q, sub_keys, peer_u, peer_v, final_g):
    B, L, D = x.shape
    depth = ada_w.shape[0]
    xc = ctx
    n_rows = -(-(B + 1) // 8) * 8
    c_rows = jnp.pad(jnp.concatenate([c, c_ctx[None, :]], axis=0), ((0, n_rows - B - 1), (0, 0)))
    zero_state = jnp.zeros((2, B, N_GROUPS, HEAD, GROUP_W), F32)
    for l in range(depth):
        last = l == depth - 1
        pr = _layer_params(l, w_in, shift_mu, w0, w2, a0, a2, g2, k_k, k_a, r_k, lnx_g, lnx_b, w_oA,
                           conv_w, cnorm_g, cnorm_b, w_oB, gate_b, w_out, norm2_g)
        mod_all = _modulation(c_rows, ada_w[l], ada_b[l])
        mod = mod_all[:B].reshape(B, 6, D)
        modc = jnp.broadcast_to(mod_all[B].reshape(1, 6, D), (B, 6, D))
        wq = _bf(w_q[l])
        keys = _bf(sub_keys[l])
        u = _bf(peer_u[l])
        v = _bf(peer_v[l]).T

        xc_new, hnc, ctx_states = _mixer(xc, modc, pr, norm1_g[l], 1, zero_state, emit=not last)
        xn, hn, _ = _mixer(x, mod, pr, norm1_g[l], GRID_W, ctx_states, emit=True)
        x = _peer(xn, hn, wq, keys, u, v, mod[:, 5:6, :], final_g if last else None)
        if not last:
            xc = _peer(xc_new, hnc, wq, keys, u, v, modc[:, 5:6, :])
    return x
```

```python
import functools

import jax
import jax.numpy as jnp
from jax import lax
from jax.experimental import pallas as pl
from jax.experimental.pallas import tpu as pltpu

F32 = jnp.float32
BF16 = jnp.bfloat16
HI = lax.Precision.HIGHEST

D_MODEL = 1024
HEAD = 64
HEADS = D_MODEL // HEAD
GROUP_HEADS = 4
GROUP_W = GROUP_HEADS * HEAD
N_GROUPS = HEADS // GROUP_HEADS
CHUNK = 64
assert CHUNK == HEAD
RWKV_ROWS_PER_STEP = 4
LORA_W = 64
LORA_A = 64
LORA_G = 160
LORA_G_PAD = 256
P_RWKV = 3 * D_MODEL + 2 * LORA_W + 2 * LORA_A + LORA_G
P_RWKV_PAD = 3 * D_MODEL + 2 * LORA_W + 2 * LORA_A + LORA_G_PAD
COL_W1 = 3 * D_MODEL
COL_A1 = COL_W1 + 2 * LORA_W
COL_G1 = COL_A1 + 2 * LORA_A
CONV_K = 31
CONV_HALF = CONV_K // 2
GRID_W = 64
PEER_HEADS = 8
PEER_NKEYS = 128
PEER_HALF = 128
PEER_TOPK = 16
PEER_LANE_TILE = 256
PEER_DENSE_TOKENS = 512
NORM_EPS = 1e-6
LN_EPS = 1e-5
GN_EPS = HEAD * 1e-5
VMEM_LIMIT = 56 * 1024 * 1024
NOT_SELECTED = 99.0
NEG_INF = float("-inf")
SQRT_HALF = 0.7071067811865476
EXP_MINUS_HALF = 0.6065306597126334


def _cp(*sem):
    return pltpu.CompilerParams(dimension_semantics=sem, vmem_limit_bytes=VMEM_LIMIT)


def _dot(a, b):
    return jnp.dot(a, b, preferred_element_type=F32)


def _dot_hi(a, b):
    return jnp.dot(a, b, precision=HI, preferred_element_type=F32)


def _dot_nt(a, b):
    return lax.dot_general(a, b, (((1,), (1,)), ((), ())), preferred_element_type=F32)


def _dot_tn(a, b):
    return lax.dot_general(a, b, (((0,), (0,)), ((), ())), preferred_element_type=F32)


def _bf(a):
    return a.astype(BF16)


def _split(a):
    hi = a.astype(BF16)
    return hi, (a - hi.astype(F32)).astype(BF16)


def _dot_split_lhs(a, b_bf):
    hi, lo = _split(a)
    return _dot(hi, b_bf) + _dot(lo, b_bf)


def _dot_split(a, b_hi, b_lo):
    hi, lo = _split(a)
    return _dot(hi, b_hi) + _dot(hi, b_lo) + _dot(lo, b_hi)


def _head_sum(t, hs_bf, hb_bf):
    return _dot_split_lhs(_dot_split_lhs(t, hs_bf), hb_bf)


def _sigmoid(x):
    return 1.0 / (1.0 + jnp.exp(-x))


def _rms(x, g):
    return x * lax.rsqrt(jnp.mean(x * x, axis=-1, keepdims=True) + NORM_EPS) * g


def _mod_kernel(c_ref, w_ref, b_ref, o_ref):
    c = c_ref[...]
    o_ref[...] = _dot_hi(c * _sigmoid(c), w_ref[...]) + b_ref[...]


def _modulation(c_rows, ada_w, ada_b):
    R, D = c_rows.shape
    N = ada_w.shape[1]
    TN = 512
    return pl.pallas_call(
        _mod_kernel,
        grid=(N // TN,),
        in_specs=[pl.BlockSpec((R, D), lambda j: (0, 0)),
                  pl.BlockSpec((D, TN), lambda j: (0, j)),
                  pl.BlockSpec((1, TN), lambda j: (0, j))],
        out_specs=pl.BlockSpec((R, TN), lambda j: (0, j)),
        out_shape=jax.ShapeDtypeStruct((R, N), F32),
        compiler_params=_cp("arbitrary"),
        name="modulation",
    )(c_rows, ada_w, ada_b.reshape(1, N))


def _proj_kernel(x_ref, sh_ref, sc_ref, g_ref, w_ref, o_ref):
    h = _rms(x_ref[0], g_ref[...]) * (1.0 + sc_ref[0]) + sh_ref[0]
    o_ref[0] = _dot(_bf(h), w_ref[...])


def _proj(x, sh, sc, g, w):
    B, L, D = x.shape
    N = w.shape[1]
    TM = min(L, 256)
    return pl.pallas_call(
        _proj_kernel,
        grid=(B, L // TM),
        in_specs=[pl.BlockSpec((1, TM, D), lambda b, i: (b, i, 0)),
                  pl.BlockSpec((1, 1, D), lambda b, i: (b, 0, 0)),
                  pl.BlockSpec((1, 1, D), lambda b, i: (b, 0, 0)),
                  pl.BlockSpec((1, D), lambda b, i: (0, 0)),
                  pl.BlockSpec((D, N), lambda b, i: (0, 0))],
        out_specs=pl.BlockSpec((1, TM, N), lambda b, i: (b, i, 0)),
        out_shape=jax.ShapeDtypeStruct((B, L, N), F32),
        compiler_params=_cp("parallel", "parallel"),
        name="proj",
    )(x, sh, sc, g.reshape(1, D), w)


def _proj_shift_kernel(x_ref, xp_ref, xn_ref, sh_ref, sc_ref, g_ref, w_ref, mu_ref, o_ref):
    i = pl.program_id(1)
    TM = x_ref.shape[1]
    xs = jnp.concatenate([xp_ref[0], x_ref[0], xn_ref[0]], axis=0)
    h = _rms(xs, g_ref[...]) * (1.0 + sc_ref[0]) + sh_ref[0]
    z = _dot(_bf(h), w_ref[...])
    row = lax.broadcasted_iota(jnp.int32, (TM, 1), 0)
    at_start = jnp.logical_and(row == 0, i == 0)
    at_end = jnp.logical_and(row == TM - 1, i == pl.num_programs(1) - 1)
    zp = jnp.where(at_start, 0.0, z[7:7 + TM])
    zn = jnp.where(at_end, 0.0, z[9:9 + TM])
    m0 = mu_ref[0:1, :]
    m1 = mu_ref[1:2, :]
    o_ref[0] = z[8:8 + TM] * (1.0 - m0 - m1) + m0 * zp + m1 * zn


def _proj_shift(x, sh, sc, g, w, mu):
    B, L, D = x.shape
    N = w.shape[1]
    TM = min(L, 256)
    nb8 = L // 8
    return pl.pallas_call(
        _proj_shift_kernel,
        grid=(B, L // TM),
        in_specs=[pl.BlockSpec((1, TM, D), lambda b, i: (b, i, 0)),
                  pl.BlockSpec((1, 8, D), lambda b, i: (b, jnp.maximum(i * (TM // 8) - 1, 0), 0)),
                  pl.BlockSpec((1, 8, D), lambda b, i: (b, jnp.minimum((i + 1) * (TM // 8), nb8 - 1), 0)),
                  pl.BlockSpec((1, 1, D), lambda b, i: (b, 0, 0)),
                  pl.BlockSpec((1, 1, D), lambda b, i: (b, 0, 0)),
                  pl.BlockSpec((1, D), lambda b, i: (0, 0)),
                  pl.BlockSpec((D, N), lambda b, i: (0, 0)),
                  pl.BlockSpec((2, N), lambda b, i: (0, 0))],
        out_specs=pl.BlockSpec((1, TM, N), lambda b, i: (b, i, 0)),
        out_shape=jax.ShapeDtypeStruct((B, L, N), F32),
        compiler_params=_cp("parallel", "parallel"),
        name="proj_shift",
    )(x, x, x, sh, sc, g.reshape(1, D), w, mu)


_KAP, _RT, _KT, _BT, _KH, _BH, _V = range(7)


def _rwkv_kernel(z_ref, w0_ref, w2h_ref, w2l_ref, a0_ref, a2h_ref, a2l_ref,
                 kk_ref, ka_ref, rk_ref, hs_ref, hb_ref, tri_ref, msl_ref, minc_ref, h0_ref,
                 y_ref, bv_ref, gs_ref, hT_ref, H_scr, nat_scr, pc_scr, *, n_chunks):
    d = pl.program_id(0)
    c = pl.program_id(2)
    cc = jnp.where(d == 0, c, n_chunks - 1 - c)
    C = CHUNK

    R = z_ref.shape[0]

    @pl.when(c == 0)
    def _():
        H_scr[...] = h0_ref[0]

    hs = hs_ref[...]
    hb = hb_ref[...]

    def head_sum(t):
        return _head_sum(t, hs, hb)

    def prepare(rr):
        def shifted(lo, hi):
            return z_ref[rr, :, lo:hi]

        r = shifted(0, D_MODEL)
        k = shifted(D_MODEL, 2 * D_MODEL)
        v = shifted(2 * D_MODEL, 3 * D_MODEL)
        w1 = shifted(COL_W1, COL_A1)
        a1 = shifted(COL_A1, COL_G1)
        g1 = shifted(COL_G1, P_RWKV_PAD)

        wl = w0_ref[0] + _dot_split(jnp.tanh(w1), w2h_ref[0], w2l_ref[0])
        logw = -EXP_MINUS_HALF * _sigmoid(wl)
        a = _sigmoid(_dot_split(a1, a2h_ref[0], a2l_ref[0]) + a0_ref[0])
        kkr = k * kk_ref[...]
        inv_norm = lax.rsqrt(jnp.maximum(_dot_split_lhs(kkr * kkr, hs), 1e-24))
        kk = kkr * _dot_split_lhs(inv_norm, hb)
        kd = k * (1.0 + (a - 1.0) * ka_ref[...])
        bb = kk * a
        bv_ref[0, rr] = head_sum(r * kd * rk_ref[...]) * v
        gs_ref[0, rr] = _sigmoid(g1)

        lw_hi, lw_lo = _split(logw)
        g_in = _dot(tri_ref[0], lw_hi) + _dot(tri_ref[0], lw_lo)
        g_ex = g_in - logw
        g_c = jnp.sum(logw, axis=0, keepdims=True)
        e_inv = jnp.exp(-g_in)
        e_hat = jnp.exp(g_c - g_in)
        nat_scr[rr, _KAP] = kk * jnp.exp(g_ex)
        nat_scr[rr, _RT] = r * jnp.exp(g_in)
        nat_scr[rr, _KT] = kd * e_inv
        nat_scr[rr, _BT] = bb * e_inv
        nat_scr[rr, _KH] = kd * e_hat
        nat_scr[rr, _BH] = bb * e_hat
        nat_scr[rr, _V] = v
        pc_scr[rr] = jnp.exp(g_c)

    for rr in range(R):
        prepare(rr)

    lane_head = lax.broadcasted_iota(jnp.int32, (C, GROUP_W), 1) // HEAD
    ii = lax.broadcasted_iota(jnp.int32, (GROUP_W, GROUP_W), 0)
    jj = lax.broadcasted_iota(jnp.int32, (GROUP_W, GROUP_W), 1)
    eye = ii == jj

    def stacked(t):
        return jnp.concatenate([jnp.where(lane_head == j, t, 0.0) for j in range(GROUP_HEADS)], axis=0)

    def collapse(t):
        return t[0:C] + t[C:2 * C] + t[2 * C:3 * C] + t[3 * C:4 * C]

    G = range(R * N_GROUPS)

    def lanes(q):
        return slice((q % N_GROUPS) * GROUP_W, (q % N_GROUPS + 1) * GROUP_W)

    def nat(i, q):
        return nat_scr[q // N_GROUPS, i, :, lanes(q)]

    x_kap = [stacked(nat(_KAP, g)) for g in G]
    x_v = [_bf(stacked(nat(_V, g))) for g in G]
    x_bk = [jnp.concatenate([_bf(stacked(nat(_BT, g))), _bf(stacked(nat(_KT, g)))], axis=0) for g in G]
    kr = [_bf(jnp.concatenate([nat(_KAP, g), nat(_RT, g)], axis=0)) for g in G]
    akr = [_dot_nt(kr[g], x_bk[g]) for g in G]

    def tiled(t):
        return jnp.concatenate([t] * GROUP_HEADS, axis=0)

    same_head = (ii // C) == (jj // C)

    def block_diag(side):
        t = tiled(_bf(side))
        return jnp.where(same_head, t, jnp.zeros_like(t))

    msl_c = collapse(msl_ref[0]) > 0.5
    minc_c = collapse(minc_ref[0]) > 0.5
    n_side = [jnp.where(msl_c, akr[g][:C, :GROUP_W], 0.0) for g in G]
    a_kk = [_bf(jnp.where(msl_c, akr[g][:C, GROUP_W:], 0.0)) for g in G]
    a_rb = [_bf(jnp.where(minc_c, akr[g][C:, :GROUP_W], 0.0)) for g in G]
    a_rk = [_bf(jnp.where(minc_c, akr[g][C:, GROUP_W:], 0.0)) for g in G]
    g0 = [stacked(_dot(a_kk[g], x_v[g])) for g in G]
    y0 = [_dot(a_rk[g], x_v[g]) for g in G]
    eye_side = collapse(jnp.where(eye, 1.0, 0.0))
    p_inv = [eye_side - n_side[g] for g in G]
    m_pow = [_dot(_bf(n_side[g]), block_diag(n_side[g])) for g in G]
    for _ in range(4):
        both = [_dot(_bf(jnp.concatenate([m_pow[g], p_inv[g]], axis=0)), block_diag(m_pow[g])) for g in G]
        m_pow = [both[g][:C] for g in G]
        p_inv = [p_inv[g] + both[g][C:] for g in G]
    p_inv = [p_inv[g] + _dot(_bf(p_inv[g]), block_diag(m_pow[g])) for g in G]
    w_nat = [_dot(_bf(p_inv[g]), jnp.concatenate([_bf(x_kap[g]), _bf(g0[g])], axis=1)) for g in G]
    w12 = [_bf(jnp.concatenate([stacked(w_nat[g][:, :GROUP_W]), stacked(w_nat[g][:, GROUP_W:])], axis=1))
           for g in G]
    aw = [_dot(a_rb[g], w12[g]) for g in G]
    def heads_transposed(t):
        tt = t.T
        return jnp.concatenate([tt[j * HEAD:(j + 1) * HEAD] for j in range(GROUP_HEADS)], axis=1)

    bw = [_dot(_bf(heads_transposed(nat(_BH, g))), w12[g]) for g in G]
    kv = [_dot(_bf(heads_transposed(nat(_KH, g))), x_v[g]) for g in G]
    h_prev = [H_scr[g // N_GROUPS, g % N_GROUPS] for g in G]
    ys = []
    h_new = []
    for g in G:
        h_hi = block_diag(h_prev[g])
        h_lo = block_diag(h_prev[g] - _bf(h_prev[g]).astype(F32))
        qb = _bf(nat(_RT, g) - aw[g][:, :GROUP_W])
        m_side = eye_side * pc_scr[g // N_GROUPS, :, lanes(g)] - bw[g][:, :GROUP_W]
        m_hi, m_lo = _split(m_side)
        by_hi = _dot(jnp.concatenate([m_hi, m_lo, qb], axis=0), h_hi)
        by_lo = _dot(jnp.concatenate([m_hi, qb], axis=0), h_lo)
        ys.append(by_hi[2 * C:] + by_lo[C:] + y0[g] - aw[g][:, GROUP_W:])
        h_new.append(by_hi[:C] + by_hi[C:2 * C] + by_lo[:C] + kv[g] - bw[g][:, GROUP_W:])
    for rr in range(R):
        y_ref[0, rr] = jnp.concatenate(ys[rr * N_GROUPS:(rr + 1) * N_GROUPS], axis=1)
    for g in G:
        H_scr[g // N_GROUPS, g % N_GROUPS] = h_new[g]

    @pl.when(c == n_chunks - 1)
    def _():
        hT_ref[0] = H_scr[...]


def _rwkv(z, pr, h0):
    B, L, N = z.shape
    C = CHUNK
    nC = L // C
    D = D_MODEL

    def cidx(d, c):
        return c + d * (nC - 1 - 2 * c)

    zmap = lambda d, b, c: (b, cidx(d, c), 0)
    const2 = lambda d, b, c: (0, 0)
    dir3 = lambda d, b, c: (d, 0, 0)
    omap = lambda d, b, c: (d, b, cidx(d, c), 0)
    smap = lambda d, b, c: (d, b, 0, 0, 0)
    R = RWKV_ROWS_PER_STEP if B % RWKV_ROWS_PER_STEP == 0 else 1
    kern = functools.partial(_rwkv_kernel, n_chunks=nC)
    return pl.pallas_call(
        kern,
        grid=(2, B // R, nC),
        in_specs=[pl.BlockSpec((R, C, N), zmap),
                  pl.BlockSpec((1, 1, D), dir3),
                  pl.BlockSpec((1, 2 * LORA_W, D), dir3),
                  pl.BlockSpec((1, 2 * LORA_W, D), dir3),
                  pl.BlockSpec((1, 1, D), dir3),
                  pl.BlockSpec((1, 2 * LORA_A, D), dir3),
                  pl.BlockSpec((1, 2 * LORA_A, D), dir3),
                  pl.BlockSpec((1, D), const2),
                  pl.BlockSpec((1, D), const2),
                  pl.BlockSpec((1, D), const2),
                  pl.BlockSpec((D, 128), const2),
                  pl.BlockSpec((128, D), const2),
                  pl.BlockSpec((1, C, C), dir3),
                  pl.BlockSpec((1, GROUP_W, GROUP_W), dir3),
                  pl.BlockSpec((1, GROUP_W, GROUP_W), dir3),
                  pl.BlockSpec((1, R, N_GROUPS, HEAD, GROUP_W), smap)],
        out_specs=[pl.BlockSpec((1, R, C, D), omap),
                   pl.BlockSpec((1, R, C, D), omap),
                   pl.BlockSpec((1, R, C, LORA_G_PAD), omap),
                   pl.BlockSpec((1, R, N_GROUPS, HEAD, GROUP_W), smap)],
        out_shape=[jax.ShapeDtypeStruct((2, B, L, D), F32),
                   jax.ShapeDtypeStruct((2, B, L, D), F32),
                   jax.ShapeDtypeStruct((2, B, L, LORA_G_PAD), F32),
                   jax.ShapeDtypeStruct((2, B, N_GROUPS, HEAD, GROUP_W), F32)],
        scratch_shapes=[pltpu.VMEM((R, N_GROUPS, HEAD, GROUP_W), F32),
                        pltpu.VMEM((R, 7, C, D), F32),
                        pltpu.VMEM((R, 1, D), F32)],
        compiler_params=_cp("arbitrary", "arbitrary", "arbitrary"),
        name="rwkv",
    )(z, pr["w0"], *pr["w2p"], pr["a0"], *pr["a2p"], pr["k_k"], pr["k_a"], pr["r_k"],
      pr["hs"], pr["hb"], pr["tri"], pr["msl"], pr["minc"], h0)


def _conv_kernel(za_ref, zb_ref, w_ref, o_ref, upad, *, L, stride, rows_per_step):
    pad = CONV_HALF * stride
    TC = za_ref.shape[-1]
    upad[0:pad, :] = jnp.zeros((pad, TC), F32)
    upad[pad + L:pad + L + pad, :] = jnp.zeros((pad, TC), F32)
    upad[pad:pad + L, :] = za_ref[0] * _sigmoid(zb_ref[0])
    RB = rows_per_step

    def block(r0):
        acc = jnp.zeros((RB, TC), F32)
        for j in range(CONV_K):
            acc = acc + w_ref[j:j + 1, :] * upad[pl.ds(r0 + j * stride, RB), :]
        o_ref[0, pl.ds(r0, RB), :] = acc

    if stride % 8 == 0:
        def body(i, carry):
            block(pl.multiple_of(i * RB, RB))
            return carry
        lax.fori_loop(0, L // RB, body, 0)
    else:
        for i in range(L // RB):
            block(i * RB)


def _conv(zcg, conv_w, stride):
    B, L, _ = zcg.shape
    D = D_MODEL
    TC = 128
    nct = D // TC
    wpad = jnp.pad(conv_w, ((0, 32 - CONV_K), (0, 0)))
    RB = min(L, 128)
    kern = functools.partial(_conv_kernel, L=L, stride=stride, rows_per_step=RB)
    return pl.pallas_call(
        kern,
        grid=(B, nct),
        in_specs=[pl.BlockSpec((1, L, TC), lambda b, j: (b, 0, j)),
                  pl.BlockSpec((1, L, TC), lambda b, j: (b, 0, j + nct)),
                  pl.BlockSpec((32, TC), lambda b, j: (0, j))],
        out_specs=pl.BlockSpec((1, L, TC), lambda b, j: (b, 0, j)),
        out_shape=jax.ShapeDtypeStruct((B, L, D), F32),
        scratch_shapes=[pltpu.VMEM((L + 2 * CONV_HALF * stride, TC), F32)],
        compiler_params=_cp("parallel", "parallel"),
        name="conv",
    )(zcg, zcg, wpad)


def _post_kernel(x_ref, yf_ref, yb_ref, bf_ref, bb_ref, gs_ref, cv_ref, zg_ref,
                 lng_ref, lnb_ref, g2_ref, woa_ref, cng_ref, cnb_ref, wob_ref, gb_ref, wout_ref,
                 gt_ref, n2g_ref, sh2_ref, sc2_ref, hs_ref, hb_ref, xo_ref, hn_ref):
    D = D_MODEL
    hs = hs_ref[...]
    hb = hb_ref[...]

    def head_mean(t):
        return _head_sum(t, hs, hb) * (1.0 / HEAD)

    o = yf_ref[0, 0] + yb_ref[0, 0]
    oc = o - head_mean(o)
    on = oc * lax.rsqrt(head_mean(oc * oc) + GN_EPS) * lng_ref[...] + lnb_ref[...]
    on = on + bf_ref[0, 0] + bb_ref[0, 0]
    gate = _dot(_bf(gs_ref[0, 0]), g2_ref[...])
    y_a = _dot(_bf(on * gate), woa_ref[...])

    cv = cv_ref[0]
    cm = jnp.mean(cv, axis=-1, keepdims=True)
    cc = cv - cm
    cn = cc * lax.rsqrt(jnp.mean(cc * cc, axis=-1, keepdims=True) + LN_EPS) * cng_ref[...] + cnb_ref[...]
    y_b = _dot(_bf(cn * _sigmoid(cn)), wob_ref[...])

    gates = _sigmoid(zg_ref[0] + gb_ref[...])
    m = gates[:, :D] * y_a + gates[:, D:] * y_b
    xn = x_ref[0] + gt_ref[0] * _dot(_bf(m), wout_ref[...])
    xo_ref[0] = xn
    hn_ref[0] = _bf(_rms(xn, n2g_ref[...]) * (1.0 + sc2_ref[0]) + sh2_ref[0])


def _post(x, y, bv, gs, cv, zcg, pr, gt1, sh2, sc2):
    B, L, D = x.shape
    TM = min(L, 256)
    tok = lambda b, i: (b, i, 0)
    fwd = lambda b, i: (0, b, i, 0)
    bwd = lambda b, i: (1, b, i, 0)
    cst = lambda b, i: (0, 0)
    per_b = lambda b, i: (b, 0, 0)
    row = pl.BlockSpec((1, D), cst)
    mat = pl.BlockSpec((D, D), cst)
    return pl.pallas_call(
        _post_kernel,
        grid=(B, L // TM),
        in_specs=[pl.BlockSpec((1, TM, D), tok),
                  pl.BlockSpec((1, 1, TM, D), fwd), pl.BlockSpec((1, 1, TM, D), bwd),
                  pl.BlockSpec((1, 1, TM, D), fwd), pl.BlockSpec((1, 1, TM, D), bwd),
                  pl.BlockSpec((1, 1, TM, LORA_G_PAD), fwd),
                  pl.BlockSpec((1, TM, D), tok),
                  pl.BlockSpec((1, TM, 2 * D), lambda b, i: (b, i, 1)),
                  row, row, pl.BlockSpec((LORA_G_PAD, D), cst), mat,
                  row, row, mat, pl.BlockSpec((1, 2 * D), cst), mat,
                  pl.BlockSpec((1, 1, D), per_b), row,
                  pl.BlockSpec((1, 1, D), per_b), pl.BlockSpec((1, 1, D), per_b),
                  pl.BlockSpec((D, 128), cst), pl.BlockSpec((128, D), cst)],
        out_specs=[pl.BlockSpec((1, TM, D), tok), pl.BlockSpec((1, TM, D), tok)],
        out_shape=[jax.ShapeDtypeStruct((B, L, D), F32), jax.ShapeDtypeStruct((B, L, D), BF16)],
        compiler_params=_cp("parallel", "parallel"),
        name="post",
    )(x, y, y, bv, bv, gs, cv, zcg,
      pr["lnx_g"], pr["lnx_b"], pr["g2p"], pr["w_oA"], pr["cnorm_g"], pr["cnorm_b"], pr["w_oB"],
      pr["gate_b"], pr["w_out"], gt1, pr["norm2_g"], sh2, sc2, pr["hs"], pr["hb"])


def _top16(s, rowid):
    rank = jnp.full(s.shape, NOT_SELECTED, F32)
    cur = s
    vals = []
    for r in range(PEER_TOPK):
        m = jnp.max(cur, axis=0, keepdims=True)
        idx = jnp.min(jnp.where(cur == m, rowid, 1e9), axis=0, keepdims=True)
        sel = rowid == idx
        rank = jnp.where(sel, float(r), rank)
        cur = jnp.where(sel, NEG_INF, cur)
        vals.append(m)
    return rank, vals


def _top16_untied(arrays):
    n = range(len(arrays))
    rank = [jnp.full(s.shape, NOT_SELECTED, F32) for s in arrays]
    cur = list(arrays)
    vals = [[] for _ in n]
    for r in range(PEER_TOPK):
        m = [jnp.max(cur[i], axis=0, keepdims=True) for i in n]
        sel = [cur[i] == m[i] for i in n]
        rank = [jnp.where(sel[i], float(r), rank[i]) for i in n]
        cur = [jnp.where(sel[i], NEG_INF, cur[i]) for i in n]
        for i in n:
            vals[i].append(m[i])
    n_ranked = [jnp.sum(jnp.where(rank[i] < float(PEER_TOPK), 1.0, 0.0), axis=0, keepdims=True) for i in n]
    return rank, vals, n_ranked


def _peer_prep_kernel(h_ref, wq_ref, keys_ref, r2_ref, na_ref, e1_ref, e2_ref, q_scr, rk_scr, vl_scr):
    TM = h_ref.shape[0]
    K = PEER_TOPK
    q_scr[...] = _bf(_dot(h_ref[...], wq_ref[...]))
    rowid = lax.broadcasted_iota(jnp.int32, (PEER_NKEYS, TM), 0).astype(F32)
    kaid = lax.broadcasted_iota(jnp.int32, (K, TM), 0).astype(F32)

    def head(h, carry):
        off = pl.multiple_of(h * 2 * PEER_HALF, 2 * PEER_HALF)
        s1 = _dot_nt(keys_ref[h, 0], q_scr[:, pl.ds(off, PEER_HALF)])
        s2 = _dot_nt(keys_ref[h, 1], q_scr[:, pl.ds(off + PEER_HALF, PEER_HALF)])
        ranks, valss, n_ranked = _top16_untied([s1, s2])
        for half in range(2):
            rk_scr[half] = ranks[half]
            vl_scr[half] = jnp.concatenate(valss[half], axis=0)
        n_max = jnp.maximum(n_ranked[0], n_ranked[1])

        @pl.when(jnp.max(n_max) > float(K))
        def _():
            for half, s in ((0, s1), (1, s2)):
                rank, vals = _top16(s, rowid)
                rk_scr[half] = rank
                vl_scr[half] = jnp.concatenate(vals, axis=0)

        rank1 = rk_scr[0]
        rank2 = rk_scr[1]
        v1 = vl_scr[0]
        vals1 = [v1[r:r + 1] for r in range(K)]
        vals2 = [vl_scr[1, r:r + 1, :] for r in range(K)]
        taken = jnp.zeros((K, TM), F32)
        front = v1 + vals2[0]
        for _ in range(K):
            m = jnp.max(front, axis=0, keepdims=True)
            idx = jnp.min(jnp.where(front == m, kaid, 1e9), axis=0, keepdims=True)
            sel = kaid == idx
            taken = taken + jnp.where(sel, 1.0, 0.0)
            nxt = jnp.full((K, TM), NEG_INF, F32)
            for kb in range(1, K):
                nxt = jnp.where(taken == float(kb), vals2[kb], nxt)
            front = jnp.where(sel, v1 + nxt, front)
        e1k = jnp.exp(v1 - vals1[0])
        pref = jnp.zeros((1, TM), F32)
        zrow = jnp.zeros((K, TM), F32)
        for kb in range(K):
            pref = pref + jnp.exp(vals2[kb] - vals2[0])
            zrow = jnp.where(taken == float(kb + 1), pref, zrow)
        z = jnp.sum(e1k * zrow, axis=0, keepdims=True)
        na = jnp.zeros((PEER_NKEYS, TM), F32)
        for ka in range(K):
            na = jnp.where(rank1 == float(ka), taken[ka:ka + 1], na)
        r2_ref[h] = _bf(rank2)
        na_ref[h] = na
        e1_ref[h] = jnp.where(rank1 < float(K), jnp.exp(s1 - vals1[0]) / z, 0.0)
        e2_ref[h] = _bf(jnp.where(rank2 < float(K), jnp.exp(s2 - vals2[0]), 0.0))
        return carry

    lax.fori_loop(0, PEER_HEADS, head, 0)


def _peer_prep(hn, wq, keys, TM):
    T, D = hn.shape
    Q = wq.shape[1]
    shp = jax.ShapeDtypeStruct((PEER_HEADS, PEER_NKEYS, T), F32)
    shp_bf = jax.ShapeDtypeStruct((PEER_HEADS, PEER_NKEYS, T), BF16)
    ospec = pl.BlockSpec((PEER_HEADS, PEER_NKEYS, TM), lambda i: (0, 0, i))
    return pl.pallas_call(
        _peer_prep_kernel,
        grid=(T // TM,),
        in_specs=[pl.BlockSpec((TM, D), lambda i: (i, 0)),
                  pl.BlockSpec((D, Q), lambda i: (0, 0)),
                  pl.BlockSpec((PEER_HEADS, 2, PEER_NKEYS, PEER_HALF), lambda i: (0, 0, 0, 0))],
        out_specs=[ospec, ospec, ospec, ospec],
        out_shape=[shp_bf, shp, shp, shp_bf],
        scratch_shapes=[pltpu.VMEM((TM, Q), BF16),
                        pltpu.VMEM((2, PEER_NKEYS, TM), F32), pltpu.VMEM((2, PEER_TOPK, TM), F32)],
        compiler_params=_cp("parallel"),
        name="peer_prep",
    )(hn, wq, keys)


def _peer_dense_kernel(h_ref, r2_ref, na_ref, e1_ref, e2_ref, u_ref, vt_ref, x_ref, gt_ref, fg_ref,
                       o_ref, acc_t, ht_scr, s0_scr, s1_scr, aw0_scr, aw1_scr, *, block, n_blocks, final_norm):
    jj = pl.program_id(1)
    TM = h_ref.shape[0]
    SLAB = PEER_NKEYS
    EB = block

    last = pl.num_programs(1) - 1
    TL = min(TM, PEER_LANE_TILE)
    zero_bf = jnp.zeros((SLAB, TL), BF16)

    def row_tile(ref, h, a, cols):
        t = _bf(jnp.broadcast_to(ref[h, pl.ds(a, 1), cols], (16, TL)))
        return jnp.concatenate([t] * (SLAB // 16), axis=0)

    def vpu_stage(k, s_ref, aw_ref):
        for p in range(EB // (2 * SLAB)):
            lo = p * 2 * SLAB
            for c0 in range(0, TM, TL):
                cols = slice(c0, c0 + TL)
                s = s_ref[lo:lo + 2 * SLAB, cols]
                act = 0.5 * s * (1.0 + lax.erf(s * SQRT_HALF))
                weights = []
                for half in range(2):
                    a = k * (EB // SLAB) + p * 2 + half
                    w = jnp.zeros((SLAB, TL), BF16)
                    for h in range(PEER_HEADS):
                        chosen = r2_ref[h, :, cols] < row_tile(na_ref, h, a, cols)
                        w = w + jnp.where(chosen, e2_ref[h, :, cols], zero_bf) * row_tile(e1_ref, h, a, cols)
                    weights.append(w)
                aw_ref[lo:lo + 2 * SLAB, cols] = act * jnp.concatenate(weights, axis=0)

    def scores(half):
        return _bf(_dot(u_ref[half * EB:(half + 1) * EB, :], ht_scr[...]))

    def outputs(half, aw_ref):
        return _dot(vt_ref[:, half * EB:(half + 1) * EB], aw_ref[...])

    @pl.when(jj == 0)
    def _():
        ht_scr[...] = _bf(h_ref[...].astype(F32).T)
        s0_scr[...] = scores(0)
        s1_scr[...] = scores(1)
        vpu_stage(0, s0_scr, aw0_scr)
        acc_t[...] = jnp.zeros_like(acc_t)

    @pl.when(jnp.logical_and(jj > 0, jj < last))
    def _():
        out0 = outputs(0, aw0_scr)
        vpu_stage(2 * jj - 1, s1_scr, aw1_scr)
        s0_scr[...] = scores(0)
        out1 = outputs(1, aw1_scr)
        s1_scr[...] = scores(1)
        vpu_stage(2 * jj, s0_scr, aw0_scr)
        acc_t[...] += out0 + out1

    @pl.when(jj == last)
    def _():
        out0 = outputs(0, aw0_scr)
        vpu_stage(n_blocks - 1, s1_scr, aw1_scr)
        out1 = outputs(1, aw1_scr)
        xn = x_ref[...] + gt_ref[0] * (acc_t[...] + out0 + out1).T
        o_ref[...] = _rms(xn, fg_ref[...]) if final_norm else xn


def _peer_dense(x, hn, sel, u, vt, gt2, TM, tiles_per_batch, final_g):
    T, D = x.shape
    E = u.shape[0]
    EB = 1024
    final_norm = final_g is not None
    fg = (final_g if final_norm else jnp.ones((D,), F32)).reshape(1, D)
    sspec = pl.BlockSpec((PEER_HEADS, PEER_NKEYS, TM), lambda i, e: (0, 0, i))
    nE = E // EB
    nP = nE // 2
    kern = functools.partial(_peer_dense_kernel, block=EB, n_blocks=nE, final_norm=final_norm)
    return pl.pallas_call(
        kern,
        grid=(T // TM, nP + 1),
        in_specs=[pl.BlockSpec((TM, D), lambda i, e: (i, 0)),
                  sspec, sspec, sspec, sspec,
                  pl.BlockSpec((2 * EB, D), lambda i, e: (jnp.minimum(e, nP - 1), 0)),
                  pl.BlockSpec((D, 2 * EB), lambda i, e: (0, jnp.maximum(e - 1, 0))),
                  pl.BlockSpec((TM, D), lambda i, e: (i, 0)),
                  pl.BlockSpec((1, 1, D), lambda i, e: (i // tiles_per_batch, 0, 0)),
                  pl.BlockSpec((1, D), lambda i, e: (0, 0))],
        out_specs=pl.BlockSpec((TM, D), lambda i, e: (i, 0)),
        out_shape=jax.ShapeDtypeStruct((T, D), F32),
        scratch_shapes=[pltpu.VMEM((D, TM), F32), pltpu.VMEM((D, TM), BF16),
                        pltpu.VMEM((EB, TM), BF16), pltpu.VMEM((EB, TM), BF16),
                        pltpu.VMEM((EB, TM), BF16), pltpu.VMEM((EB, TM), BF16)],
        compiler_params=_cp("parallel", "arbitrary"),
        name="peer_dense",
    )(hn, *sel, u, vt, x, gt2, fg)


def _direction_masks():
    t = jnp.arange(CHUNK)
    le = (t[None, :] <= t[:, None]).astype(F32)
    tri = jnp.stack([le, le.T])
    i = jnp.arange(GROUP_W)
    same = (i[:, None] // CHUNK) == (i[None, :] // CHUNK)
    ti, tj = i[:, None] % CHUNK, i[None, :] % CHUNK
    msl = jnp.stack([same & (tj < ti), same & (tj > ti)]).astype(F32)
    minc = jnp.stack([same & (tj <= ti), same & (tj >= ti)]).astype(F32)
    return tri, msl, minc


def _layer_params(l, w_in, shift_mu, w0, w2, a0, a2, g2, k_k, k_a, r_k, lnx_g, lnx_b, w_oA, conv_w,
                  cnorm_g, cnorm_b, w_oB, gate_b, w_out, norm2_g):
    D = D_MODEL
    row = lambda t: t.reshape(1, -1)
    zeros = jnp.zeros((LORA_W, D), F32)
    tri, msl, minc = _direction_masks()
    hs = (jnp.arange(D)[:, None] // HEAD == jnp.arange(128)[None, :]).astype(F32)
    pad_cols = P_RWKV_PAD - P_RWKV
    return dict(
        w_rk=_bf(jnp.pad(w_in[l][:, :P_RWKV], ((0, 0), (0, pad_cols)))),
        w_cg=_bf(w_in[l][:, P_RWKV:]),
        mu=jnp.pad(shift_mu[l], ((0, 0), (0, pad_cols))),
        w0=w0[l].reshape(2, 1, D), a0=a0[l].reshape(2, 1, D),
        w2p=_split(jnp.stack([jnp.concatenate([w2[l, 0], zeros]), jnp.concatenate([zeros, w2[l, 1]])])),
        a2p=_split(jnp.stack([jnp.concatenate([a2[l, 0], zeros]), jnp.concatenate([zeros, a2[l, 1]])])),
        g2p=_bf(jnp.pad(g2[l], ((0, LORA_G_PAD - LORA_G), (0, 0)))),
        k_k=row(k_k[l]), k_a=row(k_a[l]), r_k=row(r_k[l]),
        lnx_g=row(lnx_g[l]), lnx_b=row(lnx_b[l]), w_oA=_bf(w_oA[l]),
        conv_w=conv_w[l], cnorm_g=row(cnorm_g[l]), cnorm_b=row(cnorm_b[l]), w_oB=_bf(w_oB[l]),
        gate_b=row(gate_b[l]), w_out=_bf(w_out[l]), norm2_g=row(norm2_g[l]),
        hs=_bf(hs), hb=_bf(hs.T), tri=_bf(tri), msl=msl, minc=minc)


def _mixer(x, mod, pr, norm1_g, stride, h0, emit):
    B, L, D = x.shape
    sh1, sc1, gt1, sh2, sc2 = (mod[:, i:i + 1, :] for i in range(5))
    z_rk = _proj_shift(x, sh1, sc1, norm1_g, pr["w_rk"], pr["mu"])
    y, bv, gs, h_t = _rwkv(z_rk, pr, h0)
    if not emit:
        return None, None, h_t
    z_cg = _proj(x, sh1, sc1, norm1_g, pr["w_cg"])
    cv = _conv(z_cg, pr["conv_w"], stride)
    xn, hn = _post(x, y, bv, gs, cv, z_cg, pr, gt1, sh2, sc2)
    return xn, hn, h_t


def _peer(x, hn, wq, keys, u, v, gt2, final_g=None):
    B, L, D = x.shape
    T = B * L
    sel = _peer_prep(hn.reshape(T, D), wq, keys, min(L, PEER_LANE_TILE))
    TM = min(L, PEER_DENSE_TOKENS)
    out = _peer_dense(x.reshape(T, D), hn.reshape(T, D), sel, u, v, gt2, TM, L // TM, final_g)
    return out.reshape(B, L, D)


def kernel(x, c, ctx, c_ctx, ada_w, ada_b, norm1_g, norm2_g, w_in, shift_mu, w0, w2, a0, a2, g2, k_k, k_a, r_k, lnx_g, lnx_b, w_oA, conv_w, cnorm_g, cnorm_b, w_oB, gate_b, w_out, w_q, sub_keys, peer_u, peer_v, final_g):
    B, L, D = x.shape
    depth = ada_w.shape[0]
    xc = ctx
    n_rows = -(-(B + 1) // 8) * 8
    c_rows = jnp.pad(jnp.concatenate([c, c_ctx[None, :]], axis=0), ((0, n_rows - B - 1), (0, 0)))
    zero_state = jnp.zeros((2, B, N_GROUPS, HEAD, GROUP_W), F32)
    for l in range(depth):
        last = l == depth - 1
        pr = _layer_params(l, w_in, shift_mu, w0, w2, a0, a2, g2, k_k, k_a, r_k, lnx_g, lnx_b, w_oA,
                           conv_w, cnorm_g, cnorm_b, w_oB, gate_b, w_out, norm2_g)
        mod_all = _modulation(c_rows, ada_w[l], ada_b[l])
        mod = mod_all[:B].reshape(B, 6, D)
        modc = jnp.broadcast_to(mod_all[B].reshape(1, 6, D), (B, 6, D))
        wq = _bf(w_q[l])
        keys = _bf(sub_keys[l])
        u = _bf(peer_u[l])
        v = _bf(peer_v[l]).T

        xc_new, hnc, ctx_states = _mixer(xc, modc, pr, norm1_g[l], 1, zero_state, emit=not last)
        xn, hn, _ = _mixer(x, mod, pr, norm1_g[l], GRID_W, ctx_states, emit=True)
        x = _peer(xn, hn, wq, keys, u, v, mod[:, 5:6, :], final_g if last else None)
        if not last:
            xc = _peer(xc_new, hnc, wq, keys, u, v, modc[:, 5:6, :])
    return x
```

```python
import functools

import jax
import jax.numpy as jnp
from jax import lax
from jax.experimental import pallas as pl
from jax.experimental.pallas import tpu as pltpu

F32 = jnp.float32
BF16 = jnp.bfloat16
HI = lax.Precision.HIGHEST

D_MODEL = 1024
HEAD = 64
HEADS = D_MODEL // HEAD
GROUP_HEADS = 4
GROUP_W = GROUP_HEADS * HEAD
N_GROUPS = HEADS // GROUP_HEADS
CHUNK = 64
assert CHUNK == HEAD
RWKV_ROWS_PER_STEP = 4
LORA_W = 64
LORA_A = 64
LORA_G = 160
LORA_G_PAD = 256
P_RWKV = 3 * D_MODEL + 2 * LORA_W + 2 * LORA_A + LORA_G
P_RWKV_PAD = 3 * D_MODEL + 2 * LORA_W + 2 * LORA_A + LORA_G_PAD
COL_W1 = 3 * D_MODEL
COL_A1 = COL_W1 + 2 * LORA_W
COL_G1 = COL_A1 + 2 * LORA_A
CONV_K = 31
CONV_HALF = CONV_K // 2
GRID_W = 64
PEER_HEADS = 8
PEER_NKEYS = 128
PEER_HALF = 128
PEER_TOPK = 16
PEER_LANE_TILE = 256
PEER_DENSE_TOKENS = 512
NORM_EPS = 1e-6
LN_EPS = 1e-5
GN_EPS = HEAD * 1e-5
VMEM_LIMIT = 56 * 1024 * 1024
NOT_SELECTED = 99.0
NEG_INF = float("-inf")
RANK_MARK = -(2.0 ** 100)
SQRT_HALF = 0.7071067811865476
EXP_MINUS_HALF = 0.6065306597126334


def _cp(*sem):
    return pltpu.CompilerParams(dimension_semantics=sem, vmem_limit_bytes=VMEM_LIMIT)


def _dot(a, b):
    return jnp.dot(a, b, preferred_element_type=F32)


def _dot_hi(a, b):
    return jnp.dot(a, b, precision=HI, preferred_element_type=F32)


def _dot_nt(a, b):
    return lax.dot_general(a, b, (((1,), (1,)), ((), ())), preferred_element_type=F32)


def _dot_tn(a, b):
    return lax.dot_general(a, b, (((0,), (0,)), ((), ())), preferred_element_type=F32)


def _bf(a):
    return a.astype(BF16)


def _split(a):
    hi = a.astype(BF16)
    return hi, (a - hi.astype(F32)).astype(BF16)


def _dot_split_lhs(a, b_bf):
    hi, lo = _split(a)
    return _dot(hi, b_bf) + _dot(lo, b_bf)


def _dot_split(a, b_hi, b_lo):
    hi, lo = _split(a)
    return _dot(hi, b_hi) + _dot(hi, b_lo) + _dot(lo, b_hi)


def _head_sum(t, hs_bf, hb_bf):
    return _dot_split_lhs(_dot_split_lhs(t, hs_bf), hb_bf)


def _sigmoid(x):
    return jax.nn.sigmoid(x)


def _rms(x, g):
    return x * lax.rsqrt(jnp.mean(x * x, axis=-1, keepdims=True) + NORM_EPS) * g


def _mod_kernel(c_ref, w_ref, b_ref, o_ref):
    c = c_ref[...]
    o_ref[...] = _dot_hi(c * _sigmoid(c), w_ref[...]) + b_ref[...]


def _modulation(c_rows, ada_w, ada_b):
    R, D = c_rows.shape
    N = ada_w.shape[1]
    TN = 512
    return pl.pallas_call(
        _mod_kernel,
        grid=(N // TN,),
        in_specs=[pl.BlockSpec((R, D), lambda j: (0, 0)),
                  pl.BlockSpec((D, TN), lambda j: (0, j)),
                  pl.BlockSpec((1, TN), lambda j: (0, j))],
        out_specs=pl.BlockSpec((R, TN), lambda j: (0, j)),
        out_shape=jax.ShapeDtypeStruct((R, N), F32),
        compiler_params=_cp("arbitrary"),
        name="modulation",
    )(c_rows, ada_w, ada_b.reshape(1, N))


def _proj_kernel(x_ref, sh_ref, sc_ref, g_ref, w_ref, o_ref):
    h = _rms(x_ref[0], g_ref[...]) * (1.0 + sc_ref[0]) + sh_ref[0]
    o_ref[0] = _dot(_bf(h), w_ref[...])


def _proj(x, sh, sc, g, w):
    B, L, D = x.shape
    N = w.shape[1]
    TM = min(L, 256)
    return pl.pallas_call(
        _proj_kernel,
        grid=(B, L // TM),
        in_specs=[pl.BlockSpec((1, TM, D), lambda b, i: (b, i, 0)),
                  pl.BlockSpec((1, 1, D), lambda b, i: (b, 0, 0)),
                  pl.BlockSpec((1, 1, D), lambda b, i: (b, 0, 0)),
                  pl.BlockSpec((1, D), lambda b, i: (0, 0)),
                  pl.BlockSpec((D, N), lambda b, i: (0, 0))],
        out_specs=pl.BlockSpec((1, TM, N), lambda b, i: (b, i, 0)),
        out_shape=jax.ShapeDtypeStruct((B, L, N), F32),
        compiler_params=_cp("parallel", "parallel"),
        name="proj",
    )(x, sh, sc, g.reshape(1, D), w)


def _proj_shift_kernel(x_ref, xp_ref, xn_ref, sh_ref, sc_ref, g_ref, w_ref, mu_ref, o_ref):
    i = pl.program_id(1)
    TM = x_ref.shape[1]
    xs = jnp.concatenate([xp_ref[0], x_ref[0], xn_ref[0]], axis=0)
    h = _rms(xs, g_ref[...]) * (1.0 + sc_ref[0]) + sh_ref[0]
    z = _dot(_bf(h), w_ref[...])
    row = lax.broadcasted_iota(jnp.int32, (TM, 1), 0)
    at_start = jnp.logical_and(row == 0, i == 0)
    at_end = jnp.logical_and(row == TM - 1, i == pl.num_programs(1) - 1)
    zp = jnp.where(at_start, 0.0, z[7:7 + TM])
    zn = jnp.where(at_end, 0.0, z[9:9 + TM])
    m0 = mu_ref[0:1, :]
    m1 = mu_ref[1:2, :]
    o_ref[0] = z[8:8 + TM] * (1.0 - m0 - m1) + m0 * zp + m1 * zn


def _proj_shift(x, sh, sc, g, w, mu):
    B, L, D = x.shape
    N = w.shape[1]
    TM = min(L, 256)
    nb8 = L // 8
    return pl.pallas_call(
        _proj_shift_kernel,
        grid=(B, L // TM),
        in_specs=[pl.BlockSpec((1, TM, D), lambda b, i: (b, i, 0)),
                  pl.BlockSpec((1, 8, D), lambda b, i: (b, jnp.maximum(i * (TM // 8) - 1, 0), 0)),
                  pl.BlockSpec((1, 8, D), lambda b, i: (b, jnp.minimum((i + 1) * (TM // 8), nb8 - 1), 0)),
                  pl.BlockSpec((1, 1, D), lambda b, i: (b, 0, 0)),
                  pl.BlockSpec((1, 1, D), lambda b, i: (b, 0, 0)),
                  pl.BlockSpec((1, D), lambda b, i: (0, 0)),
                  pl.BlockSpec((D, N), lambda b, i: (0, 0)),
                  pl.BlockSpec((2, N), lambda b, i: (0, 0))],
        out_specs=pl.BlockSpec((1, TM, N), lambda b, i: (b, i, 0)),
        out_shape=jax.ShapeDtypeStruct((B, L, N), F32),
        compiler_params=_cp("parallel", "parallel"),
        name="proj_shift",
    )(x, x, x, sh, sc, g.reshape(1, D), w, mu)


_KAP, _RT, _KT, _BT, _KH, _BH, _V = range(7)


def _rwkv_kernel(z_ref, w0_ref, w2h_ref, w2l_ref, a0_ref, a2h_ref, a2l_ref,
                 kk_ref, ka_ref, rk_ref, hs_ref, hb_ref, tri_ref, msl_ref, minc_ref, h0_ref,
                 y_ref, bv_ref, gs_ref, hT_ref, H_scr, nat_scr, pc_scr, *, n_chunks):
    d = pl.program_id(0)
    c = pl.program_id(2)
    cc = jnp.where(d == 0, c, n_chunks - 1 - c)
    C = CHUNK

    R = z_ref.shape[0]

    @pl.when(c == 0)
    def _():
        H_scr[...] = h0_ref[0]

    hs = hs_ref[...]
    hb = hb_ref[...]

    def head_sum(t):
        return _head_sum(t, hs, hb)

    def prepare(rr):
        def shifted(lo, hi):
            return z_ref[rr, :, lo:hi]

        r = shifted(0, D_MODEL)
        k = shifted(D_MODEL, 2 * D_MODEL)
        v = shifted(2 * D_MODEL, 3 * D_MODEL)
        w1 = shifted(COL_W1, COL_A1)
        a1 = shifted(COL_A1, COL_G1)
        g1 = shifted(COL_G1, P_RWKV_PAD)

        wl = w0_ref[0] + _dot_split(jnp.tanh(w1), w2h_ref[0], w2l_ref[0])
        logw = -EXP_MINUS_HALF * _sigmoid(wl)
        a = _sigmoid(_dot_split(a1, a2h_ref[0], a2l_ref[0]) + a0_ref[0])
        kkr = k * kk_ref[...]
        inv_norm = lax.rsqrt(jnp.maximum(_dot_split_lhs(kkr * kkr, hs), 1e-24))
        kk = kkr * _dot_split_lhs(inv_norm, hb)
        kd = k * (1.0 + (a - 1.0) * ka_ref[...])
        bb = kk * a
        bv_ref[0, rr] = head_sum(r * kd * rk_ref[...]) * v
        gs_ref[0, rr] = _sigmoid(g1)

        lw_hi, lw_lo = _split(logw)
        g_in = _dot(tri_ref[0], lw_hi) + _dot(tri_ref[0], lw_lo)
        g_ex = g_in - logw
        g_c = jnp.sum(logw, axis=0, keepdims=True)
        e_inv = jnp.exp(-g_in)
        e_hat = jnp.exp(g_c - g_in)
        nat_scr[rr, _KAP] = kk * jnp.exp(g_ex)
        nat_scr[rr, _RT] = r * jnp.exp(g_in)
        nat_scr[rr, _KT] = kd * e_inv
        nat_scr[rr, _BT] = bb * e_inv
        nat_scr[rr, _KH] = kd * e_hat
        nat_scr[rr, _BH] = bb * e_hat
        nat_scr[rr, _V] = v
        pc_scr[rr] = jnp.exp(g_c)

    for rr in range(R):
        prepare(rr)

    lane_head = lax.broadcasted_iota(jnp.int32, (C, GROUP_W), 1) // HEAD
    ii = lax.broadcasted_iota(jnp.int32, (GROUP_W, GROUP_W), 0)
    jj = lax.broadcasted_iota(jnp.int32, (GROUP_W, GROUP_W), 1)
    eye = ii == jj

    def stacked(t):
        return jnp.concatenate([jnp.where(lane_head == j, t, 0.0) for j in range(GROUP_HEADS)], axis=0)

    def collapse(t):
        return t[0:C] + t[C:2 * C] + t[2 * C:3 * C] + t[3 * C:4 * C]

    G = range(R * N_GROUPS)

    def lanes(q):
        return slice((q % N_GROUPS) * GROUP_W, (q % N_GROUPS + 1) * GROUP_W)

    def nat(i, q):
        return nat_scr[q // N_GROUPS, i, :, lanes(q)]

    x_kap = [stacked(nat(_KAP, g)) for g in G]
    x_v = [_bf(stacked(nat(_V, g))) for g in G]
    x_bk = [jnp.concatenate([_bf(stacked(nat(_BT, g))), _bf(stacked(nat(_KT, g)))], axis=0) for g in G]
    kr = [_bf(jnp.concatenate([nat(_KAP, g), nat(_RT, g)], axis=0)) for g in G]
    akr = [_dot_nt(kr[g], x_bk[g]) for g in G]

    def tiled(t):
        return jnp.concatenate([t] * GROUP_HEADS, axis=0)

    same_head = (ii // C) == (jj // C)

    def block_diag(side):
        t = tiled(_bf(side))
        return jnp.where(same_head, t, jnp.zeros_like(t))

    msl_c = collapse(msl_ref[0]) > 0.5
    minc_c = collapse(minc_ref[0]) > 0.5
    n_side = [jnp.where(msl_c, akr[g][:C, :GROUP_W], 0.0) for g in G]
    a_kk = [_bf(jnp.where(msl_c, akr[g][:C, GROUP_W:], 0.0)) for g in G]
    a_rb = [_bf(jnp.where(minc_c, akr[g][C:, :GROUP_W], 0.0)) for g in G]
    a_rk = [_bf(jnp.where(minc_c, akr[g][C:, GROUP_W:], 0.0)) for g in G]
    g0 = [stacked(_dot(a_kk[g], x_v[g])) for g in G]
    y0 = [_dot(a_rk[g], x_v[g]) for g in G]
    eye_side = collapse(jnp.where(eye, 1.0, 0.0))
    p_inv = [eye_side - n_side[g] for g in G]
    m_pow = [_dot(_bf(n_side[g]), block_diag(n_side[g])) for g in G]
    for _ in range(4):
        both = [_dot(_bf(jnp.concatenate([m_pow[g], p_inv[g]], axis=0)), block_diag(m_pow[g])) for g in G]
        m_pow = [both[g][:C] for g in G]
        p_inv = [p_inv[g] + both[g][C:] for g in G]
    p_inv = [p_inv[g] + _dot(_bf(p_inv[g]), block_diag(m_pow[g])) for g in G]
    w_nat = [_dot(_bf(p_inv[g]), jnp.concatenate([_bf(x_kap[g]), _bf(g0[g])], axis=1)) for g in G]
    w12 = [_bf(jnp.concatenate([stacked(w_nat[g][:, :GROUP_W]), stacked(w_nat[g][:, GROUP_W:])], axis=1))
           for g in G]
    aw = [_dot(a_rb[g], w12[g]) for g in G]
    def heads_transposed(t):
        tt = t.T
        return jnp.concatenate([tt[j * HEAD:(j + 1) * HEAD] for j in range(GROUP_HEADS)], axis=1)

    bw = [_dot(_bf(heads_transposed(nat(_BH, g))), w12[g]) for g in G]
    kv = [_dot(_bf(heads_transposed(nat(_KH, g))), x_v[g]) for g in G]
    h_prev = [H_scr[g // N_GROUPS, g % N_GROUPS] for g in G]
    ys = []
    h_new = []
    for g in G:
        h_hi = block_diag(h_prev[g])
        h_lo = block_diag(h_prev[g] - _bf(h_prev[g]).astype(F32))
        qb = _bf(nat(_RT, g) - aw[g][:, :GROUP_W])
        m_side = eye_side * pc_scr[g // N_GROUPS, :, lanes(g)] - bw[g][:, :GROUP_W]
        m_hi, m_lo = _split(m_side)
        by_hi = _dot(jnp.concatenate([m_hi, m_lo, qb], axis=0), h_hi)
        by_lo = _dot(jnp.concatenate([m_hi, qb], axis=0), h_lo)
        ys.append(by_hi[2 * C:] + by_lo[C:] + y0[g] - aw[g][:, GROUP_W:])
        h_new.append(by_hi[:C] + by_hi[C:2 * C] + by_lo[:C] + kv[g] - bw[g][:, GROUP_W:])
    for rr in range(R):
        y_ref[0, rr] = jnp.concatenate(ys[rr * N_GROUPS:(rr + 1) * N_GROUPS], axis=1)
    for g in G:
        H_scr[g // N_GROUPS, g % N_GROUPS] = h_new[g]

    @pl.when(c == n_chunks - 1)
    def _():
        hT_ref[0] = H_scr[...]


def _rwkv(z, pr, h0):
    B, L, N = z.shape
    C = CHUNK
    nC = L // C
    D = D_MODEL

    def cidx(d, c):
        return c + d * (nC - 1 - 2 * c)

    zmap = lambda d, b, c: (b, cidx(d, c), 0)
    const2 = lambda d, b, c: (0, 0)
    dir3 = lambda d, b, c: (d, 0, 0)
    omap = lambda d, b, c: (d, b, cidx(d, c), 0)
    smap = lambda d, b, c: (d, b, 0, 0, 0)
    R = RWKV_ROWS_PER_STEP if B % RWKV_ROWS_PER_STEP == 0 else 1
    kern = functools.partial(_rwkv_kernel, n_chunks=nC)
    return pl.pallas_call(
        kern,
        grid=(2, B // R, nC),
        in_specs=[pl.BlockSpec((R, C, N), zmap),
                  pl.BlockSpec((1, 1, D), dir3),
                  pl.BlockSpec((1, 2 * LORA_W, D), dir3),
                  pl.BlockSpec((1, 2 * LORA_W, D), dir3),
                  pl.BlockSpec((1, 1, D), dir3),
                  pl.BlockSpec((1, 2 * LORA_A, D), dir3),
                  pl.BlockSpec((1, 2 * LORA_A, D), dir3),
                  pl.BlockSpec((1, D), const2),
                  pl.BlockSpec((1, D), const2),
                  pl.BlockSpec((1, D), const2),
                  pl.BlockSpec((D, 128), const2),
                  pl.BlockSpec((128, D), const2),
                  pl.BlockSpec((1, C, C), dir3),
                  pl.BlockSpec((1, GROUP_W, GROUP_W), dir3),
                  pl.BlockSpec((1, GROUP_W, GROUP_W), dir3),
                  pl.BlockSpec((1, R, N_GROUPS, HEAD, GROUP_W), smap)],
        out_specs=[pl.BlockSpec((1, R, C, D), omap),
                   pl.BlockSpec((1, R, C, D), omap),
                   pl.BlockSpec((1, R, C, LORA_G_PAD), omap),
                   pl.BlockSpec((1, R, N_GROUPS, HEAD, GROUP_W), smap)],
        out_shape=[jax.ShapeDtypeStruct((2, B, L, D), F32),
                   jax.ShapeDtypeStruct((2, B, L, D), F32),
                   jax.ShapeDtypeStruct((2, B, L, LORA_G_PAD), F32),
                   jax.ShapeDtypeStruct((2, B, N_GROUPS, HEAD, GROUP_W), F32)],
        scratch_shapes=[pltpu.VMEM((R, N_GROUPS, HEAD, GROUP_W), F32),
                        pltpu.VMEM((R, 7, C, D), F32),
                        pltpu.VMEM((R, 1, D), F32)],
        compiler_params=_cp("arbitrary", "arbitrary", "arbitrary"),
        name="rwkv",
    )(z, pr["w0"], *pr["w2p"], pr["a0"], *pr["a2p"], pr["k_k"], pr["k_a"], pr["r_k"],
      pr["hs"], pr["hb"], pr["tri"], pr["msl"], pr["minc"], h0)


def _conv_kernel(za_ref, zb_ref, w_ref, o_ref, upad, *, L, stride, rows_per_step):
    pad = CONV_HALF * stride
    TC = za_ref.shape[-1]
    upad[0:pad, :] = jnp.zeros((pad, TC), F32)
    upad[pad + L:pad + L + pad, :] = jnp.zeros((pad, TC), F32)
    upad[pad:pad + L, :] = za_ref[0] * _sigmoid(zb_ref[0])
    RB = rows_per_step

    def block(r0):
        acc = jnp.zeros((RB, TC), F32)
        for j in range(CONV_K):
            acc = acc + w_ref[j:j + 1, :] * upad[pl.ds(r0 + j * stride, RB), :]
        o_ref[0, pl.ds(r0, RB), :] = acc

    if stride % 8 == 0:
        def body(i, carry):
            block(pl.multiple_of(i * RB, RB))
            return carry
        lax.fori_loop(0, L // RB, body, 0)
    else:
        for i in range(L // RB):
            block(i * RB)


def _conv(zcg, conv_w, stride):
    B, L, _ = zcg.shape
    D = D_MODEL
    TC = 128
    nct = D // TC
    wpad = jnp.pad(conv_w, ((0, 32 - CONV_K), (0, 0)))
    RB = min(L, 128)
    kern = functools.partial(_conv_kernel, L=L, stride=stride, rows_per_step=RB)
    return pl.pallas_call(
        kern,
        grid=(B, nct),
        in_specs=[pl.BlockSpec((1, L, TC), lambda b, j: (b, 0, j)),
                  pl.BlockSpec((1, L, TC), lambda b, j: (b, 0, j + nct)),
                  pl.BlockSpec((32, TC), lambda b, j: (0, j))],
        out_specs=pl.BlockSpec((1, L, TC), lambda b, j: (b, 0, j)),
        out_shape=jax.ShapeDtypeStruct((B, L, D), F32),
        scratch_shapes=[pltpu.VMEM((L + 2 * CONV_HALF * stride, TC), F32)],
        compiler_params=_cp("parallel", "parallel"),
        name="conv",
    )(zcg, zcg, wpad)


def _post_kernel(x_ref, yf_ref, yb_ref, bf_ref, bb_ref, gs_ref, cv_ref, zg_ref,
                 lng_ref, lnb_ref, g2_ref, woa_ref, cng_ref, cnb_ref, wob_ref, gb_ref, wout_ref,
                 gt_ref, n2g_ref, sh2_ref, sc2_ref, hs_ref, hb_ref, xo_ref, hn_ref):
    D = D_MODEL
    hs = hs_ref[...]
    hb = hb_ref[...]

    def head_mean(t):
        return _head_sum(t, hs, hb) * (1.0 / HEAD)

    o = yf_ref[0, 0] + yb_ref[0, 0]
    oc = o - head_mean(o)
    on = oc * lax.rsqrt(head_mean(oc * oc) + GN_EPS) * lng_ref[...] + lnb_ref[...]
    on = on + bf_ref[0, 0] + bb_ref[0, 0]
    gate = _dot(_bf(gs_ref[0, 0]), g2_ref[...])
    y_a = _dot(_bf(on * gate), woa_ref[...])

    cv = cv_ref[0]
    cm = jnp.mean(cv, axis=-1, keepdims=True)
    cc = cv - cm
    cn = cc * lax.rsqrt(jnp.mean(cc * cc, axis=-1, keepdims=True) + LN_EPS) * cng_ref[...] + cnb_ref[...]
    y_b = _dot(_bf(cn * _sigmoid(cn)), wob_ref[...])

    gates = _sigmoid(zg_ref[0] + gb_ref[...])
    m = gates[:, :D] * y_a + gates[:, D:] * y_b
    xn = x_ref[0] + gt_ref[0] * _dot(_bf(m), wout_ref[...])
    xo_ref[0] = xn
    hn_ref[0] = _bf(_rms(xn, n2g_ref[...]) * (1.0 + sc2_ref[0]) + sh2_ref[0])


def _post(x, y, bv, gs, cv, zcg, pr, gt1, sh2, sc2):
    B, L, D = x.shape
    TM = min(L, 256)
    tok = lambda b, i: (b, i, 0)
    fwd = lambda b, i: (0, b, i, 0)
    bwd = lambda b, i: (1, b, i, 0)
    cst = lambda b, i: (0, 0)
    per_b = lambda b, i: (b, 0, 0)
    row = pl.BlockSpec((1, D), cst)
    mat = pl.BlockSpec((D, D), cst)
    return pl.pallas_call(
        _post_kernel,
        grid=(B, L // TM),
        in_specs=[pl.BlockSpec((1, TM, D), tok),
                  pl.BlockSpec((1, 1, TM, D), fwd), pl.BlockSpec((1, 1, TM, D), bwd),
                  pl.BlockSpec((1, 1, TM, D), fwd), pl.BlockSpec((1, 1, TM, D), bwd),
                  pl.BlockSpec((1, 1, TM, LORA_G_PAD), fwd),
                  pl.BlockSpec((1, TM, D), tok),
                  pl.BlockSpec((1, TM, 2 * D), lambda b, i: (b, i, 1)),
                  row, row, pl.BlockSpec((LORA_G_PAD, D), cst), mat,
                  row, row, mat, pl.BlockSpec((1, 2 * D), cst), mat,
                  pl.BlockSpec((1, 1, D), per_b), row,
                  pl.BlockSpec((1, 1, D), per_b), pl.BlockSpec((1, 1, D), per_b),
                  pl.BlockSpec((D, 128), cst), pl.BlockSpec((128, D), cst)],
        out_specs=[pl.BlockSpec((1, TM, D), tok), pl.BlockSpec((1, TM, D), tok)],
        out_shape=[jax.ShapeDtypeStruct((B, L, D), F32), jax.ShapeDtypeStruct((B, L, D), BF16)],
        compiler_params=_cp("parallel", "parallel"),
        name="post",
    )(x, y, y, bv, bv, gs, cv, zcg,
      pr["lnx_g"], pr["lnx_b"], pr["g2p"], pr["w_oA"], pr["cnorm_g"], pr["cnorm_b"], pr["w_oB"],
      pr["gate_b"], pr["w_out"], gt1, pr["norm2_g"], sh2, sc2, pr["hs"], pr["hb"])


def _top16(s, rowid):
    rank = jnp.full(s.shape, NOT_SELECTED, F32)
    cur = s
    vals = []
    for r in range(PEER_TOPK):
        m = jnp.max(cur, axis=0, keepdims=True)
        idx = jnp.min(jnp.where(cur == m, rowid, 1e9), axis=0, keepdims=True)
        sel = rowid == idx
        rank = jnp.where(sel, float(r), rank)
        cur = jnp.where(sel, NEG_INF, cur)
        vals.append(m)
    return rank, vals


def _top16_untied(arrays):
    n = range(len(arrays))
    cur = list(arrays)
    vals = [[] for _ in n]
    for r in range(PEER_TOPK):
        m = [jnp.max(cur[i], axis=0, keepdims=True) for i in n]
        cur = [jnp.where(cur[i] == m[i], RANK_MARK * (1.0 + r / 32.0), cur[i]) for i in n]
        for i in n:
            vals[i].append(m[i])
    ranked = [cur[i] <= RANK_MARK for i in n]
    rank = [jnp.where(ranked[i], (cur[i] * (1.0 / RANK_MARK) - 1.0) * 32.0, NOT_SELECTED) for i in n]
    n_ranked = [jnp.sum(jnp.where(ranked[i], 1.0, 0.0), axis=0, keepdims=True) for i in n]
    return rank, vals, n_ranked


def _peer_prep_kernel(h_ref, wq_ref, keys_ref, r2_ref, na_ref, e1_ref, e2_ref, q_scr, rk_scr, vl_scr):
    TM = h_ref.shape[0]
    K = PEER_TOPK
    q_scr[...] = _bf(_dot(h_ref[...], wq_ref[...]))
    rowid = lax.broadcasted_iota(jnp.int32, (PEER_NKEYS, TM), 0).astype(F32)
    kaid = lax.broadcasted_iota(jnp.int32, (K, TM), 0).astype(F32)

    def head(h, carry):
        off = pl.multiple_of(h * 2 * PEER_HALF, 2 * PEER_HALF)
        s1 = _dot_nt(keys_ref[h, 0], q_scr[:, pl.ds(off, PEER_HALF)])
        s2 = _dot_nt(keys_ref[h, 1], q_scr[:, pl.ds(off + PEER_HALF, PEER_HALF)])
        ranks, valss, n_ranked = _top16_untied([s1, s2])
        for half in range(2):
            rk_scr[half] = ranks[half]
            vl_scr[half] = jnp.concatenate(valss[half], axis=0)
        n_max = jnp.maximum(n_ranked[0], n_ranked[1])

        @pl.when(jnp.max(n_max) > float(K))
        def _():
            for half, s in ((0, s1), (1, s2)):
                rank, vals = _top16(s, rowid)
                rk_scr[half] = rank
                vl_scr[half] = jnp.concatenate(vals, axis=0)

        rank1 = rk_scr[0]
        rank2 = rk_scr[1]
        v1 = vl_scr[0]
        vals1 = [v1[r:r + 1] for r in range(K)]
        vals2 = [vl_scr[1, r:r + 1, :] for r in range(K)]
        taken = jnp.zeros((K, TM), F32)
        front = v1 + vals2[0]
        for _ in range(K):
            m = jnp.max(front, axis=0, keepdims=True)
            idx = jnp.min(jnp.where(front == m, kaid, 1e9), axis=0, keepdims=True)
            sel = kaid == idx
            taken = taken + jnp.where(sel, 1.0, 0.0)
            t_sel = jnp.max(jnp.where(sel, taken, 0.0), axis=0, keepdims=True)
            nxt = jnp.full((1, TM), NEG_INF, F32)
            for kb in range(1, K):
                nxt = jnp.where(t_sel == float(kb), vals2[kb], nxt)
            front = jnp.where(sel, v1 + nxt, front)
        e1k = jnp.exp(v1 - vals1[0])
        pref = jnp.zeros((1, TM), F32)
        zrow = jnp.zeros((K, TM), F32)
        for kb in range(K):
            pref = pref + jnp.exp(vals2[kb] - vals2[0])
            zrow = jnp.where(taken == float(kb + 1), pref, zrow)
        z = jnp.sum(e1k * zrow, axis=0, keepdims=True)
        na = jnp.zeros((PEER_NKEYS, TM), F32)
        for ka in range(K):
            na = jnp.where(rank1 == float(ka), taken[ka:ka + 1], na)
        r2_ref[h] = _bf(rank2)
        na_ref[h] = na
        e1_ref[h] = jnp.where(rank1 < float(K), jnp.exp(s1 - vals1[0]) / z, 0.0)
        e2_ref[h] = _bf(jnp.where(rank2 < float(K), jnp.exp(s2 - vals2[0]), 0.0))
        return carry

    lax.fori_loop(0, PEER_HEADS, head, 0)


def _peer_prep(hn, wq, keys, TM):
    T, D = hn.shape
    Q = wq.shape[1]
    shp = jax.ShapeDtypeStruct((PEER_HEADS, PEER_NKEYS, T), F32)
    shp_bf = jax.ShapeDtypeStruct((PEER_HEADS, PEER_NKEYS, T), BF16)
    ospec = pl.BlockSpec((PEER_HEADS, PEER_NKEYS, TM), lambda i: (0, 0, i))
    return pl.pallas_call(
        _peer_prep_kernel,
        grid=(T // TM,),
        in_specs=[pl.BlockSpec((TM, D), lambda i: (i, 0)),
                  pl.BlockSpec((D, Q), lambda i: (0, 0)),
                  pl.BlockSpec((PEER_HEADS, 2, PEER_NKEYS, PEER_HALF), lambda i: (0, 0, 0, 0))],
        out_specs=[ospec, ospec, ospec, ospec],
        out_shape=[shp_bf, shp, shp, shp_bf],
        scratch_shapes=[pltpu.VMEM((TM, Q), BF16),
                        pltpu.VMEM((2, PEER_NKEYS, TM), F32), pltpu.VMEM((2, PEER_TOPK, TM), F32)],
        compiler_params=_cp("parallel"),
        name="peer_prep",
    )(hn, wq, keys)


def _peer_dense_kernel(h_ref, r2_ref, na_ref, e1_ref, e2_ref, u_ref, vt_ref, x_ref, gt_ref, fg_ref,
                       o_ref, acc_t, ht_scr, s0_scr, s1_scr, aw0_scr, aw1_scr, *, block, n_blocks, final_norm):
    jj = pl.program_id(1)
    TM = h_ref.shape[0]
    SLAB = PEER_NKEYS
    EB = block

    last = pl.num_programs(1) - 1
    TL = min(TM, PEER_LANE_TILE)
    zero_bf = jnp.zeros((SLAB, TL), BF16)

    def row_tile(ref, h, a, cols):
        t = _bf(jnp.broadcast_to(ref[h, pl.ds(a, 1), cols], (16, TL)))
        return jnp.concatenate([t] * (SLAB // 16), axis=0)

    def vpu_stage(k, s_ref, aw_ref):
        for p in range(EB // (2 * SLAB)):
            lo = p * 2 * SLAB
            for c0 in range(0, TM, TL):
                cols = slice(c0, c0 + TL)
                s = s_ref[lo:lo + 2 * SLAB, cols]
                act = 0.5 * s * (1.0 + lax.erf(s * SQRT_HALF))
                weights = []
                for half in range(2):
                    a = k * (EB // SLAB) + p * 2 + half
                    w = jnp.zeros((SLAB, TL), BF16)
                    for h in range(PEER_HEADS):
                        chosen = r2_ref[h, :, cols] < row_tile(na_ref, h, a, cols)
                        w = w + jnp.where(chosen, e2_ref[h, :, cols], zero_bf) * row_tile(e1_ref, h, a, cols)
                    weights.append(w)
                aw_ref[lo:lo + 2 * SLAB, cols] = act * jnp.concatenate(weights, axis=0)

    def scores(half):
        return _bf(_dot(u_ref[half * EB:(half + 1) * EB, :], ht_scr[...]))

    def outputs(half, aw_ref):
        return _dot(vt_ref[:, half * EB:(half + 1) * EB], aw_ref[...])

    @pl.when(jj == 0)
    def _():
        ht_scr[...] = _bf(h_ref[...].astype(F32).T)
        s0_scr[...] = scores(0)
        s1_scr[...] = scores(1)
        vpu_stage(0, s0_scr, aw0_scr)
        acc_t[...] = jnp.zeros_like(acc_t)

    @pl.when(jnp.logical_and(jj > 0, jj < last))
    def _():
        out0 = outputs(0, aw0_scr)
        vpu_stage(2 * jj - 1, s1_scr, aw1_scr)
        s0_scr[...] = scores(0)
        out1 = outputs(1, aw1_scr)
        s1_scr[...] = scores(1)
        vpu_stage(2 * jj, s0_scr, aw0_scr)
        acc_t[...] += out0 + out1

    @pl.when(jj == last)
    def _():
        out0 = outputs(0, aw0_scr)
        vpu_stage(n_blocks - 1, s1_scr, aw1_scr)
        out1 = outputs(1, aw1_scr)
        xn = x_ref[...] + gt_ref[0] * (acc_t[...] + out0 + out1).T
        o_ref[...] = _rms(xn, fg_ref[...]) if final_norm else xn


def _peer_dense(x, hn, sel, u, vt, gt2, TM, tiles_per_batch, final_g):
    T, D = x.shape
    E = u.shape[0]
    EB = 1024
    final_norm = final_g is not None
    fg = (final_g if final_norm else jnp.ones((D,), F32)).reshape(1, D)
    sspec = pl.BlockSpec((PEER_HEADS, PEER_NKEYS, TM), lambda i, e: (0, 0, i))
    nE = E // EB
    nP = nE // 2
    kern = functools.partial(_peer_dense_kernel, block=EB, n_blocks=nE, final_norm=final_norm)
    return pl.pallas_call(
        kern,
        grid=(T // TM, nP + 1),
        in_specs=[pl.BlockSpec((TM, D), lambda i, e: (i, 0)),
                  sspec, sspec, sspec, sspec,
                  pl.BlockSpec((2 * EB, D), lambda i, e: (jnp.minimum(e, nP - 1), 0)),
                  pl.BlockSpec((D, 2 * EB), lambda i, e: (0, jnp.maximum(e - 1, 0))),
                  pl.BlockSpec((TM, D), lambda i, e: (i, 0)),
                  pl.BlockSpec((1, 1, D), lambda i, e: (i // tiles_per_batch, 0, 0)),
                  pl.BlockSpec((1, D), lambda i, e: (0, 0))],
        out_specs=pl.BlockSpec((TM, D), lambda i, e: (i, 0)),
        out_shape=jax.ShapeDtypeStruct((T, D), F32),
        scratch_shapes=[pltpu.VMEM((D, TM), F32), pltpu.VMEM((D, TM), BF16),
                        pltpu.VMEM((EB, TM), BF16), pltpu.VMEM((EB, TM), BF16),
                        pltpu.VMEM((EB, TM), BF16), pltpu.VMEM((EB, TM), BF16)],
        compiler_params=_cp("parallel", "arbitrary"),
        name="peer_dense",
    )(hn, *sel, u, vt, x, gt2, fg)


def _direction_masks():
    t = jnp.arange(CHUNK)
    le = (t[None, :] <= t[:, None]).astype(F32)
    tri = jnp.stack([le, le.T])
    i = jnp.arange(GROUP_W)
    same = (i[:, None] // CHUNK) == (i[None, :] // CHUNK)
    ti, tj = i[:, None] % CHUNK, i[None, :] % CHUNK
    msl = jnp.stack([same & (tj < ti), same & (tj > ti)]).astype(F32)
    minc = jnp.stack([same & (tj <= ti), same & (tj >= ti)]).astype(F32)
    return tri, msl, minc


def _layer_params(l, w_in, shift_mu, w0, w2, a0, a2, g2, k_k, k_a, r_k, lnx_g, lnx_b, w_oA, conv_w,
                  cnorm_g, cnorm_b, w_oB, gate_b, w_out, norm2_g):
    D = D_MODEL
    row = lambda t: t.reshape(1, -1)
    zeros = jnp.zeros((LORA_W, D), F32)
    tri, msl, minc = _direction_masks()
    hs = (jnp.arange(D)[:, None] // HEAD == jnp.arange(128)[None, :]).astype(F32)
    pad_cols = P_RWKV_PAD - P_RWKV
    return dict(
        w_rk=_bf(jnp.pad(w_in[l][:, :P_RWKV], ((0, 0), (0, pad_cols)))),
        w_cg=_bf(w_in[l][:, P_RWKV:]),
        mu=jnp.pad(shift_mu[l], ((0, 0), (0, pad_cols))),
        w0=w0[l].reshape(2, 1, D), a0=a0[l].reshape(2, 1, D),
        w2p=_split(jnp.stack([jnp.concatenate([w2[l, 0], zeros]), jnp.concatenate([zeros, w2[l, 1]])])),
        a2p=_split(jnp.stack([jnp.concatenate([a2[l, 0], zeros]), jnp.concatenate([zeros, a2[l, 1]])])),
        g2p=_bf(jnp.pad(g2[l], ((0, LORA_G_PAD - LORA_G), (0, 0)))),
        k_k=row(k_k[l]), k_a=row(k_a[l]), r_k=row(r_k[l]),
        lnx_g=row(lnx_g[l]), lnx_b=row(lnx_b[l]), w_oA=_bf(w_oA[l]),
        conv_w=conv_w[l], cnorm_g=row(cnorm_g[l]), cnorm_b=row(cnorm_b[l]), w_oB=_bf(w_oB[l]),
        gate_b=row(gate_b[l]), w_out=_bf(w_out[l]), norm2_g=row(norm2_g[l]),
        hs=_bf(hs), hb=_bf(hs.T), tri=_bf(tri), msl=msl, minc=minc)


def _mixer(x, mod, pr, norm1_g, stride, h0, emit):
    B, L, D = x.shape
    sh1, sc1, gt1, sh2, sc2 = (mod[:, i:i + 1, :] for i in range(5))
    z_rk = _proj_shift(x, sh1, sc1, norm1_g, pr["w_rk"], pr["mu"])
    y, bv, gs, h_t = _rwkv(z_rk, pr, h0)
    if not emit:
        return None, None, h_t
    z_cg = _proj(x, sh1, sc1, norm1_g, pr["w_cg"])
    cv = _conv(z_cg, pr["conv_w"], stride)
    xn, hn = _post(x, y, bv, gs, cv, z_cg, pr, gt1, sh2, sc2)
    return xn, hn, h_t


def _peer(x, hn, wq, keys, u, v, gt2, final_g=None):
    B, L, D = x.shape
    T = B * L
    sel = _peer_prep(hn.reshape(T, D), wq, keys, min(L, PEER_LANE_TILE))
    TM = min(L, PEER_DENSE_TOKENS)
    out = _peer_dense(x.reshape(T, D), hn.reshape(T, D), sel, u, v, gt2, TM, L // TM, final_g)
    return out.reshape(B, L, D)


def kernel(x, c, ctx, c_ctx, ada_w, ada_b, norm1_g, norm2_g, w_in, shift_mu, w0, w2, a0, a2, g2, k_k, k_a, r_k, lnx_g, lnx_b, w_oA, conv_w, cnorm_g, cnorm_b, w_oB, gate_b, w_out, w_q, sub_keys, peer_u, peer_v, final_g):
    B, L, D = x.shape
    depth = ada_w.shape[0]
    xc = ctx
    n_rows = -(-(B + 1) // 8) * 8
    c_rows = jnp.pad(jnp.concatenate([c, c_ctx[None, :]], axis=0), ((0, n_rows - B - 1), (0, 0)))
    zero_state = jnp.zeros((2, B, N_GROUPS, HEAD, GROUP_W), F32)
    for l in range(depth):
        last = l == depth - 1
        pr = _layer_params(l, w_in, shift_mu, w0, w2, a0, a2, g2, k_k, k_a, r_k, lnx_g, lnx_b, w_oA,
                           conv_w, cnorm_g, cnorm_b, w_oB, gate_b, w_out, norm2_g)
        mod_all = _modulation(c_rows, ada_w[l], ada_b[l])
        mod = mod_all[:B].reshape(B, 6, D)
        modc = jnp.broadcast_to(mod_all[B].reshape(1, 6, D), (B, 6, D))
        wq = _bf(w_q[l])
        keys = _bf(sub_keys[l])
        u = _bf(peer_u[l])
        v = _bf(peer_v[l]).T

        xc_new, hnc, ctx_states = _mixer(xc, modc, pr, norm1_g[l], 1, zero_state, emit=not last)
        xn, hn, _ = _mixer(x, mod, pr, norm1_g[l], GRID_W, ctx_states, emit=True)
        x = _peer(xn, hn, wq, keys, u, v, mod[:, 5:6, :], final_g if last else None)
        if not last:
            xc = _peer(xc_new, hnc, wq, keys, u, v, modc[:, 5:6, :])
    return x
```

```python
import functools

import jax
import jax.numpy as jnp
from jax import lax
from jax.experimental import pallas as pl
from jax.experimental.pallas import tpu as pltpu

F32 = jnp.float32
BF16 = jnp.bfloat16
HI = lax.Precision.HIGHEST

D_MODEL = 1024
HEAD = 64
HEADS = D_MODEL // HEAD
GROUP_HEADS = 4
GROUP_W = GROUP_HEADS * HEAD
N_GROUPS = HEADS // GROUP_HEADS
CHUNK = 64
assert CHUNK == HEAD
RWKV_ROWS_PER_STEP = 4
LORA_W = 64
LORA_A = 64
LORA_G = 160
LORA_G_PAD = 256
P_RWKV = 3 * D_MODEL + 2 * LORA_W + 2 * LORA_A + LORA_G
P_RWKV_PAD = 3 * D_MODEL + 2 * LORA_W + 2 * LORA_A + LORA_G_PAD
COL_W1 = 3 * D_MODEL
COL_A1 = COL_W1 + 2 * LORA_W
COL_G1 = COL_A1 + 2 * LORA_A
CONV_K = 31
CONV_HALF = CONV_K // 2
GRID_W = 64
PEER_HEADS = 8
PEER_NKEYS = 128
PEER_HALF = 128
PEER_TOPK = 16
PEER_LANE_TILE = 256
PEER_DENSE_TOKENS = 512
NORM_EPS = 1e-6
LN_EPS = 1e-5
GN_EPS = HEAD * 1e-5
VMEM_LIMIT = 56 * 1024 * 1024
NOT_SELECTED = 99.0
NEG_INF = float("-inf")
SQRT_HALF = 0.7071067811865476
EXP_MINUS_HALF = 0.6065306597126334


def _cp(*sem):
    return pltpu.CompilerParams(dimension_semantics=sem, vmem_limit_bytes=VMEM_LIMIT)


def _dot(a, b):
    return jnp.dot(a, b, preferred_element_type=F32)


def _dot_hi(a, b):
    return jnp.dot(a, b, precision=HI, preferred_element_type=F32)


def _dot_nt(a, b):
    return lax.dot_general(a, b, (((1,), (1,)), ((), ())), preferred_element_type=F32)


def _bf(a):
    return a.astype(BF16)


def _split(a):
    hi = a.astype(BF16)
    return hi, (a - hi.astype(F32)).astype(BF16)


def _dot_split_lhs(a, b_bf):
    hi, lo = _split(a)
    return _dot(hi, b_bf) + _dot(lo, b_bf)


def _dot_split(a, b_hi, b_lo):
    hi, lo = _split(a)
    return _dot(hi, b_hi) + _dot(hi, b_lo) + _dot(lo, b_hi)


def _head_sum(t, hs_bf, hb_bf):
    return _dot_split_lhs(_dot_split_lhs(t, hs_bf), hb_bf)


def _sigmoid(x):
    return jax.nn.sigmoid(x)


def _rms(x, g):
    return x * lax.rsqrt(jnp.mean(x * x, axis=-1, keepdims=True) + NORM_EPS) * g


def _mod_kernel(c_ref, w_ref, b_ref, o_ref):
    c = c_ref[...]
    o_ref[...] = _dot_hi(c * _sigmoid(c), w_ref[...]) + b_ref[...]


def _modulation(c_rows, ada_w, ada_b):
    R, D = c_rows.shape
    N = ada_w.shape[1]
    TN = 512
    return pl.pallas_call(
        _mod_kernel,
        grid=(N // TN,),
        in_specs=[pl.BlockSpec((R, D), lambda j: (0, 0)),
                  pl.BlockSpec((D, TN), lambda j: (0, j)),
                  pl.BlockSpec((1, TN), lambda j: (0, j))],
        out_specs=pl.BlockSpec((R, TN), lambda j: (0, j)),
        out_shape=jax.ShapeDtypeStruct((R, N), F32),
        compiler_params=_cp("arbitrary"),
        name="modulation",
    )(c_rows, ada_w, ada_b.reshape(1, N))


def _proj_kernel(x_ref, sh_ref, sc_ref, g_ref, w_ref, o_ref):
    h = _rms(x_ref[0], g_ref[...]) * (1.0 + sc_ref[0]) + sh_ref[0]
    o_ref[0] = _dot(_bf(h), w_ref[...])


def _proj(x, sh, sc, g, w):
    B, L, D = x.shape
    N = w.shape[1]
    TM = min(L, 256)
    return pl.pallas_call(
        _proj_kernel,
        grid=(B, L // TM),
        in_specs=[pl.BlockSpec((1, TM, D), lambda b, i: (b, i, 0)),
                  pl.BlockSpec((1, 1, D), lambda b, i: (b, 0, 0)),
                  pl.BlockSpec((1, 1, D), lambda b, i: (b, 0, 0)),
                  pl.BlockSpec((1, D), lambda b, i: (0, 0)),
                  pl.BlockSpec((D, N), lambda b, i: (0, 0))],
        out_specs=pl.BlockSpec((1, TM, N), lambda b, i: (b, i, 0)),
        out_shape=jax.ShapeDtypeStruct((B, L, N), F32),
        compiler_params=_cp("parallel", "parallel"),
        name="proj",
    )(x, sh, sc, g.reshape(1, D), w)


def _proj_shift_kernel(x_ref, xp_ref, xn_ref, sh_ref, sc_ref, g_ref, w_ref, mu_ref, o_ref):
    i = pl.program_id(1)
    TM = x_ref.shape[1]
    xs = jnp.concatenate([xp_ref[0], x_ref[0], xn_ref[0]], axis=0)
    h = _rms(xs, g_ref[...]) * (1.0 + sc_ref[0]) + sh_ref[0]
    z = _dot(_bf(h), w_ref[...])
    row = lax.broadcasted_iota(jnp.int32, (TM, 1), 0)
    at_start = jnp.logical_and(row == 0, i == 0)
    at_end = jnp.logical_and(row == TM - 1, i == pl.num_programs(1) - 1)
    zp = jnp.where(at_start, 0.0, z[7:7 + TM])
    zn = jnp.where(at_end, 0.0, z[9:9 + TM])
    m0 = mu_ref[0:1, :]
    m1 = mu_ref[1:2, :]
    o_ref[0] = z[8:8 + TM] * (1.0 - m0 - m1) + m0 * zp + m1 * zn


def _proj_shift(x, sh, sc, g, w, mu):
    B, L, D = x.shape
    N = w.shape[1]
    TM = min(L, 256)
    nb8 = L // 8
    return pl.pallas_call(
        _proj_shift_kernel,
        grid=(B, L // TM),
        in_specs=[pl.BlockSpec((1, TM, D), lambda b, i: (b, i, 0)),
                  pl.BlockSpec((1, 8, D), lambda b, i: (b, jnp.maximum(i * (TM // 8) - 1, 0), 0)),
                  pl.BlockSpec((1, 8, D), lambda b, i: (b, jnp.minimum((i + 1) * (TM // 8), nb8 - 1), 0)),
                  pl.BlockSpec((1, 1, D), lambda b, i: (b, 0, 0)),
                  pl.BlockSpec((1, 1, D), lambda b, i: (b, 0, 0)),
                  pl.BlockSpec((1, D), lambda b, i: (0, 0)),
                  pl.BlockSpec((D, N), lambda b, i: (0, 0)),
                  pl.BlockSpec((2, N), lambda b, i: (0, 0))],
        out_specs=pl.BlockSpec((1, TM, N), lambda b, i: (b, i, 0)),
        out_shape=jax.ShapeDtypeStruct((B, L, N), F32),
        compiler_params=_cp("parallel", "parallel"),
        name="proj_shift",
    )(x, x, x, sh, sc, g.reshape(1, D), w, mu)


_KAP, _RT, _KT, _BT, _KH, _BH, _V = range(7)


def _rwkv_kernel(z_ref, w0_ref, w2h_ref, w2l_ref, a0_ref, a2h_ref, a2l_ref,
                 kk_ref, ka_ref, rk_ref, hs_ref, hb_ref, tri_ref, msl_ref, minc_ref, h0_ref,
                 y_ref, bv_ref, gs_ref, hT_ref, H_scr, nat_scr, pc_scr, *, n_chunks):
    d = pl.program_id(0)
    c = pl.program_id(2)
    cc = jnp.where(d == 0, c, n_chunks - 1 - c)
    C = CHUNK

    R = z_ref.shape[0]

    @pl.when(c == 0)
    def _():
        H_scr[...] = h0_ref[0]

    hs = hs_ref[...]
    hb = hb_ref[...]

    def head_sum(t):
        return _head_sum(t, hs, hb)

    def prepare(rr):
        def cols(lo, hi):
            return z_ref[rr, :, lo:hi]

        r = cols(0, D_MODEL)
        k = cols(D_MODEL, 2 * D_MODEL)
        v = cols(2 * D_MODEL, 3 * D_MODEL)
        w1 = cols(COL_W1, COL_A1)
        a1 = cols(COL_A1, COL_G1)
        g1 = cols(COL_G1, P_RWKV_PAD)

        wl = w0_ref[0] + _dot_split(jnp.tanh(w1), w2h_ref[0], w2l_ref[0])
        logw = -EXP_MINUS_HALF * _sigmoid(wl)
        a = _sigmoid(_dot_split(a1, a2h_ref[0], a2l_ref[0]) + a0_ref[0])
        kkr = k * kk_ref[...]
        inv_norm = lax.rsqrt(jnp.maximum(_dot_split_lhs(kkr * kkr, hs), 1e-24))
        kk = kkr * _dot_split_lhs(inv_norm, hb)
        kd = k * (1.0 + (a - 1.0) * ka_ref[...])
        bb = kk * a
        bv_ref[0, rr] = head_sum(r * kd * rk_ref[...]) * v
        gs_ref[0, rr] = _sigmoid(g1)

        lw_hi, lw_lo = _split(logw)
        g_in = _dot(tri_ref[0], lw_hi) + _dot(tri_ref[0], lw_lo)
        g_ex = g_in - logw
        g_c = jnp.sum(logw, axis=0, keepdims=True)
        e_inv = jnp.exp(-g_in)
        e_hat = jnp.exp(g_c - g_in)
        nat_scr[rr, _KAP] = kk * jnp.exp(g_ex)
        nat_scr[rr, _RT] = r * jnp.exp(g_in)
        nat_scr[rr, _KT] = kd * e_inv
        nat_scr[rr, _BT] = bb * e_inv
        nat_scr[rr, _KH] = kd * e_hat
        nat_scr[rr, _BH] = bb * e_hat
        nat_scr[rr, _V] = v
        pc_scr[rr] = jnp.exp(g_c)

    for rr in range(R):
        prepare(rr)

    lane_head = lax.broadcasted_iota(jnp.int32, (C, GROUP_W), 1) // HEAD
    ii = lax.broadcasted_iota(jnp.int32, (GROUP_W, GROUP_W), 0)
    jj = lax.broadcasted_iota(jnp.int32, (GROUP_W, GROUP_W), 1)
    eye = ii == jj

    def stacked(t):
        return jnp.concatenate([jnp.where(lane_head == j, t, 0.0) for j in range(GROUP_HEADS)], axis=0)

    def collapse(t):
        return t[0:C] + t[C:2 * C] + t[2 * C:3 * C] + t[3 * C:4 * C]

    G = range(R * N_GROUPS)

    def lanes(q):
        return slice((q % N_GROUPS) * GROUP_W, (q % N_GROUPS + 1) * GROUP_W)

    def nat(i, q):
        return nat_scr[q // N_GROUPS, i, :, lanes(q)]

    x_kap = [stacked(nat(_KAP, g)) for g in G]
    x_v = [_bf(stacked(nat(_V, g))) for g in G]
    x_bk = [jnp.concatenate([_bf(stacked(nat(_BT, g))), _bf(stacked(nat(_KT, g)))], axis=0) for g in G]
    kr = [_bf(jnp.concatenate([nat(_KAP, g), nat(_RT, g)], axis=0)) for g in G]
    akr = [_dot_nt(kr[g], x_bk[g]) for g in G]

    def tiled(t):
        return jnp.concatenate([t] * GROUP_HEADS, axis=0)

    same_head = (ii // C) == (jj // C)

    def block_diag(side):
        t = tiled(_bf(side))
        return jnp.where(same_head, t, jnp.zeros_like(t))

    msl_c = collapse(msl_ref[0]) > 0.5
    minc_c = collapse(minc_ref[0]) > 0.5
    n_side = [jnp.where(msl_c, akr[g][:C, :GROUP_W], 0.0) for g in G]
    a_kk = [_bf(jnp.where(msl_c, akr[g][:C, GROUP_W:], 0.0)) for g in G]
    a_rb = [_bf(jnp.where(minc_c, akr[g][C:, :GROUP_W], 0.0)) for g in G]
    a_rk = [_bf(jnp.where(minc_c, akr[g][C:, GROUP_W:], 0.0)) for g in G]
    g0 = [stacked(_dot(a_kk[g], x_v[g])) for g in G]
    y0 = [_dot(a_rk[g], x_v[g]) for g in G]
    eye_side = collapse(jnp.where(eye, 1.0, 0.0))
    p_inv = [eye_side - n_side[g] for g in G]
    m_pow = [_dot(_bf(n_side[g]), block_diag(n_side[g])) for g in G]
    for _ in range(4):
        both = [_dot(_bf(jnp.concatenate([m_pow[g], p_inv[g]], axis=0)), block_diag(m_pow[g])) for g in G]
        m_pow = [both[g][:C] for g in G]
        p_inv = [p_inv[g] + both[g][C:] for g in G]
    p_inv = [p_inv[g] + _dot(_bf(p_inv[g]), block_diag(m_pow[g])) for g in G]
    w_nat = [_dot(_bf(p_inv[g]), jnp.concatenate([_bf(x_kap[g]), _bf(g0[g])], axis=1)) for g in G]
    w12 = [_bf(jnp.concatenate([stacked(w_nat[g][:, :GROUP_W]), stacked(w_nat[g][:, GROUP_W:])], axis=1))
           for g in G]
    aw = [_dot(a_rb[g], w12[g]) for g in G]
    def heads_transposed(t):
        tt = t.T
        return jnp.concatenate([tt[j * HEAD:(j + 1) * HEAD] for j in range(GROUP_HEADS)], axis=1)

    bw = [_dot(_bf(heads_transposed(nat(_BH, g))), w12[g]) for g in G]
    kv = [_dot(_bf(heads_transposed(nat(_KH, g))), x_v[g]) for g in G]
    h_prev = [H_scr[g // N_GROUPS, g % N_GROUPS] for g in G]
    ys = []
    h_new = []
    for g in G:
        h_hi = block_diag(h_prev[g])
        h_lo = block_diag(h_prev[g] - _bf(h_prev[g]).astype(F32))
        qb = _bf(nat(_RT, g) - aw[g][:, :GROUP_W])
        m_side = eye_side * pc_scr[g // N_GROUPS, :, lanes(g)] - bw[g][:, :GROUP_W]
        m_hi, m_lo = _split(m_side)
        by_hi = _dot(jnp.concatenate([m_hi, m_lo, qb], axis=0), h_hi)
        by_lo = _dot(jnp.concatenate([m_hi, qb], axis=0), h_lo)
        ys.append(by_hi[2 * C:] + by_lo[C:] + y0[g] - aw[g][:, GROUP_W:])
        h_new.append(by_hi[:C] + by_hi[C:2 * C] + by_lo[:C] + kv[g] - bw[g][:, GROUP_W:])
    for rr in range(R):
        y_ref[0, rr] = jnp.concatenate(ys[rr * N_GROUPS:(rr + 1) * N_GROUPS], axis=1)
    for g in G:
        H_scr[g // N_GROUPS, g % N_GROUPS] = h_new[g]

    @pl.when(c == n_chunks - 1)
    def _():
        hT_ref[0] = H_scr[...]


def _rwkv(z, pr, h0):
    B, L, N = z.shape
    C = CHUNK
    nC = L // C
    D = D_MODEL

    def cidx(d, c):
        return c + d * (nC - 1 - 2 * c)

    zmap = lambda d, b, c: (b, cidx(d, c), 0)
    const2 = lambda d, b, c: (0, 0)
    dir3 = lambda d, b, c: (d, 0, 0)
    omap = lambda d, b, c: (d, b, cidx(d, c), 0)
    smap = lambda d, b, c: (d, b, 0, 0, 0)
    R = RWKV_ROWS_PER_STEP if B % RWKV_ROWS_PER_STEP == 0 else 1
    kern = functools.partial(_rwkv_kernel, n_chunks=nC)
    return pl.pallas_call(
        kern,
        grid=(2, B // R, nC),
        in_specs=[pl.BlockSpec((R, C, N), zmap),
                  pl.BlockSpec((1, 1, D), dir3),
                  pl.BlockSpec((1, 2 * LORA_W, D), dir3),
                  pl.BlockSpec((1, 2 * LORA_W, D), dir3),
                  pl.BlockSpec((1, 1, D), dir3),
                  pl.BlockSpec((1, 2 * LORA_A, D), dir3),
                  pl.BlockSpec((1, 2 * LORA_A, D), dir3),
                  pl.BlockSpec((1, D), const2),
                  pl.BlockSpec((1, D), const2),
                  pl.BlockSpec((1, D), const2),
                  pl.BlockSpec((D, 128), const2),
                  pl.BlockSpec((128, D), const2),
                  pl.BlockSpec((1, C, C), dir3),
                  pl.BlockSpec((1, GROUP_W, GROUP_W), dir3),
                  pl.BlockSpec((1, GROUP_W, GROUP_W), dir3),
                  pl.BlockSpec((1, R, N_GROUPS, HEAD, GROUP_W), smap)],
        out_specs=[pl.BlockSpec((1, R, C, D), omap),
                   pl.BlockSpec((1, R, C, D), omap),
                   pl.BlockSpec((1, R, C, LORA_G_PAD), omap),
                   pl.BlockSpec((1, R, N_GROUPS, HEAD, GROUP_W), smap)],
        out_shape=[jax.ShapeDtypeStruct((2, B, L, D), F32),
                   jax.ShapeDtypeStruct((2, B, L, D), F32),
                   jax.ShapeDtypeStruct((2, B, L, LORA_G_PAD), F32),
                   jax.ShapeDtypeStruct((2, B, N_GROUPS, HEAD, GROUP_W), F32)],
        scratch_shapes=[pltpu.VMEM((R, N_GROUPS, HEAD, GROUP_W), F32),
                        pltpu.VMEM((R, 7, C, D), F32),
                        pltpu.VMEM((R, 1, D), F32)],
        compiler_params=_cp("arbitrary", "arbitrary", "arbitrary"),
        name="rwkv",
    )(z, pr["w0"], *pr["w2p"], pr["a0"], *pr["a2p"], pr["k_k"], pr["k_a"], pr["r_k"],
      pr["hs"], pr["hb"], pr["tri"], pr["msl"], pr["minc"], h0)


def _conv_kernel(za_ref, zb_ref, w_ref, o_ref, upad, *, L, stride, rows_per_step):
    pad = CONV_HALF * stride
    TC = za_ref.shape[-1]
    upad[0:pad, :] = jnp.zeros((pad, TC), F32)
    upad[pad + L:pad + L + pad, :] = jnp.zeros((pad, TC), F32)
    upad[pad:pad + L, :] = za_ref[0] * _sigmoid(zb_ref[0])
    RB = rows_per_step

    def block(r0):
        acc = jnp.zeros((RB, TC), F32)
        for j in range(CONV_K):
            acc = acc + w_ref[j:j + 1, :] * upad[pl.ds(r0 + j * stride, RB), :]
        o_ref[0, pl.ds(r0, RB), :] = acc

    if stride % 8 == 0:
        def body(i, carry):
            block(pl.multiple_of(i * RB, RB))
            return carry
        lax.fori_loop(0, L // RB, body, 0)
    else:
        for i in range(L // RB):
            block(i * RB)


def _conv(zcg, conv_w, stride):
    B, L, _ = zcg.shape
    D = D_MODEL
    TC = 128
    nct = D // TC
    wpad = jnp.pad(conv_w, ((0, 32 - CONV_K), (0, 0)))
    RB = min(L, 128)
    kern = functools.partial(_conv_kernel, L=L, stride=stride, rows_per_step=RB)
    return pl.pallas_call(
        kern,
        grid=(B, nct),
        in_specs=[pl.BlockSpec((1, L, TC), lambda b, j: (b, 0, j)),
                  pl.BlockSpec((1, L, TC), lambda b, j: (b, 0, j + nct)),
                  pl.BlockSpec((32, TC), lambda b, j: (0, j))],
        out_specs=pl.BlockSpec((1, L, TC), lambda b, j: (b, 0, j)),
        out_shape=jax.ShapeDtypeStruct((B, L, D), F32),
        scratch_shapes=[pltpu.VMEM((L + 2 * CONV_HALF * stride, TC), F32)],
        compiler_params=_cp("parallel", "parallel"),
        name="conv",
    )(zcg, zcg, wpad)


def _post_kernel(x_ref, yf_ref, yb_ref, bf_ref, bb_ref, gs_ref, cv_ref, zg_ref,
                 lng_ref, lnb_ref, g2_ref, woa_ref, cng_ref, cnb_ref, wob_ref, gb_ref, wout_ref,
                 gt_ref, n2g_ref, sh2_ref, sc2_ref, hs_ref, hb_ref, xo_ref, hn_ref):
    D = D_MODEL
    hs = hs_ref[...]
    hb = hb_ref[...]

    def head_mean(t):
        return _head_sum(t, hs, hb) * (1.0 / HEAD)

    o = yf_ref[0, 0] + yb_ref[0, 0]
    oc = o - head_mean(o)
    on = oc * lax.rsqrt(head_mean(oc * oc) + GN_EPS) * lng_ref[...] + lnb_ref[...]
    on = on + bf_ref[0, 0] + bb_ref[0, 0]
    gate = _dot(_bf(gs_ref[0, 0]), g2_ref[...])
    y_a = _dot(_bf(on * gate), woa_ref[...])

    cv = cv_ref[0]
    cm = jnp.mean(cv, axis=-1, keepdims=True)
    cc = cv - cm
    cn = cc * lax.rsqrt(jnp.mean(cc * cc, axis=-1, keepdims=True) + LN_EPS) * cng_ref[...] + cnb_ref[...]
    y_b = _dot(_bf(cn * _sigmoid(cn)), wob_ref[...])

    gates = _sigmoid(zg_ref[0] + gb_ref[...])
    m = gates[:, :D] * y_a + gates[:, D:] * y_b
    xn = x_ref[0] + gt_ref[0] * _dot(_bf(m), wout_ref[...])
    xo_ref[0] = xn
    hn_ref[0] = _bf(_rms(xn, n2g_ref[...]) * (1.0 + sc2_ref[0]) + sh2_ref[0])


def _post(x, y, bv, gs, cv, zcg, pr, gt1, sh2, sc2):
    B, L, D = x.shape
    TM = min(L, 256)
    tok = lambda b, i: (b, i, 0)
    fwd = lambda b, i: (0, b, i, 0)
    bwd = lambda b, i: (1, b, i, 0)
    cst = lambda b, i: (0, 0)
    per_b = lambda b, i: (b, 0, 0)
    row = pl.BlockSpec((1, D), cst)
    mat = pl.BlockSpec((D, D), cst)
    return pl.pallas_call(
        _post_kernel,
        grid=(B, L // TM),
        in_specs=[pl.BlockSpec((1, TM, D), tok),
                  pl.BlockSpec((1, 1, TM, D), fwd), pl.BlockSpec((1, 1, TM, D), bwd),
                  pl.BlockSpec((1, 1, TM, D), fwd), pl.BlockSpec((1, 1, TM, D), bwd),
                  pl.BlockSpec((1, 1, TM, LORA_G_PAD), fwd),
                  pl.BlockSpec((1, TM, D), tok),
                  pl.BlockSpec((1, TM, 2 * D), lambda b, i: (b, i, 1)),
                  row, row, pl.BlockSpec((LORA_G_PAD, D), cst), mat,
                  row, row, mat, pl.BlockSpec((1, 2 * D), cst), mat,
                  pl.BlockSpec((1, 1, D), per_b), row,
                  pl.BlockSpec((1, 1, D), per_b), pl.BlockSpec((1, 1, D), per_b),
                  pl.BlockSpec((D, 128), cst), pl.BlockSpec((128, D), cst)],
        out_specs=[pl.BlockSpec((1, TM, D), tok), pl.BlockSpec((1, TM, D), tok)],
        out_shape=[jax.ShapeDtypeStruct((B, L, D), F32), jax.ShapeDtypeStruct((B, L, D), BF16)],
        compiler_params=_cp("parallel", "parallel"),
        name="post",
    )(x, y, y, bv, bv, gs, cv, zcg,
      pr["lnx_g"], pr["lnx_b"], pr["g2p"], pr["w_oA"], pr["cnorm_g"], pr["cnorm_b"], pr["w_oB"],
      pr["gate_b"], pr["w_out"], gt1, pr["norm2_g"], sh2, sc2, pr["hs"], pr["hb"])


def _top16(s, rowid):
    rank = jnp.full(s.shape, NOT_SELECTED, F32)
    cur = s
    vals = []
    for r in range(PEER_TOPK):
        m = jnp.max(cur, axis=0, keepdims=True)
        idx = jnp.min(jnp.where(cur == m, rowid, 1e9), axis=0, keepdims=True)
        sel = rowid == idx
        rank = jnp.where(sel, float(r), rank)
        cur = jnp.where(sel, NEG_INF, cur)
        vals.append(m)
    return rank, vals


def _top16_untied(arrays):
    n = range(len(arrays))
    rank = [jnp.full(s.shape, NOT_SELECTED, F32) for s in arrays]
    cur = list(arrays)
    vals = [[] for _ in n]
    for r in range(PEER_TOPK):
        m = [jnp.max(cur[i], axis=0, keepdims=True) for i in n]
        sel = [cur[i] == m[i] for i in n]
        rank = [jnp.where(sel[i], float(r), rank[i]) for i in n]
        cur = [jnp.where(sel[i], NEG_INF, cur[i]) for i in n]
        for i in n:
            vals[i].append(m[i])
    n_ranked = [jnp.sum(jnp.where(rank[i] < float(PEER_TOPK), 1.0, 0.0), axis=0, keepdims=True) for i in n]
    return rank, vals, n_ranked


def _peer_prep_kernel(h_ref, wq_ref, keys_ref, r2_ref, na_ref, e1_ref, e2_ref, q_scr, rk_scr, vl_scr):
    TM = h_ref.shape[0]
    K = PEER_TOPK
    q_scr[...] = _bf(_dot(h_ref[...], wq_ref[...]))
    rowid = lax.broadcasted_iota(jnp.int32, (PEER_NKEYS, TM), 0).astype(F32)
    kaid = lax.broadcasted_iota(jnp.int32, (K, TM), 0).astype(F32)

    def head(h, carry):
        off = pl.multiple_of(h * 2 * PEER_HALF, 2 * PEER_HALF)
        s1 = _dot_nt(keys_ref[h, 0], q_scr[:, pl.ds(off, PEER_HALF)])
        s2 = _dot_nt(keys_ref[h, 1], q_scr[:, pl.ds(off + PEER_HALF, PEER_HALF)])
        ranks, valss, n_ranked = _top16_untied([s1, s2])
        for half in range(2):
            rk_scr[half] = ranks[half]
            vl_scr[half] = jnp.concatenate(valss[half], axis=0)
        n_max = jnp.maximum(n_ranked[0], n_ranked[1])

        @pl.when(jnp.max(n_max) > float(K))
        def _():
            for half, s in ((0, s1), (1, s2)):
                rank, vals = _top16(s, rowid)
                rk_scr[half] = rank
                vl_scr[half] = jnp.concatenate(vals, axis=0)

        rank1 = rk_scr[0]
        rank2 = rk_scr[1]
        v1 = vl_scr[0]
        vals1 = [v1[r:r + 1] for r in range(K)]
        vals2 = [vl_scr[1, r:r + 1, :] for r in range(K)]
        taken = jnp.zeros((K, TM), F32)
        front = v1 + vals2[0]
        for _ in range(K):
            m = jnp.max(front, axis=0, keepdims=True)
            idx = jnp.min(jnp.where(front == m, kaid, 1e9), axis=0, keepdims=True)
            sel = kaid == idx
            taken = taken + jnp.where(sel, 1.0, 0.0)
            t_sel = jnp.max(jnp.where(sel, taken, 0.0), axis=0, keepdims=True)
            nxt = jnp.full((1, TM), NEG_INF, F32)
            for kb in range(1, K):
                nxt = jnp.where(t_sel == float(kb), vals2[kb], nxt)
            front = jnp.where(sel, v1 + nxt, front)
        e1k = jnp.exp(v1 - vals1[0])
        pref = jnp.zeros((1, TM), F32)
        zrow = jnp.zeros((K, TM), F32)
        for kb in range(K):
            pref = pref + jnp.exp(vals2[kb] - vals2[0])
            zrow = jnp.where(taken == float(kb + 1), pref, zrow)
        z = jnp.sum(e1k * zrow, axis=0, keepdims=True)
        na = jnp.zeros((PEER_NKEYS, TM), F32)
        for ka in range(K):
            na = jnp.where(rank1 == float(ka), taken[ka:ka + 1], na)
        r2_ref[h] = _bf(rank2)
        na_ref[h] = na
        e1_ref[h] = jnp.where(rank1 < float(K), jnp.exp(s1 - vals1[0]) / z, 0.0)
        e2_ref[h] = _bf(jnp.where(rank2 < float(K), jnp.exp(s2 - vals2[0]), 0.0))
        return carry

    lax.fori_loop(0, PEER_HEADS, head, 0)


def _peer_prep(hn, wq, keys, TM):
    T, D = hn.shape
    Q = wq.shape[1]
    shp = jax.ShapeDtypeStruct((PEER_HEADS, PEER_NKEYS, T), F32)
    shp_bf = jax.ShapeDtypeStruct((PEER_HEADS, PEER_NKEYS, T), BF16)
    ospec = pl.BlockSpec((PEER_HEADS, PEER_NKEYS, TM), lambda i: (0, 0, i))
    return pl.pallas_call(
        _peer_prep_kernel,
        grid=(T // TM,),
        in_specs=[pl.BlockSpec((TM, D), lambda i: (i, 0)),
                  pl.BlockSpec((D, Q), lambda i: (0, 0)),
                  pl.BlockSpec((PEER_HEADS, 2, PEER_NKEYS, PEER_HALF), lambda i: (0, 0, 0, 0))],
        out_specs=[ospec, ospec, ospec, ospec],
        out_shape=[shp_bf, shp, shp, shp_bf],
        scratch_shapes=[pltpu.VMEM((TM, Q), BF16),
                        pltpu.VMEM((2, PEER_NKEYS, TM), F32), pltpu.VMEM((2, PEER_TOPK, TM), F32)],
        compiler_params=_cp("parallel"),
        name="peer_prep",
    )(hn, wq, keys)


def _peer_dense_kernel(h_ref, r2_ref, na_ref, e1_ref, e2_ref, u_ref, vt_ref, x_ref, gt_ref, fg_ref,
                       o_ref, acc_t, ht_scr, s0_scr, s1_scr, aw0_scr, aw1_scr, *, block, n_blocks, final_norm):
    jj = pl.program_id(1)
    TM = h_ref.shape[0]
    SLAB = PEER_NKEYS
    EB = block

    last = pl.num_programs(1) - 1
    TL = min(TM, PEER_LANE_TILE)
    zero_bf = jnp.zeros((SLAB, TL), BF16)

    def row_tile(ref, h, a, cols):
        t = _bf(jnp.broadcast_to(ref[h, pl.ds(a, 1), cols], (16, TL)))
        return jnp.concatenate([t] * (SLAB // 16), axis=0)

    def vpu_stage(k, s_ref, aw_ref):
        for p in range(EB // (2 * SLAB)):
            lo = p * 2 * SLAB
            for c0 in range(0, TM, TL):
                cols = slice(c0, c0 + TL)
                s = s_ref[lo:lo + 2 * SLAB, cols]
                act = 0.5 * s * (1.0 + lax.erf(s * SQRT_HALF))
                weights = []
                for half in range(2):
                    a = k * (EB // SLAB) + p * 2 + half
                    w = jnp.zeros((SLAB, TL), BF16)
                    for h in range(PEER_HEADS):
                        chosen = r2_ref[h, :, cols] < row_tile(na_ref, h, a, cols)
                        w = w + jnp.where(chosen, e2_ref[h, :, cols], zero_bf) * row_tile(e1_ref, h, a, cols)
                    weights.append(w)
                aw_ref[lo:lo + 2 * SLAB, cols] = act * jnp.concatenate(weights, axis=0)

    def scores(half):
        return _bf(_dot(u_ref[half * EB:(half + 1) * EB, :], ht_scr[...]))

    def outputs(half, aw_ref):
        return _dot(vt_ref[:, half * EB:(half + 1) * EB], aw_ref[...])

    @pl.when(jj == 0)
    def _():
        ht_scr[...] = _bf(h_ref[...].astype(F32).T)
        s0_scr[...] = scores(0)
        s1_scr[...] = scores(1)
        vpu_stage(0, s0_scr, aw0_scr)
        acc_t[...] = jnp.zeros_like(acc_t)

    @pl.when(jnp.logical_and(jj > 0, jj < last))
    def _():
        out0 = outputs(0, aw0_scr)
        vpu_stage(2 * jj - 1, s1_scr, aw1_scr)
        s0_scr[...] = scores(0)
        out1 = outputs(1, aw1_scr)
        s1_scr[...] = scores(1)
        vpu_stage(2 * jj, s0_scr, aw0_scr)
        acc_t[...] += out0 + out1

    @pl.when(jj == last)
    def _():
        out0 = outputs(0, aw0_scr)
        vpu_stage(n_blocks - 1, s1_scr, aw1_scr)
        out1 = outputs(1, aw1_scr)
        xn = x_ref[...] + gt_ref[0] * (acc_t[...] + out0 + out1).T
        o_ref[...] = _rms(xn, fg_ref[...]) if final_norm else xn


def _peer_dense(x, hn, sel, u, vt, gt2, TM, tiles_per_batch, final_g):
    T, D = x.shape
    E = u.shape[0]
    EB = 1024
    final_norm = final_g is not None
    fg = (final_g if final_norm else jnp.ones((D,), F32)).reshape(1, D)
    sspec = pl.BlockSpec((PEER_HEADS, PEER_NKEYS, TM), lambda i, e: (0, 0, i))
    nE = E // EB
    nP = nE // 2
    kern = functools.partial(_peer_dense_kernel, block=EB, n_blocks=nE, final_norm=final_norm)
    return pl.pallas_call(
        kern,
        grid=(T // TM, nP + 1),
        in_specs=[pl.BlockSpec((TM, D), lambda i, e: (i, 0)),
                  sspec, sspec, sspec, sspec,
                  pl.BlockSpec((2 * EB, D), lambda i, e: (jnp.minimum(e, nP - 1), 0)),
                  pl.BlockSpec((D, 2 * EB), lambda i, e: (0, jnp.maximum(e - 1, 0))),
                  pl.BlockSpec((TM, D), lambda i, e: (i, 0)),
                  pl.BlockSpec((1, 1, D), lambda i, e: (i // tiles_per_batch, 0, 0)),
                  pl.BlockSpec((1, D), lambda i, e: (0, 0))],
        out_specs=pl.BlockSpec((TM, D), lambda i, e: (i, 0)),
        out_shape=jax.ShapeDtypeStruct((T, D), F32),
        scratch_shapes=[pltpu.VMEM((D, TM), F32), pltpu.VMEM((D, TM), BF16),
                        pltpu.VMEM((EB, TM), BF16), pltpu.VMEM((EB, TM), BF16),
                        pltpu.VMEM((EB, TM), BF16), pltpu.VMEM((EB, TM), BF16)],
        compiler_params=_cp("parallel", "arbitrary"),
        name="peer_dense",
    )(hn, *sel, u, vt, x, gt2, fg)


def _direction_masks():
    t = jnp.arange(CHUNK)
    le = (t[None, :] <= t[:, None]).astype(F32)
    tri = jnp.stack([le, le.T])
    i = jnp.arange(GROUP_W)
    same = (i[:, None] // CHUNK) == (i[None, :] // CHUNK)
    ti, tj = i[:, None] % CHUNK, i[None, :] % CHUNK
    msl = jnp.stack([same & (tj < ti), same & (tj > ti)]).astype(F32)
    minc = jnp.stack([same & (tj <= ti), same & (tj >= ti)]).astype(F32)
    return tri, msl, minc


def _layer_params(l, w_in, shift_mu, w0, w2, a0, a2, g2, k_k, k_a, r_k, lnx_g, lnx_b, w_oA, conv_w,
                  cnorm_g, cnorm_b, w_oB, gate_b, w_out, norm2_g):
    D = D_MODEL
    row = lambda t: t.reshape(1, -1)
    zeros = jnp.zeros((LORA_W, D), F32)
    tri, msl, minc = _direction_masks()
    hs = (jnp.arange(D)[:, None] // HEAD == jnp.arange(128)[None, :]).astype(F32)
    pad_cols = P_RWKV_PAD - P_RWKV
    return dict(
        w_rk=_bf(jnp.pad(w_in[l][:, :P_RWKV], ((0, 0), (0, pad_cols)))),
        w_cg=_bf(w_in[l][:, P_RWKV:]),
        mu=jnp.pad(shift_mu[l], ((0, 0), (0, pad_cols))),
        w0=w0[l].reshape(2, 1, D), a0=a0[l].reshape(2, 1, D),
        w2p=_split(jnp.stack([jnp.concatenate([w2[l, 0], zeros]), jnp.concatenate([zeros, w2[l, 1]])])),
        a2p=_split(jnp.stack([jnp.concatenate([a2[l, 0], zeros]), jnp.concatenate([zeros, a2[l, 1]])])),
        g2p=_bf(jnp.pad(g2[l], ((0, LORA_G_PAD - LORA_G), (0, 0)))),
        k_k=row(k_k[l]), k_a=row(k_a[l]), r_k=row(r_k[l]),
        lnx_g=row(lnx_g[l]), lnx_b=row(lnx_b[l]), w_oA=_bf(w_oA[l]),
        conv_w=conv_w[l], cnorm_g=row(cnorm_g[l]), cnorm_b=row(cnorm_b[l]), w_oB=_bf(w_oB[l]),
        gate_b=row(gate_b[l]), w_out=_bf(w_out[l]), norm2_g=row(norm2_g[l]),
        hs=_bf(hs), hb=_bf(hs.T), tri=_bf(tri), msl=msl, minc=minc)


def _mixer(x, mod, pr, norm1_g, stride, h0, emit):
    B, L, D = x.shape
    sh1, sc1, gt1, sh2, sc2 = (mod[:, i:i + 1, :] for i in range(5))
    z_rk = _proj_shift(x, sh1, sc1, norm1_g, pr["w_rk"], pr["mu"])
    y, bv, gs, h_t = _rwkv(z_rk, pr, h0)
    if not emit:
        return None, None, h_t
    z_cg = _proj(x, sh1, sc1, norm1_g, pr["w_cg"])
    cv = _conv(z_cg, pr["conv_w"], stride)
    xn, hn = _post(x, y, bv, gs, cv, z_cg, pr, gt1, sh2, sc2)
    return xn, hn, h_t


def _peer(x, hn, wq, keys, u, v, gt2, final_g=None):
    B, L, D = x.shape
    T = B * L
    sel = _peer_prep(hn.reshape(T, D), wq, keys, min(L, PEER_LANE_TILE))
    TM = min(L, PEER_DENSE_TOKENS)
    out = _peer_dense(x.reshape(T, D), hn.reshape(T, D), sel, u, v, gt2, TM, L // TM, final_g)
    return out.reshape(B, L, D)


def kernel(x, c, ctx, c_ctx, ada_w, ada_b, norm1_g, norm2_g, w_in, shift_mu, w0, w2, a0, a2, g2, k_k, k_a, r_k, lnx_g, lnx_b, w_oA, conv_w, cnorm_g, cnorm_b, w_oB, gate_b, w_out, w_q, sub_keys, peer_u, peer_v, final_g):
    B, L, D = x.shape
    depth = ada_w.shape[0]
    xc = ctx
    n_rows = -(-(B + 1) // 8) * 8
    c_rows = jnp.pad(jnp.concatenate([c, c_ctx[None, :]], axis=0), ((0, n_rows - B - 1), (0, 0)))
    zero_state = jnp.zeros((2, B, N_GROUPS, HEAD, GROUP_W), F32)
    for l in range(depth):
        last = l == depth - 1
        pr = _layer_params(l, w_in, shift_mu, w0, w2, a0, a2, g2, k_k, k_a, r_k, lnx_g, lnx_b, w_oA,
                           conv_w, cnorm_g, cnorm_b, w_oB, gate_b, w_out, norm2_g)
        mod_all = _modulation(c_rows, ada_w[l], ada_b[l])
        mod = mod_all[:B].reshape(B, 6, D)
        modc = jnp.broadcast_to(mod_all[B].reshape(1, 6, D), (B, 6, D))
        wq = _bf(w_q[l])
        keys = _bf(sub_keys[l])
        u = _bf(peer_u[l])
        v = _bf(peer_v[l]).T

        xc_new, hnc, ctx_states = _mixer(xc, modc, pr, norm1_g[l], 1, zero_state, emit=not last)
        xn, hn, _ = _mixer(x, mod, pr, norm1_g[l], GRID_W, ctx_states, emit=True)
        x = _peer(xn, hn, wq, keys, u, v, mod[:, 5:6, :], final_g if last else None)
        if not last:
            xc = _peer(xc_new, hnc, wq, keys, u, v, modc[:, 5:6, :])
    return x
```

```python
import functools

import jax
import jax.numpy as jnp
from jax import lax
from jax.experimental import pallas as pl
from jax.experimental.pallas import tpu as pltpu

F32 = jnp.float32
BF16 = jnp.bfloat16
HI = lax.Precision.HIGHEST

D_MODEL = 1024
HEAD = 64
HEADS = D_MODEL // HEAD
GROUP_HEADS = 4
GROUP_W = GROUP_HEADS * HEAD
N_GROUPS = HEADS // GROUP_HEADS
CHUNK = 64
assert CHUNK == HEAD
RWKV_ROWS_PER_STEP = 4
LORA_W = 64
LORA_A = 64
LORA_G = 160
LORA_G_PAD = 256
P_RWKV = 3 * D_MODEL + 2 * LORA_W + 2 * LORA_A + LORA_G
P_RWKV_PAD = 3 * D_MODEL + 2 * LORA_W + 2 * LORA_A + LORA_G_PAD
COL_W1 = 3 * D_MODEL
COL_A1 = COL_W1 + 2 * LORA_W
COL_G1 = COL_A1 + 2 * LORA_A
CONV_K = 31
CONV_HALF = CONV_K // 2
GRID_W = 64
PEER_HEADS = 8
PEER_NKEYS = 128
PEER_HALF = 128
PEER_TOPK = 16
PEER_LANE_TILE = 256
PEER_DENSE_TOKENS = 512
NORM_EPS = 1e-6
LN_EPS = 1e-5
GN_EPS = HEAD * 1e-5
VMEM_LIMIT = 56 * 1024 * 1024
NOT_SELECTED = 99.0
NEG_INF = float("-inf")
SQRT_HALF = 0.7071067811865476
EXP_MINUS_HALF = 0.6065306597126334


def _cp(*sem):
    return pltpu.CompilerParams(dimension_semantics=sem, vmem_limit_bytes=VMEM_LIMIT)


def _dot(a, b):
    return jnp.dot(a, b, preferred_element_type=F32)


def _dot_hi(a, b):
    return jnp.dot(a, b, precision=HI, preferred_element_type=F32)


def _dot_nt(a, b):
    return lax.dot_general(a, b, (((1,), (1,)), ((), ())), preferred_element_type=F32)


def _bf(a):
    return a.astype(BF16)


def _split(a):
    hi = a.astype(BF16)
    return hi, (a - hi.astype(F32)).astype(BF16)


def _dot_split_lhs(a, b_bf):
    hi, lo = _split(a)
    return _dot(hi, b_bf) + _dot(lo, b_bf)


def _dot_split(a, b_hi, b_lo):
    hi, lo = _split(a)
    return _dot(hi, b_hi) + _dot(hi, b_lo) + _dot(lo, b_hi)


def _head_sum(t, hs_bf, hb_bf):
    return _dot_split_lhs(_dot_split_lhs(t, hs_bf), hb_bf)


def _sigmoid(x):
    return jax.nn.sigmoid(x)


def _rms(x, g):
    return x * lax.rsqrt(jnp.mean(x * x, axis=-1, keepdims=True) + NORM_EPS) * g


def _mod_kernel(c_ref, w_ref, b_ref, o_ref):
    c = c_ref[...]
    o_ref[...] = _dot_hi(c * _sigmoid(c), w_ref[...]) + b_ref[...]


def _modulation(c_rows, ada_w, ada_b):
    R, D = c_rows.shape
    N = ada_w.shape[1]
    TN = 512
    return pl.pallas_call(
        _mod_kernel,
        grid=(N // TN,),
        in_specs=[pl.BlockSpec((R, D), lambda j: (0, 0)),
                  pl.BlockSpec((D, TN), lambda j: (0, j)),
                  pl.BlockSpec((1, TN), lambda j: (0, j))],
        out_specs=pl.BlockSpec((R, TN), lambda j: (0, j)),
        out_shape=jax.ShapeDtypeStruct((R, N), F32),
        compiler_params=_cp("arbitrary"),
        name="modulation",
    )(c_rows, ada_w, ada_b.reshape(1, N))


def _proj_kernel(x_ref, sh_ref, sc_ref, g_ref, w_ref, o_ref):
    h = _rms(x_ref[0], g_ref[...]) * (1.0 + sc_ref[0]) + sh_ref[0]
    o_ref[0] = _dot(_bf(h), w_ref[...])


def _proj(x, sh, sc, g, w):
    B, L, D = x.shape
    N = w.shape[1]
    TM = min(L, 256)
    return pl.pallas_call(
        _proj_kernel,
        grid=(B, L // TM),
        in_specs=[pl.BlockSpec((1, TM, D), lambda b, i: (b, i, 0)),
                  pl.BlockSpec((1, 1, D), lambda b, i: (b, 0, 0)),
                  pl.BlockSpec((1, 1, D), lambda b, i: (b, 0, 0)),
                  pl.BlockSpec((1, D), lambda b, i: (0, 0)),
                  pl.BlockSpec((D, N), lambda b, i: (0, 0))],
        out_specs=pl.BlockSpec((1, TM, N), lambda b, i: (b, i, 0)),
        out_shape=jax.ShapeDtypeStruct((B, L, N), F32),
        compiler_params=_cp("parallel", "parallel"),
        name="proj",
    )(x, sh, sc, g.reshape(1, D), w)


def _proj_shift_kernel(x_ref, xp_ref, xn_ref, sh_ref, sc_ref, g_ref, w_ref, mu_ref, o_ref):
    i = pl.program_id(1)
    TM = x_ref.shape[1]
    xs = jnp.concatenate([xp_ref[0], x_ref[0], xn_ref[0]], axis=0)
    h = _rms(xs, g_ref[...]) * (1.0 + sc_ref[0]) + sh_ref[0]
    z = _dot(_bf(h), w_ref[...])
    row = lax.broadcasted_iota(jnp.int32, (TM, 1), 0)
    at_start = jnp.logical_and(row == 0, i == 0)
    at_end = jnp.logical_and(row == TM - 1, i == pl.num_programs(1) - 1)
    zp = jnp.where(at_start, 0.0, z[7:7 + TM])
    zn = jnp.where(at_end, 0.0, z[9:9 + TM])
    m0 = mu_ref[0:1, :]
    m1 = mu_ref[1:2, :]
    o_ref[0] = z[8:8 + TM] * (1.0 - m0 - m1) + m0 * zp + m1 * zn


def _proj_shift(x, sh, sc, g, w, mu):
    B, L, D = x.shape
    N = w.shape[1]
    TM = min(L, 256)
    nb8 = L // 8
    return pl.pallas_call(
        _proj_shift_kernel,
        grid=(B, L // TM),
        in_specs=[pl.BlockSpec((1, TM, D), lambda b, i: (b, i, 0)),
                  pl.BlockSpec((1, 8, D), lambda b, i: (b, jnp.maximum(i * (TM // 8) - 1, 0), 0)),
                  pl.BlockSpec((1, 8, D), lambda b, i: (b, jnp.minimum((i + 1) * (TM // 8), nb8 - 1), 0)),
                  pl.BlockSpec((1, 1, D), lambda b, i: (b, 0, 0)),
                  pl.BlockSpec((1, 1, D), lambda b, i: (b, 0, 0)),
                  pl.BlockSpec((1, D), lambda b, i: (0, 0)),
                  pl.BlockSpec((D, N), lambda b, i: (0, 0)),
                  pl.BlockSpec((2, N), lambda b, i: (0, 0))],
        out_specs=pl.BlockSpec((1, TM, N), lambda b, i: (b, i, 0)),
        out_shape=jax.ShapeDtypeStruct((B, L, N), F32),
        compiler_params=_cp("parallel", "parallel"),
        name="proj_shift",
    )(x, x, x, sh, sc, g.reshape(1, D), w, mu)


_KAP, _RT, _KT, _BT, _KH, _BH, _V = range(7)


def _rwkv_kernel(z_ref, w0_ref, w2h_ref, w2l_ref, a0_ref, a2h_ref, a2l_ref,
                 kk_ref, ka_ref, rk_ref, hs_ref, hb_ref, tri_ref, msl_ref, minc_ref, h0_ref,
                 y_ref, bv_ref, gs_ref, hT_ref, H_scr, nat_scr, pc_scr, *, n_chunks):
    d = pl.program_id(0)
    c = pl.program_id(2)
    cc = jnp.where(d == 0, c, n_chunks - 1 - c)
    C = CHUNK

    R = z_ref.shape[0]

    @pl.when(c == 0)
    def _():
        H_scr[...] = h0_ref[0]

    hs = hs_ref[...]
    hb = hb_ref[...]

    def head_sum(t):
        return _head_sum(t, hs, hb)

    def prepare(rr):
        def cols(lo, hi):
            return z_ref[rr, :, lo:hi]

        r = cols(0, D_MODEL)
        k = cols(D_MODEL, 2 * D_MODEL)
        v = cols(2 * D_MODEL, 3 * D_MODEL)
        w1 = cols(COL_W1, COL_A1)
        a1 = cols(COL_A1, COL_G1)
        g1 = cols(COL_G1, P_RWKV_PAD)

        wl = w0_ref[0] + _dot_split(jnp.tanh(w1), w2h_ref[0], w2l_ref[0])
        logw = -EXP_MINUS_HALF * _sigmoid(wl)
        a = _sigmoid(_dot_split(a1, a2h_ref[0], a2l_ref[0]) + a0_ref[0])
        kkr = k * kk_ref[...]
        inv_norm = lax.rsqrt(jnp.maximum(_dot_split_lhs(kkr * kkr, hs), 1e-24))
        kk = kkr * _dot_split_lhs(inv_norm, hb)
        kd = k * (1.0 + (a - 1.0) * ka_ref[...])
        bb = kk * a
        bv_ref[0, rr] = head_sum(r * kd * rk_ref[...]) * v
        gs_ref[0, rr] = _sigmoid(g1)

        lw_hi, lw_lo = _split(logw)
        g_in = _dot(tri_ref[0], lw_hi) + _dot(tri_ref[0], lw_lo)
        g_ex = g_in - logw
        g_c = jnp.sum(logw, axis=0, keepdims=True)
        e_inv = jnp.exp(-g_in)
        e_hat = jnp.exp(g_c - g_in)
        nat_scr[rr, _KAP] = kk * jnp.exp(g_ex)
        nat_scr[rr, _RT] = r * jnp.exp(g_in)
        nat_scr[rr, _KT] = kd * e_inv
        nat_scr[rr, _BT] = bb * e_inv
        nat_scr[rr, _KH] = kd * e_hat
        nat_scr[rr, _BH] = bb * e_hat
        nat_scr[rr, _V] = v
        pc_scr[rr] = jnp.exp(g_c)

    for rr in range(R):
        prepare(rr)

    lane_head = lax.broadcasted_iota(jnp.int32, (C, GROUP_W), 1) // HEAD
    ii = lax.broadcasted_iota(jnp.int32, (GROUP_W, GROUP_W), 0)
    jj = lax.broadcasted_iota(jnp.int32, (GROUP_W, GROUP_W), 1)
    eye = ii == jj

    def stacked(t):
        return jnp.concatenate([jnp.where(lane_head == j, t, 0.0) for j in range(GROUP_HEADS)], axis=0)

    def collapse(t):
        return t[0:C] + t[C:2 * C] + t[2 * C:3 * C] + t[3 * C:4 * C]

    G = range(R * N_GROUPS)

    def lanes(q):
        return slice((q % N_GROUPS) * GROUP_W, (q % N_GROUPS + 1) * GROUP_W)

    def nat(i, q):
        return nat_scr[q // N_GROUPS, i, :, lanes(q)]

    x_kap = [stacked(nat(_KAP, g)) for g in G]
    x_v = [_bf(stacked(nat(_V, g))) for g in G]
    x_bk = [jnp.concatenate([_bf(stacked(nat(_BT, g))), _bf(stacked(nat(_KT, g)))], axis=0) for g in G]
    kr = [_bf(jnp.concatenate([nat(_KAP, g), nat(_RT, g)], axis=0)) for g in G]
    akr = [_dot_nt(kr[g], x_bk[g]) for g in G]

    def tiled(t):
        return jnp.concatenate([t] * GROUP_HEADS, axis=0)

    same_head = (ii // C) == (jj // C)

    def block_diag(side):
        t = tiled(_bf(side))
        return jnp.where(same_head, t, jnp.zeros_like(t))

    msl_c = collapse(msl_ref[0]) > 0.5
    minc_c = collapse(minc_ref[0]) > 0.5
    n_side = [jnp.where(msl_c, akr[g][:C, :GROUP_W], 0.0) for g in G]
    a_kk = [_bf(jnp.where(msl_c, akr[g][:C, GROUP_W:], 0.0)) for g in G]
    a_rb = [_bf(jnp.where(minc_c, akr[g][C:, :GROUP_W], 0.0)) for g in G]
    a_rk = [_bf(jnp.where(minc_c, akr[g][C:, GROUP_W:], 0.0)) for g in G]
    g0 = [stacked(_dot(a_kk[g], x_v[g])) for g in G]
    y0 = [_dot(a_rk[g], x_v[g]) for g in G]
    eye_side = collapse(jnp.where(eye, 1.0, 0.0))
    p_inv = [eye_side - n_side[g] for g in G]
    m_pow = [_dot(_bf(n_side[g]), block_diag(n_side[g])) for g in G]
    for _ in range(4):
        both = [_dot(_bf(jnp.concatenate([m_pow[g], p_inv[g]], axis=0)), block_diag(m_pow[g])) for g in G]
        m_pow = [both[g][:C] for g in G]
        p_inv = [p_inv[g] + both[g][C:] for g in G]
    p_inv = [p_inv[g] + _dot(_bf(p_inv[g]), block_diag(m_pow[g])) for g in G]
    w_nat = [_dot(_bf(p_inv[g]), jnp.concatenate([_bf(x_kap[g]), _bf(g0[g])], axis=1)) for g in G]
    w12 = [_bf(jnp.concatenate([stacked(w_nat[g][:, :GROUP_W]), stacked(w_nat[g][:, GROUP_W:])], axis=1))
           for g in G]
    aw = [_dot(a_rb[g], w12[g]) for g in G]
    def heads_transposed(t):
        tt = t.T
        return jnp.concatenate([tt[j * HEAD:(j + 1) * HEAD] for j in range(GROUP_HEADS)], axis=1)

    bw = [_dot(_bf(heads_transposed(nat(_BH, g))), w12[g]) for g in G]
    kv = [_dot(_bf(heads_transposed(nat(_KH, g))), x_v[g]) for g in G]
    h_prev = [H_scr[g // N_GROUPS, g % N_GROUPS] for g in G]
    ys = []
    h_new = []
    for g in G:
        h_hi = block_diag(h_prev[g])
        h_lo = block_diag(h_prev[g] - _bf(h_prev[g]).astype(F32))
        qb = _bf(nat(_RT, g) - aw[g][:, :GROUP_W])
        m_side = eye_side * pc_scr[g // N_GROUPS, :, lanes(g)] - bw[g][:, :GROUP_W]
        m_hi, m_lo = _split(m_side)
        by_hi = _dot(jnp.concatenate([m_hi, m_lo, qb], axis=0), h_hi)
        by_lo = _dot(jnp.concatenate([m_hi, qb], axis=0), h_lo)
        ys.append(by_hi[2 * C:] + by_lo[C:] + y0[g] - aw[g][:, GROUP_W:])
        h_new.append(by_hi[:C] + by_hi[C:2 * C] + by_lo[:C] + kv[g] - bw[g][:, GROUP_W:])
    for rr in range(R):
        y_ref[0, rr] = jnp.concatenate(ys[rr * N_GROUPS:(rr + 1) * N_GROUPS], axis=1)
    for g in G:
        H_scr[g // N_GROUPS, g % N_GROUPS] = h_new[g]

    @pl.when(c == n_chunks - 1)
    def _():
        hT_ref[0] = H_scr[...]


def _rwkv(z, pr, h0):
    B, L, N = z.shape
    C = CHUNK
    nC = L // C
    D = D_MODEL

    def cidx(d, c):
        return c + d * (nC - 1 - 2 * c)

    zmap = lambda d, b, c: (b, cidx(d, c), 0)
    const2 = lambda d, b, c: (0, 0)
    dir3 = lambda d, b, c: (d, 0, 0)
    omap = lambda d, b, c: (d, b, cidx(d, c), 0)
    smap = lambda d, b, c: (d, b, 0, 0, 0)
    R = RWKV_ROWS_PER_STEP if B % RWKV_ROWS_PER_STEP == 0 else 1
    kern = functools.partial(_rwkv_kernel, n_chunks=nC)
    return pl.pallas_call(
        kern,
        grid=(2, B // R, nC),
        in_specs=[pl.BlockSpec((R, C, N), zmap),
                  pl.BlockSpec((1, 1, D), dir3),
                  pl.BlockSpec((1, 2 * LORA_W, D), dir3),
                  pl.BlockSpec((1, 2 * LORA_W, D), dir3),
                  pl.BlockSpec((1, 1, D), dir3),
                  pl.BlockSpec((1, 2 * LORA_A, D), dir3),
                  pl.BlockSpec((1, 2 * LORA_A, D), dir3),
                  pl.BlockSpec((1, D), const2),
                  pl.BlockSpec((1, D), const2),
                  pl.BlockSpec((1, D), const2),
                  pl.BlockSpec((D, 128), const2),
                  pl.BlockSpec((128, D), const2),
                  pl.BlockSpec((1, C, C), dir3),
                  pl.BlockSpec((1, GROUP_W, GROUP_W), dir3),
                  pl.BlockSpec((1, GROUP_W, GROUP_W), dir3),
                  pl.BlockSpec((1, R, N_GROUPS, HEAD, GROUP_W), smap)],
        out_specs=[pl.BlockSpec((1, R, C, D), omap),
                   pl.BlockSpec((1, R, C, D), omap),
                   pl.BlockSpec((1, R, C, LORA_G_PAD), omap),
                   pl.BlockSpec((1, R, N_GROUPS, HEAD, GROUP_W), smap)],
        out_shape=[jax.ShapeDtypeStruct((2, B, L, D), F32),
                   jax.ShapeDtypeStruct((2, B, L, D), F32),
                   jax.ShapeDtypeStruct((2, B, L, LORA_G_PAD), F32),
                   jax.ShapeDtypeStruct((2, B, N_GROUPS, HEAD, GROUP_W), F32)],
        scratch_shapes=[pltpu.VMEM((R, N_GROUPS, HEAD, GROUP_W), F32),
                        pltpu.VMEM((R, 7, C, D), F32),
                        pltpu.VMEM((R, 1, D), F32)],
        compiler_params=_cp("arbitrary", "arbitrary", "arbitrary"),
        name="rwkv",
    )(z, pr["w0"], *pr["w2p"], pr["a0"], *pr["a2p"], pr["k_k"], pr["k_a"], pr["r_k"],
      pr["hs"], pr["hb"], pr["tri"], pr["msl"], pr["minc"], h0)


def _conv_kernel(za_ref, zb_ref, w_ref, o_ref, upad, *, L, stride, rows_per_step):
    pad = CONV_HALF * stride
    TC = za_ref.shape[-1]
    upad[0:pad, :] = jnp.zeros((pad, TC), F32)
    upad[pad + L:pad + L + pad, :] = jnp.zeros((pad, TC), F32)
    upad[pad:pad + L, :] = za_ref[0] * _sigmoid(zb_ref[0])
    RB = rows_per_step

    def block(r0):
        acc = jnp.zeros((RB, TC), F32)
        for j in range(CONV_K):
            acc = acc + w_ref[j:j + 1, :] * upad[pl.ds(r0 + j * stride, RB), :]
        o_ref[0, pl.ds(r0, RB), :] = acc

    if stride % 8 == 0:
        def body(i, carry):
            block(pl.multiple_of(i * RB, RB))
            return carry
        lax.fori_loop(0, L // RB, body, 0)
    else:
        for i in range(L // RB):
            block(i * RB)


def _conv(zcg, conv_w, stride):
    B, L, _ = zcg.shape
    D = D_MODEL
    TC = 128
    nct = D // TC
    wpad = jnp.pad(conv_w, ((0, 32 - CONV_K), (0, 0)))
    RB = min(L, 128)
    kern = functools.partial(_conv_kernel, L=L, stride=stride, rows_per_step=RB)
    return pl.pallas_call(
        kern,
        grid=(B, nct),
        in_specs=[pl.BlockSpec((1, L, TC), lambda b, j: (b, 0, j)),
                  pl.BlockSpec((1, L, TC), lambda b, j: (b, 0, j + nct)),
                  pl.BlockSpec((32, TC), lambda b, j: (0, j))],
        out_specs=pl.BlockSpec((1, L, TC), lambda b, j: (b, 0, j)),
        out_shape=jax.ShapeDtypeStruct((B, L, D), F32),
        scratch_shapes=[pltpu.VMEM((L + 2 * CONV_HALF * stride, TC), F32)],
        compiler_params=_cp("parallel", "parallel"),
        name="conv",
    )(zcg, zcg, wpad)


def _post_kernel(x_ref, yf_ref, yb_ref, bf_ref, bb_ref, gs_ref, cv_ref, zg_ref,
                 lng_ref, lnb_ref, g2_ref, woa_ref, cng_ref, cnb_ref, wob_ref, gb_ref, wout_ref,
                 gt_ref, n2g_ref, sh2_ref, sc2_ref, hs_ref, hb_ref, xo_ref, hn_ref):
    D = D_MODEL
    hs = hs_ref[...]
    hb = hb_ref[...]

    def head_mean(t):
        return _head_sum(t, hs, hb) * (1.0 / HEAD)

    o = yf_ref[0, 0] + yb_ref[0, 0]
    oc = o - head_mean(o)
    on = oc * lax.rsqrt(head_mean(oc * oc) + GN_EPS) * lng_ref[...] + lnb_ref[...]
    on = on + bf_ref[0, 0] + bb_ref[0, 0]
    gate = _dot(_bf(gs_ref[0, 0]), g2_ref[...])
    y_a = _dot(_bf(on * gate), woa_ref[...])

    cv = cv_ref[0]
    cm = jnp.mean(cv, axis=-1, keepdims=True)
    cc = cv - cm
    cn = cc * lax.rsqrt(jnp.mean(cc * cc, axis=-1, keepdims=True) + LN_EPS) * cng_ref[...] + cnb_ref[...]
    y_b = _dot(_bf(cn * _sigmoid(cn)), wob_ref[...])

    gates = _sigmoid(zg_ref[0] + gb_ref[...])
    m = gates[:, :D] * y_a + gates[:, D:] * y_b
    xn = x_ref[0] + gt_ref[0] * _dot(_bf(m), wout_ref[...])
    xo_ref[0] = xn
    hn_ref[0] = _bf(_rms(xn, n2g_ref[...]) * (1.0 + sc2_ref[0]) + sh2_ref[0])


def _post(x, y, bv, gs, cv, zcg, pr, gt1, sh2, sc2):
    B, L, D = x.shape
    TM = min(L, 256)
    tok = lambda b, i: (b, i, 0)
    fwd = lambda b, i: (0, b, i, 0)
    bwd = lambda b, i: (1, b, i, 0)
    cst = lambda b, i: (0, 0)
    per_b = lambda b, i: (b, 0, 0)
    row = pl.BlockSpec((1, D), cst)
    mat = pl.BlockSpec((D, D), cst)
    return pl.pallas_call(
        _post_kernel,
        grid=(B, L // TM),
        in_specs=[pl.BlockSpec((1, TM, D), tok),
                  pl.BlockSpec((1, 1, TM, D), fwd), pl.BlockSpec((1, 1, TM, D), bwd),
                  pl.BlockSpec((1, 1, TM, D), fwd), pl.BlockSpec((1, 1, TM, D), bwd),
                  pl.BlockSpec((1, 1, TM, LORA_G_PAD), fwd),
                  pl.BlockSpec((1, TM, D), tok),
                  pl.BlockSpec((1, TM, 2 * D), lambda b, i: (b, i, 1)),
                  row, row, pl.BlockSpec((LORA_G_PAD, D), cst), mat,
                  row, row, mat, pl.BlockSpec((1, 2 * D), cst), mat,
                  pl.BlockSpec((1, 1, D), per_b), row,
                  pl.BlockSpec((1, 1, D), per_b), pl.BlockSpec((1, 1, D), per_b),
                  pl.BlockSpec((D, 128), cst), pl.BlockSpec((128, D), cst)],
        out_specs=[pl.BlockSpec((1, TM, D), tok), pl.BlockSpec((1, TM, D), tok)],
        out_shape=[jax.ShapeDtypeStruct((B, L, D), F32), jax.ShapeDtypeStruct((B, L, D), BF16)],
        compiler_params=_cp("parallel", "parallel"),
        name="post",
    )(x, y, y, bv, bv, gs, cv, zcg,
      pr["lnx_g"], pr["lnx_b"], pr["g2p"], pr["w_oA"], pr["cnorm_g"], pr["cnorm_b"], pr["w_oB"],
      pr["gate_b"], pr["w_out"], gt1, pr["norm2_g"], sh2, sc2, pr["hs"], pr["hb"])


def _top16(s, rowid):
    rank = jnp.full(s.shape, NOT_SELECTED, F32)
    cur = s
    vals = []
    for r in range(PEER_TOPK):
        m = jnp.max(cur, axis=0, keepdims=True)
        idx = jnp.min(jnp.where(cur == m, rowid, 1e9), axis=0, keepdims=True)
        sel = rowid == idx
        rank = jnp.where(sel, float(r), rank)
        cur = jnp.where(sel, NEG_INF, cur)
        vals.append(m)
    return rank, vals


def _top16_untied(arrays):
    n = range(len(arrays))
    rank = [jnp.full(s.shape, NOT_SELECTED, F32) for s in arrays]
    cur = list(arrays)
    vals = [[] for _ in n]
    for r in range(PEER_TOPK):
        m = [jnp.max(cur[i], axis=0, keepdims=True) for i in n]
        sel = [cur[i] == m[i] for i in n]
        rank = [jnp.where(sel[i], float(r), rank[i]) for i in n]
        cur = [jnp.where(sel[i], NEG_INF, cur[i]) for i in n]
        for i in n:
            vals[i].append(m[i])
    n_ranked = [jnp.sum(jnp.where(rank[i] < float(PEER_TOPK), 1.0, 0.0), axis=0, keepdims=True) for i in n]
    return rank, vals, n_ranked


def _peer_prep_kernel(h_ref, wq_ref, keys_ref, r2_ref, na_ref, e1_ref, e2_ref, q_scr, rk_scr, vl_scr):
    TM = h_ref.shape[0]
    K = PEER_TOPK
    q_scr[...] = _bf(_dot(h_ref[...], wq_ref[...]))
    rowid = lax.broadcasted_iota(jnp.int32, (PEER_NKEYS, TM), 0).astype(F32)
    kaid = lax.broadcasted_iota(jnp.int32, (K, TM), 0).astype(F32)

    def head(h, carry):
        off = pl.multiple_of(h * 2 * PEER_HALF, 2 * PEER_HALF)
        s1 = _dot_nt(keys_ref[h, 0], q_scr[:, pl.ds(off, PEER_HALF)])
        s2 = _dot_nt(keys_ref[h, 1], q_scr[:, pl.ds(off + PEER_HALF, PEER_HALF)])
        ranks, valss, n_ranked = _top16_untied([s1, s2])
        for half in range(2):
            rk_scr[half] = ranks[half]
            vl_scr[half] = jnp.concatenate(valss[half], axis=0)
        n_max = jnp.maximum(n_ranked[0], n_ranked[1])

        @pl.when(jnp.max(n_max) > float(K))
        def _():
            for half, s in ((0, s1), (1, s2)):
                rank, vals = _top16(s, rowid)
                rk_scr[half] = rank
                vl_scr[half] = jnp.concatenate(vals, axis=0)

        rank1 = rk_scr[0]
        rank2 = rk_scr[1]
        v1 = vl_scr[0]
        vals1 = [v1[r:r + 1] for r in range(K)]
        vals2 = [vl_scr[1, r:r + 1, :] for r in range(K)]
        taken = jnp.zeros((K, TM), F32)
        front = v1 + vals2[0]
        for _ in range(K):
            m = jnp.max(front, axis=0, keepdims=True)
            idx = jnp.min(jnp.where(front == m, kaid, 1e9), axis=0, keepdims=True)
            sel = kaid == idx
            taken = taken + jnp.where(sel, 1.0, 0.0)
            nxt = jnp.full((K, TM), NEG_INF, F32)
            for kb in range(1, K):
                nxt = jnp.where(taken == float(kb), vals2[kb], nxt)
            front = jnp.where(sel, v1 + nxt, front)
        e1k = jnp.exp(v1 - vals1[0])
        pref = jnp.zeros((1, TM), F32)
        zrow = jnp.zeros((K, TM), F32)
        for kb in range(K):
            pref = pref + jnp.exp(vals2[kb] - vals2[0])
            zrow = jnp.where(taken == float(kb + 1), pref, zrow)
        z = jnp.sum(e1k * zrow, axis=0, keepdims=True)
        na = jnp.zeros((PEER_NKEYS, TM), F32)
        for ka in range(K):
            na = jnp.where(rank1 == float(ka), taken[ka:ka + 1], na)
        r2_ref[h] = _bf(rank2)
        na_ref[h] = na
        e1_ref[h] = jnp.where(rank1 < float(K), jnp.exp(s1 - vals1[0]) / z, 0.0)
        e2_ref[h] = _bf(jnp.where(rank2 < float(K), jnp.exp(s2 - vals2[0]), 0.0))
        return carry

    lax.fori_loop(0, PEER_HEADS, head, 0)


def _peer_prep(hn, wq, keys, TM):
    T, D = hn.shape
    Q = wq.shape[1]
    shp = jax.ShapeDtypeStruct((PEER_HEADS, PEER_NKEYS, T), F32)
    shp_bf = jax.ShapeDtypeStruct((PEER_HEADS, PEER_NKEYS, T), BF16)
    ospec = pl.BlockSpec((PEER_HEADS, PEER_NKEYS, TM), lambda i: (0, 0, i))
    return pl.pallas_call(
        _peer_prep_kernel,
        grid=(T // TM,),
        in_specs=[pl.BlockSpec((TM, D), lambda i: (i, 0)),
                  pl.BlockSpec((D, Q), lambda i: (0, 0)),
                  pl.BlockSpec((PEER_HEADS, 2, PEER_NKEYS, PEER_HALF), lambda i: (0, 0, 0, 0))],
        out_specs=[ospec, ospec, ospec, ospec],
        out_shape=[shp_bf, shp, shp, shp_bf],
        scratch_shapes=[pltpu.VMEM((TM, Q), BF16),
                        pltpu.VMEM((2, PEER_NKEYS, TM), F32), pltpu.VMEM((2, PEER_TOPK, TM), F32)],
        compiler_params=_cp("parallel"),
        name="peer_prep",
    )(hn, wq, keys)


def _peer_dense_kernel(h_ref, r2_ref, na_ref, e1_ref, e2_ref, u_ref, vt_ref, x_ref, gt_ref, fg_ref,
                       o_ref, acc_t, ht_scr, s0_scr, s1_scr, aw0_scr, aw1_scr, *, block, n_blocks, final_norm):
    jj = pl.program_id(1)
    TM = h_ref.shape[0]
    SLAB = PEER_NKEYS
    EB = block

    last = pl.num_programs(1) - 1
    TL = min(TM, PEER_LANE_TILE)
    zero_bf = jnp.zeros((SLAB, TL), BF16)

    def row_tile(ref, h, a, cols):
        t = _bf(jnp.broadcast_to(ref[h, pl.ds(a, 1), cols], (16, TL)))
        return jnp.concatenate([t] * (SLAB // 16), axis=0)

    def vpu_stage(k, s_ref, aw_ref):
        for p in range(EB // (2 * SLAB)):
            lo = p * 2 * SLAB
            for c0 in range(0, TM, TL):
                cols = slice(c0, c0 + TL)
                s = s_ref[lo:lo + 2 * SLAB, cols]
                act = 0.5 * s * (1.0 + lax.erf(s * SQRT_HALF))
                weights = []
                for half in range(2):
                    a = k * (EB // SLAB) + p * 2 + half
                    w = jnp.zeros((SLAB, TL), BF16)
                    for h in range(PEER_HEADS):
                        chosen = r2_ref[h, :, cols] < row_tile(na_ref, h, a, cols)
                        w = w + jnp.where(chosen, e2_ref[h, :, cols], zero_bf) * row_tile(e1_ref, h, a, cols)
                    weights.append(w)
                aw_ref[lo:lo + 2 * SLAB, cols] = act * jnp.concatenate(weights, axis=0)

    def scores(half):
        return _bf(_dot(u_ref[half * EB:(half + 1) * EB, :], ht_scr[...]))

    def outputs(half, aw_ref):
        return _dot(vt_ref[:, half * EB:(half + 1) * EB], aw_ref[...])

    @pl.when(jj == 0)
    def _():
        ht_scr[...] = _bf(h_ref[...].astype(F32).T)
        s0_scr[...] = scores(0)
        s1_scr[...] = scores(1)
        vpu_stage(0, s0_scr, aw0_scr)
        acc_t[...] = jnp.zeros_like(acc_t)

    @pl.when(jnp.logical_and(jj > 0, jj < last))
    def _():
        out0 = outputs(0, aw0_scr)
        vpu_stage(2 * jj - 1, s1_scr, aw1_scr)
        s0_scr[...] = scores(0)
        out1 = outputs(1, aw1_scr)
        s1_scr[...] = scores(1)
        vpu_stage(2 * jj, s0_scr, aw0_scr)
        acc_t[...] += out0 + out1

    @pl.when(jj == last)
    def _():
        out0 = outputs(0, aw0_scr)
        vpu_stage(n_blocks - 1, s1_scr, aw1_scr)
        out1 = outputs(1, aw1_scr)
        xn = x_ref[...] + gt_ref[0] * (acc_t[...] + out0 + out1).T
        o_ref[...] = _rms(xn, fg_ref[...]) if final_norm else xn


def _peer_dense(x, hn, sel, u, vt, gt2, TM, tiles_per_batch, final_g):
    T, D = x.shape
    E = u.shape[0]
    EB = 1024
    final_norm = final_g is not None
    fg = (final_g if final_norm else jnp.ones((D,), F32)).reshape(1, D)
    sspec = pl.BlockSpec((PEER_HEADS, PEER_NKEYS, TM), lambda i, e: (0, 0, i))
    nE = E // EB
    nP = nE // 2
    kern = functools.partial(_peer_dense_kernel, block=EB, n_blocks=nE, final_norm=final_norm)
    return pl.pallas_call(
        kern,
        grid=(T // TM, nP + 1),
        in_specs=[pl.BlockSpec((TM, D), lambda i, e: (i, 0)),
                  sspec, sspec, sspec, sspec,
                  pl.BlockSpec((2 * EB, D), lambda i, e: (jnp.minimum(e, nP - 1), 0)),
                  pl.BlockSpec((D, 2 * EB), lambda i, e: (0, jnp.maximum(e - 1, 0))),
                  pl.BlockSpec((TM, D), lambda i, e: (i, 0)),
                  pl.BlockSpec((1, 1, D), lambda i, e: (i // tiles_per_batch, 0, 0)),
                  pl.BlockSpec((1, D), lambda i, e: (0, 0))],
        out_specs=pl.BlockSpec((TM, D), lambda i, e: (i, 0)),
        out_shape=jax.ShapeDtypeStruct((T, D), F32),
        scratch_shapes=[pltpu.VMEM((D, TM), F32), pltpu.VMEM((D, TM), BF16),
                        pltpu.VMEM((EB, TM), BF16), pltpu.VMEM((EB, TM), BF16),
                        pltpu.VMEM((EB, TM), BF16), pltpu.VMEM((EB, TM), BF16)],
        compiler_params=_cp("parallel", "arbitrary"),
        name="peer_dense",
    )(hn, *sel, u, vt, x, gt2, fg)


def _direction_masks():
    t = jnp.arange(CHUNK)
    le = (t[None, :] <= t[:, None]).astype(F32)
    tri = jnp.stack([le, le.T])
    i = jnp.arange(GROUP_W)
    same = (i[:, None] // CHUNK) == (i[None, :] // CHUNK)
    ti, tj = i[:, None] % CHUNK, i[None, :] % CHUNK
    msl = jnp.stack([same & (tj < ti), same & (tj > ti)]).astype(F32)
    minc = jnp.stack([same & (tj <= ti), same & (tj >= ti)]).astype(F32)
    return tri, msl, minc


def _layer_params(l, w_in, shift_mu, w0, w2, a0, a2, g2, k_k, k_a, r_k, lnx_g, lnx_b, w_oA, conv_w,
                  cnorm_g, cnorm_b, w_oB, gate_b, w_out, norm2_g):
    D = D_MODEL
    row = lambda t: t.reshape(1, -1)
    zeros = jnp.zeros((LORA_W, D), F32)
    tri, msl, minc = _direction_masks()
    hs = (jnp.arange(D)[:, None] // HEAD == jnp.arange(128)[None, :]).astype(F32)
    pad_cols = P_RWKV_PAD - P_RWKV
    return dict(
        w_rk=_bf(jnp.pad(w_in[l][:, :P_RWKV], ((0, 0), (0, pad_cols)))),
        w_cg=_bf(w_in[l][:, P_RWKV:]),
        mu=jnp.pad(shift_mu[l], ((0, 0), (0, pad_cols))),
        w0=w0[l].reshape(2, 1, D), a0=a0[l].reshape(2, 1, D),
        w2p=_split(jnp.stack([jnp.concatenate([w2[l, 0], zeros]), jnp.concatenate([zeros, w2[l, 1]])])),
        a2p=_split(jnp.stack([jnp.concatenate([a2[l, 0], zeros]), jnp.concatenate([zeros, a2[l, 1]])])),
        g2p=_bf(jnp.pad(g2[l], ((0, LORA_G_PAD - LORA_G), (0, 0)))),
        k_k=row(k_k[l]), k_a=row(k_a[l]), r_k=row(r_k[l]),
        lnx_g=row(lnx_g[l]), lnx_b=row(lnx_b[l]), w_oA=_bf(w_oA[l]),
        conv_w=conv_w[l], cnorm_g=row(cnorm_g[l]), cnorm_b=row(cnorm_b[l]), w_oB=_bf(w_oB[l]),
        gate_b=row(gate_b[l]), w_out=_bf(w_out[l]), norm2_g=row(norm2_g[l]),
        hs=_bf(hs), hb=_bf(hs.T), tri=_bf(tri), msl=msl, minc=minc)


def _mixer(x, mod, pr, norm1_g, stride, h0, emit):
    B, L, D = x.shape
    sh1, sc1, gt1, sh2, sc2 = (mod[:, i:i + 1, :] for i in range(5))
    z_rk = _proj_shift(x, sh1, sc1, norm1_g, pr["w_rk"], pr["mu"])
    y, bv, gs, h_t = _rwkv(z_rk, pr, h0)
    if not emit:
        return None, None, h_t
    z_cg = _proj(x, sh1, sc1, norm1_g, pr["w_cg"])
    cv = _conv(z_cg, pr["conv_w"], stride)
    xn, hn = _post(x, y, bv, gs, cv, z_cg, pr, gt1, sh2, sc2)
    return xn, hn, h_t


def _peer(x, hn, wq, keys, u, v, gt2, final_g=None):
    B, L, D = x.shape
    T = B * L
    sel = _peer_prep(hn.reshape(T, D), wq, keys, min(L, PEER_LANE_TILE))
    TM = min(L, PEER_DENSE_TOKENS)
    out = _peer_dense(x.reshape(T, D), hn.reshape(T, D), sel, u, v, gt2, TM, L // TM, final_g)
    return out.reshape(B, L, D)


def kernel(x, c, ctx, c_ctx, ada_w, ada_b, norm1_g, norm2_g, w_in, shift_mu, w0, w2, a0, a2, g2, k_k, k_a, r_k, lnx_g, lnx_b, w_oA, conv_w, cnorm_g, cnorm_b, w_oB, gate_b, w_out, w_q, sub_keys, peer_u, peer_v, final_g):
    B, L, D = x.shape
    depth = ada_w.shape[0]
    xc = ctx
    n_rows = -(-(B + 1) // 8) * 8
    c_rows = jnp.pad(jnp.concatenate([c, c_ctx[None, :]], axis=0), ((0, n_rows - B - 1), (0, 0)))
    zero_state = jnp.zeros((2, B, N_GROUPS, HEAD, GROUP_W), F32)
    for l in range(depth):
        last = l == depth - 1
        pr = _layer_params(l, w_in, shift_mu, w0, w2, a0, a2, g2, k_k, k_a, r_k, lnx_g, lnx_b, w_oA,
                           conv_w, cnorm_g, cnorm_b, w_oB, gate_b, w_out, norm2_g)
        mod_all = _modulation(c_rows, ada_w[l], ada_b[l])
        mod = mod_all[:B].reshape(B, 6, D)
        modc = jnp.broadcast_to(mod_all[B].reshape(1, 6, D), (B, 6, D))
        wq = _bf(w_q[l])
        keys = _bf(sub_keys[l])
        u = _bf(peer_u[l])
        v = _bf(peer_v[l]).T

        xc_new, hnc, ctx_states = _mixer(xc, modc, pr, norm1_g[l], 1, zero_state, emit=not last)
        xn, hn, _ = _mixer(x, mod, pr, norm1_g[l], GRID_W, ctx_states, emit=True)
        x = _peer(xn, hn, wq, keys, u, v, mod[:, 5:6, :], final_g if last else None)
        if not last:
            xc = _peer(xc_new, hnc, wq, keys, u, v, modc[:, 5:6, :])
    return x
```
